```python
import math
import jax, jax.numpy as jnp
from jax import lax
import numpy as np

D_MODEL = 1024
BATCH = 16
SEQ = 2048
DEPTH = 1

CHUNK = 64
MIX_WIDTH = D_MODEL
POOL_WIDTH = MIX_WIDTH // 2
POOL_WINDOWS = (2, 4, 8, 16)
POOL_GROUPS = len(POOL_WINDOWS)
POOL_GROUP_DIM = POOL_WIDTH // POOL_GROUPS
DN_WIDTH = MIX_WIDTH - POOL_WIDTH
DN_HEADS = 4
DN_HEAD_DIM = DN_WIDTH // DN_HEADS
CONV_K = 4
IN_COLS = POOL_WIDTH + 4 * DN_WIDTH + 2 * DN_HEADS
MEM_LEN = 256
XA_HEADS = 4
XA_HEAD_DIM = D_MODEL // XA_HEADS
D_FF = 4 * D_MODEL
EPS = 1e-6

kernel_name = "hymba_pool_gdn_memxattn_layer"


def rmsnorm(x, g):
    xf = x.astype(jnp.float32)
    y = xf * lax.rsqrt(jnp.mean(xf * xf, axis=-1, keepdims=True) + EPS)
    return (y * g.astype(jnp.float32)).astype(x.dtype)


def l2norm(x):
    return x * lax.rsqrt(jnp.sum(x * x, axis=-1, keepdims=True) + EPS)


def pool_mixer(u, w_pool, pool_scale):
    B, S, _ = u.shape
    uf = u.astype(jnp.float32)
    csum = jnp.concatenate([jnp.zeros((B, 1, POOL_WIDTH), jnp.float32),
                            jnp.cumsum(uf, axis=1)], axis=1)
    t = jnp.arange(S)
    outs = []
    for gi, w in enumerate(POOL_WINDOWS):
        sl = slice(gi * POOL_GROUP_DIM, (gi + 1) * POOL_GROUP_DIM)
        c = csum[:, :, sl]
        lagged = jnp.concatenate([jnp.zeros((B, w - 1, POOL_GROUP_DIM), jnp.float32),
                                  c[:, :S - w + 1]], axis=1)
        cnt = jnp.minimum(t + 1, w).astype(jnp.float32)[None, :, None]
        outs.append((c[:, 1:] - lagged) / cnt - uf[:, :, sl])
    pooled = jnp.stack(outs, axis=2)
    mixed = jnp.einsum('bsgc,gcd->bsgd', pooled, w_pool.astype(jnp.float32))
    return (mixed.reshape(B, S, POOL_WIDTH) * pool_scale.astype(jnp.float32)).astype(u.dtype)


def causal_dwconv(x, w):
    K, C = w.shape
    return lax.conv_general_dilated(x, w[:, None, :], window_strides=(1,),
                                    padding=[(K - 1, 0)],
                                    dimension_numbers=('NWC', 'WIO', 'NWC'),
                                    feature_group_count=C)


def gated_delta_rule(q, k, v, g, beta):
    B, S, H, Dk = q.shape
    Dv = v.shape[-1]
    NC = S // CHUNK
    q = q * (Dk ** -0.5)

    def chunks(x):
        return x.reshape(B, NC, CHUNK, H, x.shape[-1]).transpose(0, 3, 1, 2, 4)

    qc, kc, vc = chunks(q), chunks(k), chunks(v)
    gc = jnp.cumsum(g.reshape(B, NC, CHUNK, H).transpose(0, 3, 1, 2), axis=-1)
    bc = beta.reshape(B, NC, CHUNK, H).transpose(0, 3, 1, 2)[..., None]

    tril = jnp.tril(jnp.ones((CHUNK, CHUNK), dtype=bool))
    strict = jnp.tril(jnp.ones((CHUNK, CHUNK), dtype=bool), k=-1)
    diff = gc[..., :, None] - gc[..., None, :]
    decay = jnp.where(tril, jnp.exp(jnp.where(tril, diff, 0.0)), 0.0)

    k_beta = kc * bc
    v_beta = vc * bc
    L = jnp.where(strict, jnp.einsum('bhnid,bhnjd->bhnij', k_beta, kc) * decay, 0.0)
    eye = jnp.eye(CHUNK, dtype=jnp.float32)
    T = lax.linalg.triangular_solve(eye + L, jnp.broadcast_to(eye, L.shape),
                                    left_side=True, lower=True)
    u = jnp.einsum('bhnij,bhnjd->bhnid', T, v_beta)
    w = jnp.einsum('bhnij,bhnjd->bhnid', T, k_beta * jnp.exp(gc)[..., None])
    attn = jnp.where(tril, jnp.einsum('bhnid,bhnjd->bhnij', qc, kc) * decay, 0.0)

    def step(state, inp):
        q_c, k_c, u_c, w_c, g_c, a_c = inp
        v_new = u_c - jnp.einsum('bhcd,bhde->bhce', w_c, state)
        o = (jnp.einsum('bhcd,bhde->bhce', q_c * jnp.exp(g_c)[..., None], state)
             + jnp.einsum('bhij,bhje->bhie', a_c, v_new))
        g_last = g_c[..., -1]
        k_dec = k_c * jnp.exp(g_last[..., None] - g_c)[..., None]
        state = state * jnp.exp(g_last)[..., None, None] + jnp.einsum('bhcd,bhce->bhde', k_dec, v_new)
        return state, o

    xs = (jnp.moveaxis(qc, 2, 0), jnp.moveaxis(kc, 2, 0), jnp.moveaxis(u, 2, 0),
          jnp.moveaxis(w, 2, 0), jnp.moveaxis(gc, 2, 0), jnp.moveaxis(attn, 2, 0))
    state0 = jnp.zeros((B, H, Dk, Dv), jnp.float32)
    _, o = lax.scan(step, state0, xs)
    return o.transpose(1, 0, 3, 2, 4).reshape(B, S, H, Dv)


def deltanet_mixer(q, k, v, z, b, a, conv_w, a_log, dt_bias, o_norm_g):
    B, S, _ = q.shape
    dtype = q.dtype
    qkv = jnp.concatenate([q, k, v], axis=-1).astype(jnp.float32)
    qkv = jax.nn.silu(causal_dwconv(qkv, conv_w.astype(jnp.float32)))
    qf, kf, vf = jnp.split(qkv, 3, axis=-1)
    qf = l2norm(qf.reshape(B, S, DN_HEADS, DN_HEAD_DIM))
    kf = l2norm(kf.reshape(B, S, DN_HEADS, DN_HEAD_DIM))
    vf = vf.reshape(B, S, DN_HEADS, DN_HEAD_DIM)
    beta = jax.nn.sigmoid(b.astype(jnp.float32))
    g = -jnp.exp(a_log.astype(jnp.float32)) * jax.nn.softplus(a.astype(jnp.float32) + dt_bias.astype(jnp.float32))
    o = gated_delta_rule(qf, kf, vf, g, beta)
    o = o * lax.rsqrt(jnp.mean(o * o, axis=-1, keepdims=True) + EPS) * o_norm_g.astype(jnp.float32)
    o = o * jax.nn.silu(z.astype(jnp.float32).reshape(B, S, DN_HEADS, DN_HEAD_DIM))
    return o.reshape(B, S, DN_WIDTH).astype(dtype)


def memory_cross_attention(h, m, w_q, w_k, w_v, w_o):
    B, S, _ = h.shape
    M = m.shape[1]
    q = (h @ w_q).reshape(B, S, XA_HEADS, XA_HEAD_DIM)
    k = (m @ w_k).reshape(B, M, XA_HEADS, XA_HEAD_DIM)
    v = (m @ w_v).reshape(B, M, XA_HEADS, XA_HEAD_DIM)
    s = jnp.einsum('bshd,bmhd->bhsm', q, k).astype(jnp.float32) * (XA_HEAD_DIM ** -0.5)
    p = jax.nn.softmax(s, axis=-1).astype(v.dtype)
    o = jnp.einsum('bhsm,bmhd->bshd', p, v).reshape(B, S, D_MODEL)
    return o @ w_o


def sqrelu_mlp(h, w_up, w_down):
    a = jax.nn.relu(h @ w_up)
    return (a * a) @ w_down


def setup_inputs(seed: int = 0) -> dict:
    key = jax.random.key(seed)
    ks = jax.random.split(key, 24)

    def nrm(k, shape, scale):
        return jax.random.normal(k, shape, jnp.float32) * scale

    def gain(k, shape):
        return 1.0 + 0.05 * jax.random.normal(k, shape, jnp.float32)

    L = DEPTH
    return {
        "x": nrm(ks[0], (BATCH, SEQ, D_MODEL), 1.0),
        "mem": nrm(ks[1], (BATCH, MEM_LEN, D_MODEL), 1.0),
        "norm_mix_g": gain(ks[2], (L, D_MODEL)),
        "w_in": nrm(ks[3], (L, D_MODEL, IN_COLS), D_MODEL ** -0.5),
        "w_pool": nrm(ks[4], (L, POOL_GROUPS, POOL_GROUP_DIM, POOL_GROUP_DIM), POOL_GROUP_DIM ** -0.5),
        "pool_scale": gain(ks[5], (L, POOL_WIDTH)),
        "conv_w": nrm(ks[6], (L, CONV_K, 3 * DN_WIDTH), CONV_K ** -0.5),
        "a_log": jnp.log(jax.random.uniform(ks[7], (L, DN_HEADS), jnp.float32, 1.0, 16.0)),
        "dt_bias": 0.5 + 0.1 * jax.random.normal(ks[8], (L, DN_HEADS), jnp.float32),
        "dn_out_norm_g": gain(ks[9], (L, DN_HEAD_DIM)),
        "w_out": nrm(ks[10], (L, MIX_WIDTH, D_MODEL), MIX_WIDTH ** -0.5),
        "norm_xattn_g": gain(ks[11], (L, D_MODEL)),
        "mem_norm_g": gain(ks[12], (L, D_MODEL)),
        "w_xq": nrm(ks[13], (L, D_MODEL, D_MODEL), D_MODEL ** -0.5),
        "w_xk": nrm(ks[14], (L, D_MODEL, D_MODEL), D_MODEL ** -0.5),
        "w_xv": nrm(ks[15], (L, D_MODEL, D_MODEL), D_MODEL ** -0.5),
        "w_xo": nrm(ks[16], (L, D_MODEL, D_MODEL), D_MODEL ** -0.5),
        "norm_mlp_g": gain(ks[17], (L, D_MODEL)),
        "w_up": nrm(ks[18], (L, D_MODEL, D_FF), D_MODEL ** -0.5),
        "w_down": nrm(ks[19], (L, D_FF, D_MODEL), D_FF ** -0.5),
        "final_norm_g": gain(ks[20], (D_MODEL,)),
    }


def reference(x, mem, norm_mix_g, w_in, w_pool, pool_scale, conv_w, a_log, dt_bias,
              dn_out_norm_g, w_out, norm_xattn_g, mem_norm_g, w_xq, w_xk, w_xv, w_xo,
              norm_mlp_g, w_up, w_down, final_norm_g):
    P, W, H = POOL_WIDTH, DN_WIDTH, DN_HEADS
    h = x
    for l in range(DEPTH):
        hn = rmsnorm(h, norm_mix_g[l])
        proj = hn @ w_in[l]
        u_pool = proj[..., :P]
        q = proj[..., P:P + W]
        k = proj[..., P + W:P + 2 * W]
        v = proj[..., P + 2 * W:P + 3 * W]
        z = proj[..., P + 3 * W:P + 4 * W]
        b = proj[..., P + 4 * W:P + 4 * W + H]
        a = proj[..., P + 4 * W + H:]
        y_pool = pool_mixer(u_pool, w_pool[l], pool_scale[l])
        y_dn = deltanet_mixer(q, k, v, z, b, a, conv_w[l], a_log[l], dt_bias[l], dn_out_norm_g[l])
        h = h + jnp.concatenate([y_pool, y_dn], axis=-1) @ w_out[l]
        m = rmsnorm(mem, mem_norm_g[l])
        h = h + memory_cross_attention(rmsnorm(h, norm_xattn_g[l]), m, w_xq[l], w_xk[l], w_xv[l], w_xo[l])
        h = h + sqrelu_mlp(rmsnorm(h, norm_mlp_g[l]), w_up[l], w_down[l])
    return rmsnorm(h, final_norm_g)
```

```python
import functools

import jax
import jax.numpy as jnp
from jax import lax
from jax.experimental import pallas as pl
from jax.experimental.pallas import tpu as pltpu

D_MODEL = 1024
POOL_WIDTH = 512
POOL_WINDOWS = (2, 4, 8, 16)
POOL_GROUP_DIM = 128
DN_WIDTH = 512
DN_HEADS = 4
DN_HEAD_DIM = 128
CONV_K = 4
MAIN_COLS = POOL_WIDTH + 4 * DN_WIDTH
QKV_OFF = POOL_WIDTH
Z_OFF = POOL_WIDTH + 3 * DN_WIDTH
XA_HEADS = 4
XA_HEAD_DIM = 256
D_FF = 4096
EPS = 1e-6

LANES = 128
GDN_CHUNK = 128
POOL_HALO = 16
CONV_HALO = 8
TM_PROJ = 512
TM_MIX = 512
TM_POST = 512
FF_BLOCK = 1024
VMEM_LIMIT = 56 * 1024 * 1024

F32 = jnp.float32
BF16 = jnp.bfloat16


def _dot(a, b):
    return jnp.dot(a, b, preferred_element_type=F32)


def _dot_nt(a, b):
    return lax.dot_general(a, b, (((1,), (1,)), ((), ())), preferred_element_type=F32)


def _rms(x, g):
    return x * lax.rsqrt(jnp.mean(x * x, axis=-1, keepdims=True) + EPS) * g


def _silu(x):
    return x * jax.nn.sigmoid(x)


def _const_spec(shape):
    zeros = (0,) * len(shape)
    return pl.BlockSpec(shape, lambda *_: zeros, pipeline_mode=pl.Buffered(1))


def _in_proj_kernel(x_ref, g_ref, wm_ref, wba_ref, main_ref, ba_ref):
    hn = _rms(x_ref[...], g_ref[...]).astype(BF16)
    main_ref[...] = _dot(hn, wm_ref[...])
    ba_ref[...] = _dot(hn, wba_ref[...])


def _in_proj(x2, g, w_main, w_ba):
    t = x2.shape[0]
    return pl.pallas_call(
        _in_proj_kernel,
        grid=(t // TM_PROJ,),
        in_specs=[
            pl.BlockSpec((TM_PROJ, D_MODEL), lambda i: (i, 0)),
            _const_spec((1, D_MODEL)),
            _const_spec((D_MODEL, MAIN_COLS)),
            _const_spec((D_MODEL, LANES)),
        ],
        out_specs=[
            pl.BlockSpec((TM_PROJ, MAIN_COLS), lambda i: (i, 0)),
            pl.BlockSpec((TM_PROJ, LANES), lambda i: (i, 0)),
        ],
        out_shape=[
            jax.ShapeDtypeStruct((t, MAIN_COLS), F32),
            jax.ShapeDtypeStruct((t, LANES), F32),
        ],
        compiler_params=pltpu.CompilerParams(
            dimension_semantics=("arbitrary",), vmem_limit_bytes=VMEM_LIMIT),
        name="in_proj",
    )(x2, g, w_main, w_ba)


def _mixer_kernel(pm_ref, ba_ref, wpool_ref, pscale_ref, convw_ref, gp_ref, og_ref, y_ref,
                  extp_ref, extq_ref, grow_ref, gcol_ref, state_ref):
    tm = TM_MIX
    c = GDN_CHUNK
    s_idx = pl.program_id(1)

    @pl.when(s_idx == 0)
    def _():
        extp_ref[0:POOL_HALO, :] = jnp.zeros((POOL_HALO, POOL_WIDTH), F32)
        extq_ref[0:CONV_HALO, :] = jnp.zeros((CONV_HALO, 3 * DN_WIDTH), F32)
        state_ref[...] = jnp.zeros_like(state_ref)

    extp_ref[POOL_HALO:POOL_HALO + tm, :] = pm_ref[:, 0:POOL_WIDTH]
    extq_ref[CONV_HALO:CONV_HALO + tm, :] = pm_ref[:, QKV_OFF:QKV_OFF + 3 * DN_WIDTH]

    t_glob = lax.broadcasted_iota(jnp.int32, (tm, POOL_GROUP_DIM), 0) + s_idx * tm
    for gi, win in enumerate(POOL_WINDOWS):
        lo = gi * POOL_GROUP_DIM
        u = extp_ref[POOL_HALO:POOL_HALO + tm, lo:lo + POOL_GROUP_DIM]
        acc = u
        for j in range(1, win):
            acc = acc + extp_ref[POOL_HALO - j:POOL_HALO - j + tm, lo:lo + POOL_GROUP_DIM]
        cnt = jnp.minimum(t_glob + 1, win).astype(F32)
        pooled = acc / cnt - u
        mixed = _dot(pooled.astype(BF16), wpool_ref[gi])
        y_ref[:, lo:lo + POOL_GROUP_DIM] = (mixed * pscale_ref[:, lo:lo + POOL_GROUP_DIM]).astype(y_ref.dtype)

    bat = ba_ref[...].T[0:8, :]
    gp = gp_ref[...]
    a_log = gp[:, 0:1]
    dt_bias = gp[:, 1:2]
    xa = bat + dt_bias
    softplus = jnp.maximum(xa, 0.0) + jnp.log1p(jnp.exp(-jnp.abs(xa)))
    cs = -jnp.exp(a_log) * softplus
    lane = lax.broadcasted_iota(jnp.int32, (8, tm), 1) & (c - 1)
    sh = 1
    while sh < c:
        cs = cs + jnp.where(lane >= sh, pltpu.roll(cs, sh, axis=1), 0.0)
        sh *= 2
    row8 = lax.broadcasted_iota(jnp.int32, (8, tm), 0)
    stack = jnp.where(row8 < DN_HEADS, jax.nn.sigmoid(bat), cs)
    grow_ref[...] = stack
    gcol_ref[...] = jnp.concatenate([stack, jnp.zeros((LANES - 8, tm), F32)], axis=0).T

    ii = lax.broadcasted_iota(jnp.int32, (c, c), 0)
    jj = lax.broadcasted_iota(jnp.int32, (c, c), 1)
    tril = ii >= jj
    strict = ii > jj
    eye = jnp.where(ii == jj, 1.0, 0.0).astype(F32)
    q_scale = DN_HEAD_DIM ** -0.5

    for ci in range(tm // c):
        r0 = ci * c
        xc = convw_ref[0:1, :] * extq_ref[CONV_HALO - 3 + r0:CONV_HALO - 3 + r0 + c, :]
        for j in range(1, CONV_K):
            off = CONV_HALO - (CONV_K - 1) + j + r0
            xc = xc + convw_ref[j:j + 1, :] * extq_ref[off:off + c, :]
        act = _silu(xc)
        for h in range(DN_HEADS):
            lo = h * DN_HEAD_DIM
            qh = act[:, lo:lo + DN_HEAD_DIM]
            kh = act[:, DN_WIDTH + lo:DN_WIDTH + lo + DN_HEAD_DIM]
            vh = act[:, 2 * DN_WIDTH + lo:2 * DN_WIDTH + lo + DN_HEAD_DIM]
            qn = qh * lax.rsqrt(jnp.sum(qh * qh, axis=-1, keepdims=True) + EPS) * q_scale
            kn = kh * lax.rsqrt(jnp.sum(kh * kh, axis=-1, keepdims=True) + EPS)
            beta_c = gcol_ref[r0:r0 + c, h:h + 1]
            gc_c = gcol_ref[r0:r0 + c, DN_HEADS + h:DN_HEADS + h + 1]
            gc_r = grow_ref[DN_HEADS + h:DN_HEADS + h + 1, r0:r0 + c]
            g_last = gcol_ref[r0 + c - 1:r0 + c, DN_HEADS + h:DN_HEADS + h + 1]

            diff = gc_c - gc_r
            decay = jnp.where(tril, jnp.exp(jnp.where(tril, diff, 0.0)), 0.0)
            kb = kn * beta_c
            vb = vh * beta_c
            k16 = kn.astype(BF16)
            lmat = jnp.where(strict, _dot_nt(kb.astype(BF16), k16) * decay, 0.0)
            xp = -lmat
            tinv = eye + xp
            n_sq = 1
            while 2 * n_sq < c:
                x16 = xp.astype(BF16)
                xp = _dot(x16, x16)
                tinv = tinv + _dot(tinv.astype(BF16), xp.astype(BF16))
                n_sq *= 2
            t16 = tinv.astype(BF16)
            eg = jnp.exp(gc_c)
            u = _dot(t16, vb.astype(BF16))
            w = _dot(t16, (kb * eg).astype(BF16))
            attn = jnp.where(tril, _dot_nt(qn.astype(BF16), k16) * decay, 0.0)

            st = state_ref[h]
            st16 = st.astype(BF16)
            v_new = u - _dot(w.astype(BF16), st16)
            vn16 = v_new.astype(BF16)
            o = _dot((qn * eg).astype(BF16), st16) + _dot(attn.astype(BF16), vn16)
            kdec_t = (kn * jnp.exp(g_last - gc_c)).T
            state_ref[h] = st * jnp.exp(g_last) + _dot(kdec_t.astype(BF16), vn16)

            on = o * lax.rsqrt(jnp.mean(o * o, axis=-1, keepdims=True) + EPS) * og_ref[...]
            zh = pm_ref[r0:r0 + c, Z_OFF + lo:Z_OFF + lo + DN_HEAD_DIM]
            y_ref[r0:r0 + c, POOL_WIDTH + lo:POOL_WIDTH + lo + DN_HEAD_DIM] = (on * _silu(zh)).astype(y_ref.dtype)

    extp_ref[0:POOL_HALO, :] = extp_ref[tm:tm + POOL_HALO, :]
    extq_ref[0:CONV_HALO, :] = extq_ref[tm:tm + CONV_HALO, :]


def _mixer(main3, ba3, w_pool, pool_scale, conv_w, gate_params, o_norm_g):
    b, s, _ = main3.shape
    return pl.pallas_call(
        _mixer_kernel,
        grid=(b, s // TM_MIX),
        in_specs=[
            pl.BlockSpec((None, TM_MIX, MAIN_COLS), lambda bi, si: (bi, si, 0)),
            pl.BlockSpec((None, TM_MIX, LANES), lambda bi, si: (bi, si, 0)),
            _const_spec((len(POOL_WINDOWS), POOL_GROUP_DIM, POOL_GROUP_DIM)),
            _const_spec((1, POOL_WIDTH)),
            _const_spec((CONV_K, 3 * DN_WIDTH)),
            _const_spec((8, LANES)),
            _const_spec((1, DN_HEAD_DIM)),
        ],
        out_specs=pl.BlockSpec((None, TM_MIX, D_MODEL), lambda bi, si: (bi, si, 0)),
        out_shape=jax.ShapeDtypeStruct((b, s, D_MODEL), BF16),
        scratch_shapes=[
            pltpu.VMEM((POOL_HALO + TM_MIX, POOL_WIDTH), F32),
            pltpu.VMEM((CONV_HALO + TM_MIX, 3 * DN_WIDTH), F32),
            pltpu.VMEM((8, TM_MIX), F32),
            pltpu.VMEM((TM_MIX, LANES), F32),
            pltpu.VMEM((DN_HEADS, DN_HEAD_DIM, DN_HEAD_DIM), F32),
        ],
        compiler_params=pltpu.CompilerParams(
            dimension_semantics=("arbitrary", "arbitrary"), vmem_limit_bytes=VMEM_LIMIT),
        name="mixer",
    )(main3, ba3, w_pool, pool_scale, conv_w, gate_params, o_norm_g)


def _mem_kv_kernel(m_ref, g_ref, wk_ref, wv_ref, k_ref, v_ref):
    mn = _rms(m_ref[...], g_ref[...]).astype(BF16)
    k_ref[...] = _dot(mn, wk_ref[...]).astype(k_ref.dtype)
    v_ref[...] = _dot(mn, wv_ref[...]).astype(v_ref.dtype)


def _mem_kv(mem2, g, w_k, w_v):
    t = mem2.shape[0]
    tm = 512
    return pl.pallas_call(
        _mem_kv_kernel,
        grid=(t // tm,),
        in_specs=[
            pl.BlockSpec((tm, D_MODEL), lambda i: (i, 0)),
            _const_spec((1, D_MODEL)),
            _const_spec((D_MODEL, D_MODEL)),
            _const_spec((D_MODEL, D_MODEL)),
        ],
        out_specs=[
            pl.BlockSpec((tm, D_MODEL), lambda i: (i, 0)),
            pl.BlockSpec((tm, D_MODEL), lambda i: (i, 0)),
        ],
        out_shape=[jax.ShapeDtypeStruct((t, D_MODEL), BF16)] * 2,
        compiler_params=pltpu.CompilerParams(
            dimension_semantics=("arbitrary",), vmem_limit_bytes=VMEM_LIMIT),
        name="mem_kv",
    )(mem2, g, w_k, w_v)


def _post_kernel(x_ref, y_ref, k_ref, v_ref, wout_ref, gxa_ref, wq_ref, wo_ref, gmlp_ref,
                 wup_ref, wdn_ref, gfin_ref, o_ref):
    h1 = x_ref[...] + _dot(y_ref[...], wout_ref[...])
    q = _dot(_rms(h1, gxa_ref[...]).astype(BF16), wq_ref[...])
    heads = []
    for hd in range(XA_HEADS):
        lo = hd * XA_HEAD_DIM
        s = _dot_nt(q[:, lo:lo + XA_HEAD_DIM].astype(BF16), k_ref[:, lo:lo + XA_HEAD_DIM])
        s = s * (XA_HEAD_DIM ** -0.5)
        e = jnp.exp(s - jnp.max(s, axis=-1, keepdims=True))
        p = e / jnp.sum(e, axis=-1, keepdims=True)
        heads.append(_dot(p.astype(BF16), v_ref[:, lo:lo + XA_HEAD_DIM]))
    att = jnp.concatenate(heads, axis=-1)
    h2 = h1 + _dot(att.astype(BF16), wo_ref[...])
    hn2 = _rms(h2, gmlp_ref[...]).astype(BF16)
    acc = h2
    for j in range(D_FF // FF_BLOCK):
        a = jnp.maximum(_dot(hn2, wup_ref[:, j * FF_BLOCK:(j + 1) * FF_BLOCK]), 0.0)
        acc = acc + _dot((a * a).astype(BF16), wdn_ref[j * FF_BLOCK:(j + 1) * FF_BLOCK, :])
    o_ref[...] = _rms(acc, gfin_ref[...])


def _post(x2, y2, k3, v3, w_out, g_xa, w_q, w_o, g_mlp, w_up, w_dn, g_fin, seq):
    t = x2.shape[0]
    mem_len = k3.shape[1]
    tiles_per_batch = seq // TM_POST
    return pl.pallas_call(
        _post_kernel,
        grid=(t // TM_POST,),
        in_specs=[
            pl.BlockSpec((TM_POST, D_MODEL), lambda i: (i, 0)),
            pl.BlockSpec((TM_POST, D_MODEL), lambda i: (i, 0)),
            pl.BlockSpec((None, mem_len, D_MODEL), lambda i: (i // tiles_per_batch, 0, 0)),
            pl.BlockSpec((None, mem_len, D_MODEL), lambda i: (i // tiles_per_batch, 0, 0)),
            _const_spec((D_MODEL, D_MODEL)),
            _const_spec((1, D_MODEL)),
            _const_spec((D_MODEL, D_MODEL)),
            _const_spec((D_MODEL, D_MODEL)),
            _const_spec((1, D_MODEL)),
            _const_spec((D_MODEL, D_FF)),
            _const_spec((D_FF, D_MODEL)),
            _const_spec((1, D_MODEL)),
        ],
        out_specs=pl.BlockSpec((TM_POST, D_MODEL), lambda i: (i, 0)),
        out_shape=jax.ShapeDtypeStruct((t, D_MODEL), F32),
        compiler_params=pltpu.CompilerParams(
            dimension_semantics=("arbitrary",), vmem_limit_bytes=VMEM_LIMIT),
        name="post",
    )(x2, y2, k3, v3, w_out, g_xa, w_q, w_o, g_mlp, w_up, w_dn, g_fin)


def kernel(x, mem, norm_mix_g, w_in, w_pool, pool_scale, conv_w, a_log, dt_bias, dn_out_norm_g, w_out,
           norm_xattn_g, mem_norm_g, w_xq, w_xk, w_xv, w_xo, norm_mlp_g, w_up, w_down, final_norm_g):
    b, s, d = x.shape
    mem_len = mem.shape[1]
    depth = w_in.shape[0]
    row = lambda v: v.reshape(1, -1).astype(F32)

    h = x
    for l in range(depth):
        w_main = w_in[l][:, :MAIN_COLS].astype(BF16)
        w_ba = jnp.pad(w_in[l][:, MAIN_COLS:], ((0, 0), (0, LANES - 2 * DN_HEADS))).astype(BF16)
        gate_params = jnp.zeros((8, LANES), F32)
        gate_params = gate_params.at[DN_HEADS:, 0].set(a_log[l]).at[DN_HEADS:, 1].set(dt_bias[l])

        x2 = h.reshape(b * s, d)
        main, ba = _in_proj(x2, row(norm_mix_g[l]), w_main, w_ba)
        y = _mixer(main.reshape(b, s, MAIN_COLS), ba.reshape(b, s, LANES), w_pool[l].astype(BF16),
                   row(pool_scale[l]), conv_w[l].astype(F32), gate_params, row(dn_out_norm_g[l]))
        k, v = _mem_kv(mem.reshape(b * mem_len, d), row(mem_norm_g[l]),
                       w_xk[l].astype(BF16), w_xv[l].astype(BF16))
        g_fin = row(final_norm_g) if l == depth - 1 else None
        assert g_fin is not None, "only the last layer is followed by the final norm"
        h = _post(x2, y.reshape(b * s, d), k.reshape(b, mem_len, d), v.reshape(b, mem_len, d),
                  w_out[l].astype(BF16), row(norm_xattn_g[l]), w_xq[l].astype(BF16), w_xo[l].astype(BF16),
                  row(norm_mlp_g[l]), w_up[l].astype(BF16), w_down[l].astype(BF16), g_fin, s).reshape(b, s, d)
    return h
```

```python
import functools

import jax
import jax.numpy as jnp
from jax import lax
from jax.experimental import pallas as pl
from jax.experimental.pallas import tpu as pltpu

D_MODEL = 1024
POOL_WIDTH = 512
POOL_WINDOWS = (2, 4, 8, 16)
POOL_GROUP_DIM = 128
DN_WIDTH = 512
DN_HEADS = 4
DN_HEAD_DIM = 128
CONV_K = 4
MAIN_COLS = POOL_WIDTH + 4 * DN_WIDTH
QKV_OFF = POOL_WIDTH
Z_OFF = POOL_WIDTH + 3 * DN_WIDTH
XA_HEADS = 4
XA_HEAD_DIM = 256
D_FF = 4096
EPS = 1e-6

LANES = 128
GDN_CHUNK = 128
POOL_HALO = 16
CONV_HALO = 8
TM_PROJ = 512
TM_MIX = 512
CHUNKS_PER_SWEEP = 2
TM_POST = 512
FF_BLOCK = 1024
VMEM_LIMIT = 56 * 1024 * 1024

F32 = jnp.float32
BF16 = jnp.bfloat16


def _dot(a, b):
    return jnp.dot(a, b, preferred_element_type=F32)


def _dot_nt(a, b):
    return lax.dot_general(a, b, (((1,), (1,)), ((), ())), preferred_element_type=F32)


def _rms(x, g):
    return x * lax.rsqrt(jnp.mean(x * x, axis=-1, keepdims=True) + EPS) * g


def _silu(x):
    return x * jax.nn.sigmoid(x)


def _const_spec(shape):
    zeros = (0,) * len(shape)
    return pl.BlockSpec(shape, lambda *_: zeros, pipeline_mode=pl.Buffered(1))


def _in_proj_kernel(x_ref, g_ref, wm_ref, wba_ref, main_ref, ba_ref):
    hn = _rms(x_ref[...], g_ref[...]).astype(BF16)
    main_ref[...] = _dot(hn, wm_ref[...])
    ba_ref[...] = _dot(hn, wba_ref[...])


def _in_proj(x2, g, w_main, w_ba):
    t = x2.shape[0]
    return pl.pallas_call(
        _in_proj_kernel,
        grid=(t // TM_PROJ,),
        in_specs=[
            pl.BlockSpec((TM_PROJ, D_MODEL), lambda i: (i, 0)),
            _const_spec((1, D_MODEL)),
            _const_spec((D_MODEL, MAIN_COLS)),
            _const_spec((D_MODEL, LANES)),
        ],
        out_specs=[
            pl.BlockSpec((TM_PROJ, MAIN_COLS), lambda i: (i, 0)),
            pl.BlockSpec((TM_PROJ, LANES), lambda i: (i, 0)),
        ],
        out_shape=[
            jax.ShapeDtypeStruct((t, MAIN_COLS), F32),
            jax.ShapeDtypeStruct((t, LANES), F32),
        ],
        compiler_params=pltpu.CompilerParams(
            dimension_semantics=("arbitrary",), vmem_limit_bytes=VMEM_LIMIT),
        name="in_proj",
    )(x2, g, w_main, w_ba)


def _mixer_kernel(pm_ref, ba_ref, wpool_ref, pscale_ref, convw_ref, gp_ref, og_ref, y_ref,
                  extp_ref, extq_ref, grow_ref, gcol_ref, state_ref, u_ref, wq_ref, ak_ref):
    tm = TM_MIX
    c = GDN_CHUNK
    s_idx = pl.program_id(1)

    @pl.when(s_idx == 0)
    def _():
        extp_ref[0:POOL_HALO, :] = jnp.zeros((POOL_HALO, POOL_WIDTH), F32)
        extq_ref[0:CONV_HALO, :] = jnp.zeros((CONV_HALO, 3 * DN_WIDTH), F32)
        state_ref[...] = jnp.zeros_like(state_ref)

    extp_ref[POOL_HALO:POOL_HALO + tm, :] = pm_ref[:, 0:POOL_WIDTH]
    extq_ref[CONV_HALO:CONV_HALO + tm, :] = pm_ref[:, QKV_OFF:QKV_OFF + 3 * DN_WIDTH]

    t_glob = lax.broadcasted_iota(jnp.int32, (tm, POOL_GROUP_DIM), 0) + s_idx * tm
    for gi, win in enumerate(POOL_WINDOWS):
        lo = gi * POOL_GROUP_DIM
        u = extp_ref[POOL_HALO:POOL_HALO + tm, lo:lo + POOL_GROUP_DIM]
        acc = u
        for j in range(1, win):
            acc = acc + extp_ref[POOL_HALO - j:POOL_HALO - j + tm, lo:lo + POOL_GROUP_DIM]
        cnt = jnp.minimum(t_glob + 1, win).astype(F32)
        pooled = acc / cnt - u
        mixed = _dot(pooled.astype(BF16), wpool_ref[gi])
        y_ref[:, lo:lo + POOL_GROUP_DIM] = (mixed * pscale_ref[:, lo:lo + POOL_GROUP_DIM]).astype(y_ref.dtype)

    bat = ba_ref[...].T[0:8, :]
    gp = gp_ref[...]
    a_log = gp[:, 0:1]
    dt_bias = gp[:, 1:2]
    xa = bat + dt_bias
    softplus = jnp.maximum(xa, 0.0) + jnp.log1p(jnp.exp(-jnp.abs(xa)))
    cs = -jnp.exp(a_log) * softplus
    lane = lax.broadcasted_iota(jnp.int32, (8, tm), 1) & (c - 1)
    sh = 1
    while sh < c:
        cs = cs + jnp.where(lane >= sh, pltpu.roll(cs, sh, axis=1), 0.0)
        sh *= 2
    row8 = lax.broadcasted_iota(jnp.int32, (8, tm), 0)
    stack = jnp.where(row8 < DN_HEADS, jax.nn.sigmoid(bat), cs)
    grow_ref[...] = stack
    gcol_ref[...] = jnp.concatenate([stack, jnp.zeros((LANES - 8, tm), F32)], axis=0).T

    ii = lax.broadcasted_iota(jnp.int32, (c, c), 0)
    jj = lax.broadcasted_iota(jnp.int32, (c, c), 1)
    tril = ii >= jj
    strict = ii > jj
    eye = jnp.where(ii == jj, 1.0, 0.0).astype(F32)
    q_scale = DN_HEAD_DIM ** -0.5
    n_chunks = tm // c
    heads = range(DN_HEADS)

    def col(r0, j):
        return jnp.broadcast_to(gcol_ref[r0:r0 + c, j:j + 1], (c, c))

    for g0 in range(0, n_chunks, CHUNKS_PER_SWEEP):
        units = [(ci, h) for ci in range(g0, g0 + CHUNKS_PER_SWEEP) for h in heads]
        k16, lhs_kq, dec, neg_r, vbkg, qg16, kdt16 = [], [], [], [], [], [], []
        for ci in range(g0, g0 + CHUNKS_PER_SWEEP):
            r0 = ci * c
            base = CONV_HALO - (CONV_K - 1) + r0
            xc = convw_ref[0:1, :] * extq_ref[base:base + c, :]
            for j in range(1, CONV_K):
                xc = xc + convw_ref[j:j + 1, :] * extq_ref[base + j:base + j + c, :]
            act = _silu(xc)
            for h in heads:
                lo = h * DN_HEAD_DIM
                qh = act[:, lo:lo + DN_HEAD_DIM]
                kh = act[:, DN_WIDTH + lo:DN_WIDTH + lo + DN_HEAD_DIM]
                vh = act[:, 2 * DN_WIDTH + lo:2 * DN_WIDTH + lo + DN_HEAD_DIM]
                qn = qh * lax.rsqrt(jnp.sum(qh * qh, axis=-1, keepdims=True) + EPS) * q_scale
                kn = kh * lax.rsqrt(jnp.sum(kh * kh, axis=-1, keepdims=True) + EPS)
                beta_b = col(r0, h)
                gc_b = col(r0, DN_HEADS + h)
                gc_r = grow_ref[DN_HEADS + h:DN_HEADS + h + 1, r0:r0 + c]
                g_last = gcol_ref[r0 + c - 1:r0 + c, DN_HEADS + h:DN_HEADS + h + 1]
                d = jnp.where(tril, jnp.exp(gc_b - gc_r), 0.0)
                eg = jnp.exp(gc_b)
                kb = kn * beta_b
                dec.append(d)
                neg_r.append(jnp.where(strict, -d, 0.0))
                k16.append(kn.astype(BF16))
                lhs_kq.append(jnp.concatenate([kb, qn], axis=0).astype(BF16))
                vbkg.append(jnp.concatenate([vh * beta_b, kb * eg], axis=1).astype(BF16))
                qg16.append((qn * eg).astype(BF16))
                kdt16.append((kn * jnp.exp(g_last - gc_b)).T.astype(BF16))
        n = len(units)
        kq = [_dot_nt(lhs_kq[i], k16[i]) for i in range(n)]
        xs = [kq[i][:c] * neg_r[i] for i in range(n)]
        ps = [eye + xs[i] for i in range(n)]
        att16 = [(kq[i][c:] * dec[i]).astype(BF16) for i in range(n)]
        x16 = [x.astype(BF16) for x in xs]
        xs = [_dot(x, x) for x in x16]
        span = 2
        while 2 * span < c:
            x16 = [x.astype(BF16) for x in xs]
            both = [_dot(jnp.concatenate([ps[i].astype(BF16), x16[i]], axis=0), x16[i]) for i in range(n)]
            ps = [ps[i] + both[i][:c] for i in range(n)]
            xs = [both[i][c:] for i in range(n)]
            span *= 2
        ps = [ps[i] + _dot(ps[i].astype(BF16), xs[i].astype(BF16)) for i in range(n)]
        uw = [_dot(ps[i].astype(BF16), vbkg[i]) for i in range(n)]
        for i, (ci, h) in enumerate(units):
            idx = ci * DN_HEADS + h
            u_ref[idx] = uw[i][:, :DN_HEAD_DIM]
            wq_ref[idx] = jnp.concatenate([uw[i][:, DN_HEAD_DIM:].astype(BF16), qg16[i]], axis=0)
            ak_ref[idx] = jnp.concatenate([att16[i], kdt16[i]], axis=0)

    for ci in range(n_chunks):
        r0 = ci * c
        st = [state_ref[h] for h in heads]
        st16 = [s.astype(BF16) for s in st]
        r1 = [_dot(wq_ref[ci * DN_HEADS + h], st16[h]) for h in heads]
        vn16 = [(u_ref[ci * DN_HEADS + h] - r1[h][:c]).astype(BF16) for h in heads]
        r2 = [_dot(ak_ref[ci * DN_HEADS + h], vn16[h]) for h in heads]
        for h in heads:
            lo = h * DN_HEAD_DIM
            g_last = gcol_ref[r0 + c - 1:r0 + c, DN_HEADS + h:DN_HEADS + h + 1]
            state_ref[h] = st[h] * jnp.exp(g_last) + r2[h][c:]
            o = r1[h][c:] + r2[h][:c]
            on = o * lax.rsqrt(jnp.mean(o * o, axis=-1, keepdims=True) + EPS) * og_ref[...]
            zh = pm_ref[r0:r0 + c, Z_OFF + lo:Z_OFF + lo + DN_HEAD_DIM]
            y_ref[r0:r0 + c, POOL_WIDTH + lo:POOL_WIDTH + lo + DN_HEAD_DIM] = (on * _silu(zh)).astype(y_ref.dtype)

    extp_ref[0:POOL_HALO, :] = extp_ref[tm:tm + POOL_HALO, :]
    extq_ref[0:CONV_HALO, :] = extq_ref[tm:tm + CONV_HALO, :]


def _mixer(main3, ba3, w_pool, pool_scale, conv_w, gate_params, o_norm_g):
    b, s, _ = main3.shape
    units = (TM_MIX // GDN_CHUNK) * DN_HEADS
    return pl.pallas_call(
        _mixer_kernel,
        grid=(b, s // TM_MIX),
        in_specs=[
            pl.BlockSpec((None, TM_MIX, MAIN_COLS), lambda bi, si: (bi, si, 0)),
            pl.BlockSpec((None, TM_MIX, LANES), lambda bi, si: (bi, si, 0)),
            _const_spec((len(POOL_WINDOWS), POOL_GROUP_DIM, POOL_GROUP_DIM)),
            _const_spec((1, POOL_WIDTH)),
            _const_spec((CONV_K, 3 * DN_WIDTH)),
            _const_spec((8, LANES)),
            _const_spec((1, DN_HEAD_DIM)),
        ],
        out_specs=pl.BlockSpec((None, TM_MIX, D_MODEL), lambda bi, si: (bi, si, 0)),
        out_shape=jax.ShapeDtypeStruct((b, s, D_MODEL), BF16),
        scratch_shapes=[
            pltpu.VMEM((POOL_HALO + TM_MIX, POOL_WIDTH), F32),
            pltpu.VMEM((CONV_HALO + TM_MIX, 3 * DN_WIDTH), F32),
            pltpu.VMEM((8, TM_MIX), F32),
            pltpu.VMEM((TM_MIX, LANES), F32),
            pltpu.VMEM((DN_HEADS, DN_HEAD_DIM, DN_HEAD_DIM), F32),
            pltpu.VMEM((units, GDN_CHUNK, DN_HEAD_DIM), F32),
            pltpu.VMEM((units, 2 * GDN_CHUNK, DN_HEAD_DIM), BF16),
            pltpu.VMEM((units, 2 * GDN_CHUNK, DN_HEAD_DIM), BF16),
        ],
        compiler_params=pltpu.CompilerParams(
            dimension_semantics=("arbitrary", "arbitrary"), vmem_limit_bytes=VMEM_LIMIT),
        name="mixer",
    )(main3, ba3, w_pool, pool_scale, conv_w, gate_params, o_norm_g)


def _mem_kv_kernel(m_ref, g_ref, wk_ref, wv_ref, k_ref, v_ref):
    mn = _rms(m_ref[...], g_ref[...]).astype(BF16)
    k_ref[...] = _dot(mn, wk_ref[...]).astype(k_ref.dtype)
    v_ref[...] = _dot(mn, wv_ref[...]).astype(v_ref.dtype)


def _mem_kv(mem2, g, w_k, w_v):
    t = mem2.shape[0]
    tm = 512
    return pl.pallas_call(
        _mem_kv_kernel,
        grid=(t // tm,),
        in_specs=[
            pl.BlockSpec((tm, D_MODEL), lambda i: (i, 0)),
            _const_spec((1, D_MODEL)),
            _const_spec((D_MODEL, D_MODEL)),
            _const_spec((D_MODEL, D_MODEL)),
        ],
        out_specs=[
            pl.BlockSpec((tm, D_MODEL), lambda i: (i, 0)),
            pl.BlockSpec((tm, D_MODEL), lambda i: (i, 0)),
        ],
        out_shape=[jax.ShapeDtypeStruct((t, D_MODEL), BF16)] * 2,
        compiler_params=pltpu.CompilerParams(
            dimension_semantics=("arbitrary",), vmem_limit_bytes=VMEM_LIMIT),
        name="mem_kv",
    )(mem2, g, w_k, w_v)


def _post_kernel(x_ref, y_ref, k_ref, v_ref, wout_ref, gxa_ref, wq_ref, wo_ref, gmlp_ref,
                 wup_ref, wdn_ref, gfin_ref, o_ref):
    h1 = x_ref[...] + _dot(y_ref[...], wout_ref[...])
    q = _dot(_rms(h1, gxa_ref[...]).astype(BF16), wq_ref[...])
    heads = []
    for hd in range(XA_HEADS):
        lo = hd * XA_HEAD_DIM
        s = _dot_nt(q[:, lo:lo + XA_HEAD_DIM].astype(BF16), k_ref[:, lo:lo + XA_HEAD_DIM])
        s = s * (XA_HEAD_DIM ** -0.5)
        e = jnp.exp(s - jnp.max(s, axis=-1, keepdims=True))
        p = e / jnp.sum(e, axis=-1, keepdims=True)
        heads.append(_dot(p.astype(BF16), v_ref[:, lo:lo + XA_HEAD_DIM]))
    att = jnp.concatenate(heads, axis=-1)
    h2 = h1 + _dot(att.astype(BF16), wo_ref[...])
    hn2 = _rms(h2, gmlp_ref[...]).astype(BF16)
    acc = h2
    for j in range(D_FF // FF_BLOCK):
        a = jnp.maximum(_dot(hn2, wup_ref[:, j * FF_BLOCK:(j + 1) * FF_BLOCK]), 0.0)
        acc = acc + _dot((a * a).astype(BF16), wdn_ref[j * FF_BLOCK:(j + 1) * FF_BLOCK, :])
    o_ref[...] = _rms(acc, gfin_ref[...])


def _post(x2, y2, k3, v3, w_out, g_xa, w_q, w_o, g_mlp, w_up, w_dn, g_fin, seq):
    t = x2.shape[0]
    mem_len = k3.shape[1]
    tiles_per_batch = seq // TM_POST
    return pl.pallas_call(
        _post_kernel,
        grid=(t // TM_POST,),
        in_specs=[
            pl.BlockSpec((TM_POST, D_MODEL), lambda i: (i, 0)),
            pl.BlockSpec((TM_POST, D_MODEL), lambda i: (i, 0)),
            pl.BlockSpec((None, mem_len, D_MODEL), lambda i: (i // tiles_per_batch, 0, 0)),
            pl.BlockSpec((None, mem_len, D_MODEL), lambda i: (i // tiles_per_batch, 0, 0)),
            _const_spec((D_MODEL, D_MODEL)),
            _const_spec((1, D_MODEL)),
            _const_spec((D_MODEL, D_MODEL)),
            _const_spec((D_MODEL, D_MODEL)),
            _const_spec((1, D_MODEL)),
            _const_spec((D_MODEL, D_FF)),
            _const_spec((D_FF, D_MODEL)),
            _const_spec((1, D_MODEL)),
        ],
        out_specs=pl.BlockSpec((TM_POST, D_MODEL), lambda i: (i, 0)),
        out_shape=jax.ShapeDtypeStruct((t, D_MODEL), F32),
        compiler_params=pltpu.CompilerParams(
            dimension_semantics=("arbitrary",), vmem_limit_bytes=VMEM_LIMIT),
        name="post",
    )(x2, y2, k3, v3, w_out, g_xa, w_q, w_o, g_mlp, w_up, w_dn, g_fin)


def kernel(x, mem, norm_mix_g, w_in, w_pool, pool_scale, conv_w, a_log, dt_bias, dn_out_norm_g, w_out,
           norm_xattn_g, mem_norm_g, w_xq, w_xk, w_xv, w_xo, norm_mlp_g, w_up, w_down, final_norm_g):
    b, s, d = x.shape
    mem_len = mem.shape[1]
    depth = w_in.shape[0]
    row = lambda v: v.reshape(1, -1).astype(F32)

    h = x
    for l in range(depth):
        w_main = w_in[l][:, :MAIN_COLS].astype(BF16)
        w_ba = jnp.pad(w_in[l][:, MAIN_COLS:], ((0, 0), (0, LANES - 2 * DN_HEADS))).astype(BF16)
        gate_params = jnp.zeros((8, LANES), F32)
        gate_params = gate_params.at[DN_HEADS:, 0].set(a_log[l]).at[DN_HEADS:, 1].set(dt_bias[l])

        x2 = h.reshape(b * s, d)
        main, ba = _in_proj(x2, row(norm_mix_g[l]), w_main, w_ba)
        y = _mixer(main.reshape(b, s, MAIN_COLS), ba.reshape(b, s, LANES), w_pool[l].astype(BF16),
                   row(pool_scale[l]), conv_w[l].astype(F32), gate_params, row(dn_out_norm_g[l]))
        k, v = _mem_kv(mem.reshape(b * mem_len, d), row(mem_norm_g[l]),
                       w_xk[l].astype(BF16), w_xv[l].astype(BF16))
        g_fin = row(final_norm_g) if l == depth - 1 else None
        assert g_fin is not None, "only the last layer is followed by the final norm"
        h = _post(x2, y.reshape(b * s, d), k.reshape(b, mem_len, d), v.reshape(b, mem_len, d),
                  w_out[l].astype(BF16), row(norm_xattn_g[l]), w_xq[l].astype(BF16), w_xo[l].astype(BF16),
                  row(norm_mlp_g[l]), w_up[l].astype(BF16), w_down[l].astype(BF16), g_fin, s).reshape(b, s, d)
    return h
```

```python
import functools
import itertools

import jax
import jax.numpy as jnp
from jax import lax
from jax.experimental import pallas as pl
from jax.experimental.pallas import tpu as pltpu

D_MODEL = 1024
POOL_WIDTH = 512
POOL_WINDOWS = (2, 4, 8, 16)
POOL_GROUP_DIM = 128
DN_WIDTH = 512
DN_HEADS = 4
DN_HEAD_DIM = 128
CONV_K = 4
MAIN_COLS = POOL_WIDTH + 4 * DN_WIDTH
QKV_OFF = POOL_WIDTH
Z_OFF = POOL_WIDTH + 3 * DN_WIDTH
XA_HEADS = 4
XA_HEAD_DIM = 256
D_FF = 4096
EPS = 1e-6

LANES = 128
GDN_CHUNK = 128
POOL_HALO = 16
CONV_HALO = 8
TM_PROJ = 512
TM = 256
COL_BLOCK = 512
FF_BLOCK = 512
POST_PIECES_AFTER_PREP_SEGMENT = (2, 1, 1, 1, 2, 1, 1, 2, 1, 2, 1, 1)
VMEM_LIMIT = 56 * 1024 * 1024

F32 = jnp.float32
BF16 = jnp.bfloat16


def _dot(a, b):
    return jnp.dot(a, b, preferred_element_type=F32)


def _rms(x, g):
    return x * lax.rsqrt(jnp.mean(x * x, axis=-1, keepdims=True) + EPS) * g


def _silu(x):
    return x * jax.nn.sigmoid(x)


def _const_spec(shape):
    zeros = (0,) * len(shape)
    return pl.BlockSpec(shape, lambda *_: zeros, pipeline_mode=pl.Buffered(1))


def _in_proj_kernel(x_ref, g_ref, wm_ref, wba_ref, main_ref, ba_ref):
    hn = _rms(x_ref[...], g_ref[...]).astype(BF16)
    main_ref[...] = _dot(hn, wm_ref[...])
    ba_ref[...] = _dot(hn, wba_ref[...])


def _in_proj(x2, g, w_main, w_ba):
    t = x2.shape[0]
    return pl.pallas_call(
        _in_proj_kernel,
        grid=(t // TM_PROJ,),
        in_specs=[
            pl.BlockSpec((TM_PROJ, D_MODEL), lambda i: (i, 0)),
            _const_spec((1, D_MODEL)),
            _const_spec((D_MODEL, MAIN_COLS)),
            _const_spec((D_MODEL, LANES)),
        ],
        out_specs=[
            pl.BlockSpec((TM_PROJ, MAIN_COLS), lambda i: (i, 0)),
            pl.BlockSpec((TM_PROJ, LANES), lambda i: (i, 0)),
        ],
        out_shape=[
            jax.ShapeDtypeStruct((t, MAIN_COLS), F32),
            jax.ShapeDtypeStruct((t, LANES), F32),
        ],
        compiler_params=pltpu.CompilerParams(
            dimension_semantics=("arbitrary",), vmem_limit_bytes=VMEM_LIMIT),
        name="in_proj",
    )(x2, g, w_main, w_ba)


def _mem_kv_kernel(m_ref, g_ref, wk_ref, wv_ref, kt_ref, v_ref):
    mn = _rms(m_ref[...], g_ref[...]).astype(BF16)
    kt_ref[...] = _dot(mn, wk_ref[...]).T.astype(kt_ref.dtype)
    v_ref[...] = _dot(mn, wv_ref[...]).astype(v_ref.dtype)


def _mem_kv(mem, g, w_k, w_v):
    b, mem_len, _ = mem.shape
    return pl.pallas_call(
        _mem_kv_kernel,
        grid=(b,),
        in_specs=[
            pl.BlockSpec((None, mem_len, D_MODEL), lambda i: (i, 0, 0)),
            _const_spec((1, D_MODEL)),
            _const_spec((D_MODEL, D_MODEL)),
            _const_spec((D_MODEL, D_MODEL)),
        ],
        out_specs=[
            pl.BlockSpec((None, D_MODEL, mem_len), lambda i: (i, 0, 0)),
            pl.BlockSpec((None, mem_len, D_MODEL), lambda i: (i, 0, 0)),
        ],
        out_shape=[
            jax.ShapeDtypeStruct((b, D_MODEL, mem_len), BF16),
            jax.ShapeDtypeStruct((b, mem_len, D_MODEL), BF16),
        ],
        compiler_params=pltpu.CompilerParams(
            dimension_semantics=("arbitrary",), vmem_limit_bytes=VMEM_LIMIT),
        name="mem_kv",
    )(mem, g, w_k, w_v)


def _mixer_stages(t_in_batch, pm_ref, ba_ref, wpool_ref, pscale_ref, convw_ref, gp_ref, og_ref, y_ref,
                  extp_ref, extq_ref, grow_ref, gcol_ref, state_ref):
    tm = TM
    c = GDN_CHUNK
    n_chunks = tm // c
    heads = range(DN_HEADS)
    prep_hints = iter(POST_PIECES_AFTER_PREP_SEGMENT)
    yield next(prep_hints)

    extp_ref[POOL_HALO:POOL_HALO + tm, :] = pm_ref[:, 0:POOL_WIDTH]
    extq_ref[CONV_HALO:CONV_HALO + tm, :] = pm_ref[:, QKV_OFF:QKV_OFF + 3 * DN_WIDTH]

    t_glob = lax.broadcasted_iota(jnp.int32, (tm, POOL_GROUP_DIM), 0) + t_in_batch * tm
    for gi, win in enumerate(POOL_WINDOWS):
        lo = gi * POOL_GROUP_DIM
        u = extp_ref[POOL_HALO:POOL_HALO + tm, lo:lo + POOL_GROUP_DIM]
        acc = u
        for j in range(1, win):
            acc = acc + extp_ref[POOL_HALO - j:POOL_HALO - j + tm, lo:lo + POOL_GROUP_DIM]
        cnt = jnp.minimum(t_glob + 1, win).astype(F32)
        pooled = acc / cnt - u
        mixed = _dot(pooled.astype(BF16), wpool_ref[gi])
        y_ref[:, lo:lo + POOL_GROUP_DIM] = (mixed * pscale_ref[:, lo:lo + POOL_GROUP_DIM]).astype(y_ref.dtype)
    yield next(prep_hints)

    bat = ba_ref[...].T[0:8, :]
    gp = gp_ref[...]
    a_log = gp[:, 0:1]
    dt_bias = gp[:, 1:2]
    xa = bat + dt_bias
    softplus = jnp.maximum(xa, 0.0) + jnp.log1p(jnp.exp(-jnp.abs(xa)))
    cs = -jnp.exp(a_log) * softplus
    lane = lax.broadcasted_iota(jnp.int32, (8, tm), 1) & (c - 1)
    sh = 1
    while sh < c:
        cs = cs + jnp.where(lane >= sh, pltpu.roll(cs, sh, axis=1), 0.0)
        sh *= 2
    row8 = lax.broadcasted_iota(jnp.int32, (8, tm), 0)
    stack = jnp.where(row8 < DN_HEADS, jax.nn.sigmoid(bat), cs)
    grow_ref[...] = stack
    gcol_ref[...] = jnp.concatenate([stack, jnp.zeros((LANES - 8, tm), F32)], axis=0).T

    ii = lax.broadcasted_iota(jnp.int32, (c, c), 0)
    jj = lax.broadcasted_iota(jnp.int32, (c, c), 1)
    tril = ii >= jj
    strict = ii > jj
    eye = jnp.where(ii == jj, 1.0, 0.0).astype(F32)
    q_scale = DN_HEAD_DIM ** -0.5

    def col(r0, j):
        return jnp.broadcast_to(gcol_ref[r0:r0 + c, j:j + 1], (c, c))

    units = [(ci, h) for ci in range(n_chunks) for h in heads]
    n = len(units)
    acts = []
    for ci in range(n_chunks):
        base = CONV_HALO - (CONV_K - 1) + ci * c
        parts = []
        for part in range(3):
            lanes = slice(part * DN_WIDTH, (part + 1) * DN_WIDTH)
            xc = convw_ref[0:1, lanes] * extq_ref[base:base + c, lanes]
            for j in range(1, CONV_K):
                xc = xc + convw_ref[j:j + 1, lanes] * extq_ref[base + j:base + j + c, lanes]
            parts.append(_silu(xc))
            yield next(prep_hints)
        acts.append(parts)
    kt16, lhs_kq, dec, neg_d, vbkg, qg16, kdt16 = [], [], [], [], [], [], []
    for ci in range(n_chunks):
        r0 = ci * c
        for h in heads:
            lo = h * DN_HEAD_DIM
            qh = acts[ci][0][:, lo:lo + DN_HEAD_DIM]
            kh = acts[ci][1][:, lo:lo + DN_HEAD_DIM]
            vh = acts[ci][2][:, lo:lo + DN_HEAD_DIM]
            qn = qh * lax.rsqrt(jnp.sum(qh * qh, axis=-1, keepdims=True) + EPS) * q_scale
            kn = kh * lax.rsqrt(jnp.sum(kh * kh, axis=-1, keepdims=True) + EPS)
            beta_b = col(r0, h)
            gc_b = col(r0, DN_HEADS + h)
            gc_r = grow_ref[DN_HEADS + h:DN_HEADS + h + 1, r0:r0 + c]
            g_last = gcol_ref[r0 + c - 1:r0 + c, DN_HEADS + h:DN_HEADS + h + 1]
            d = jnp.where(tril, jnp.exp(gc_b - gc_r), 0.0)
            eg = jnp.exp(gc_b)
            kb = kn * beta_b
            kt = kn.T
            dec.append(d)
            neg_d.append(jnp.where(strict, -d, 0.0))
            kt16.append(kt.astype(BF16))
            lhs_kq.append(jnp.concatenate([kb, qn], axis=0).astype(BF16))
            vbkg.append(jnp.concatenate([vh * beta_b, kb * eg], axis=1).astype(BF16))
            qg16.append((qn * eg).astype(BF16))
            kdt16.append((kt * jnp.exp(g_last - gc_r)).astype(BF16))
            if h % 2 == 1:
                yield next(prep_hints)
    kq = [_dot(lhs_kq[i], kt16[i]) for i in range(n)]
    yield 1
    xs = [kq[i][:c] * neg_d[i] for i in range(n)]
    ps = [eye + xs[i] for i in range(n)]
    att16 = [(kq[i][c:] * dec[i]).astype(BF16) for i in range(n)]
    x16 = [x.astype(BF16) for x in xs]
    xs = [_dot(x, x) for x in x16]
    yield 1
    span = 2
    while 2 * span < c:
        x16 = [x.astype(BF16) for x in xs]
        both = [_dot(jnp.concatenate([ps[i].astype(BF16), x16[i]], axis=0), x16[i]) for i in range(n)]
        ps = [ps[i] + both[i][:c] for i in range(n)]
        xs = [both[i][c:] for i in range(n)]
        span *= 2
        yield 1
    ps = [ps[i] + _dot(ps[i].astype(BF16), xs[i].astype(BF16)) for i in range(n)]
    yield 1
    uw = [_dot(ps[i].astype(BF16), vbkg[i]) for i in range(n)]
    yield 1
    u = [uw[i][:, :DN_HEAD_DIM] for i in range(n)]
    wq16 = [jnp.concatenate([uw[i][:, DN_HEAD_DIM:].astype(BF16), qg16[i]], axis=0) for i in range(n)]
    ak16 = [jnp.concatenate([att16[i], kdt16[i]], axis=0) for i in range(n)]

    for ci in range(n_chunks):
        r0 = ci * c
        st = [state_ref[h] for h in heads]
        st16 = [s.astype(BF16) for s in st]
        r1 = [_dot(wq16[ci * DN_HEADS + h], st16[h]) for h in heads]
        yield 1
        vn16 = [(u[ci * DN_HEADS + h] - r1[h][:c]).astype(BF16) for h in heads]
        r2 = [_dot(ak16[ci * DN_HEADS + h], vn16[h]) for h in heads]
        yield 1
        for h in heads:
            lo = h * DN_HEAD_DIM
            g_last = gcol_ref[r0 + c - 1:r0 + c, DN_HEADS + h:DN_HEADS + h + 1]
            state_ref[h] = st[h] * jnp.exp(g_last) + r2[h][c:]
            o = r1[h][c:] + r2[h][:c]
            on = o * lax.rsqrt(jnp.mean(o * o, axis=-1, keepdims=True) + EPS) * og_ref[...]
            zh = pm_ref[r0:r0 + c, Z_OFF + lo:Z_OFF + lo + DN_HEAD_DIM]
            y_ref[r0:r0 + c, POOL_WIDTH + lo:POOL_WIDTH + lo + DN_HEAD_DIM] = (on * _silu(zh)).astype(y_ref.dtype)

    extp_ref[0:POOL_HALO, :] = extp_ref[tm:tm + POOL_HALO, :]
    extq_ref[0:CONV_HALO, :] = extq_ref[tm:tm + CONV_HALO, :]


def _post_mlp_stages(wup_ref, wdn_ref, gfin_ref, o_ref, h_ref, hn2_ref, a_ref):
    for j in range(D_FF // FF_BLOCK):
        blk = slice(j * FF_BLOCK, (j + 1) * FF_BLOCK)
        a = jnp.maximum(_dot(hn2_ref[...], wup_ref[:, blk]), 0.0)
        a_ref[j % 2] = (a * a).astype(BF16)
        yield
        h_ref[...] = h_ref[...] + _dot(a_ref[j % 2], wdn_ref[blk, :])
        yield
    o_ref[...] = _rms(h_ref[...], gfin_ref[...])


def _post_attn_stages(x_ref, y_ref, kt_ref, v_ref, wout_ref, gxa_ref, wq_ref, wo_ref, gmlp_ref,
                      h_ref, hn1_ref, hn2_ref, q_ref, att_ref):
    col_blocks = [slice(n * COL_BLOCK, (n + 1) * COL_BLOCK) for n in range(D_MODEL // COL_BLOCK)]
    for cols in col_blocks:
        h_ref[:, cols] = x_ref[:, cols] + _dot(y_ref[...], wout_ref[:, cols])
        yield
    hn1_ref[...] = _rms(h_ref[...], gxa_ref[...]).astype(BF16)
    for cols in col_blocks:
        q_ref[:, cols] = _dot(hn1_ref[...], wq_ref[:, cols]).astype(BF16)
        yield
    head_lanes = [slice(hd * XA_HEAD_DIM, (hd + 1) * XA_HEAD_DIM) for hd in range(XA_HEADS)]
    scores = [_dot(q_ref[:, hl], kt_ref[hl, :]) * (XA_HEAD_DIM ** -0.5) for hl in head_lanes]
    yield
    expd = [jnp.exp(s - jnp.max(s, axis=-1, keepdims=True)) for s in scores]
    pv = [_dot(e.astype(BF16), v_ref[:, hl]) for e, hl in zip(expd, head_lanes)]
    yield
    for e, o, hl in zip(expd, pv, head_lanes):
        att_ref[:, hl] = (o / jnp.sum(e, axis=-1, keepdims=True)).astype(BF16)
    for cols in col_blocks:
        h_ref[:, cols] = h_ref[:, cols] + _dot(att_ref[...], wo_ref[:, cols])
        yield
    hn2_ref[...] = _rms(h_ref[...], gmlp_ref[...]).astype(BF16)


def _mix_post_kernel(n_tiles, tiles_per_batch,
                     pm_ref, ba_ref, x_ref, kt_ref, v_ref,
                     wpool_ref, pscale_ref, convw_ref, gp_ref, og_ref,
                     wout_ref, gxa_ref, wq_ref, wo_ref, gmlp_ref, wup_ref, wdn_ref, gfin_ref,
                     o_ref,
                     extp_ref, extq_ref, grow_ref, gcol_ref, state_ref, y_ring,
                     h_ref, hn1_ref, hn2_ref, q_ref, att_ref, a_ref):
    s = pl.program_id(0)
    t_in_batch = lax.rem(jnp.minimum(s, n_tiles - 1), jnp.int32(tiles_per_batch))
    wslot = lax.rem(s, jnp.int32(2))

    @pl.when(s == 0)
    def _():
        y_ring[1] = jnp.zeros((TM, D_MODEL), y_ring.dtype)
        h_ref[...] = jnp.zeros_like(h_ref)
        hn2_ref[...] = jnp.zeros_like(hn2_ref)

    @pl.when(t_in_batch == 0)
    def _():
        extp_ref[0:POOL_HALO, :] = jnp.zeros((POOL_HALO, POOL_WIDTH), F32)
        extq_ref[0:CONV_HALO, :] = jnp.zeros((CONV_HALO, 3 * DN_WIDTH), F32)
        state_ref[...] = jnp.zeros_like(state_ref)

    mixer = _mixer_stages(t_in_batch, pm_ref, ba_ref, wpool_ref, pscale_ref, convw_ref, gp_ref, og_ref,
                          y_ring.at[wslot], extp_ref, extq_ref, grow_ref, gcol_ref, state_ref)
    mlp = _post_mlp_stages(wup_ref, wdn_ref, gfin_ref, o_ref, h_ref, hn2_ref, a_ref)
    attn = _post_attn_stages(x_ref, y_ring.at[1 - wslot], kt_ref, v_ref, wout_ref, gxa_ref, wq_ref, wo_ref,
                             gmlp_ref, h_ref, hn1_ref, hn2_ref, q_ref, att_ref)
    post = itertools.chain(mlp, attn)
    for n_post_pieces in mixer:
        for _ in range(n_post_pieces):
            next(post, None)
    for _ in post:
        pass


def _mix_post(main, ba, x2, kt3, v3, w_pool, pool_scale, conv_w, gate_params, o_norm_g,
              w_out, g_xa, w_q, w_o, g_mlp, w_up, w_dn, g_fin, seq):
    t = x2.shape[0]
    mem_len = v3.shape[1]
    n_tiles = t // TM
    tiles_per_batch = seq // TM
    clamp = lambda i: jnp.clip(i, 0, n_tiles - 1)
    mix_tile = lambda s: clamp(s)
    attn_tile = lambda s: clamp(s - 1)
    mlp_tile = lambda s: clamp(s - 2)
    return pl.pallas_call(
        functools.partial(_mix_post_kernel, n_tiles, tiles_per_batch),
        grid=(n_tiles + 2,),
        in_specs=[
            pl.BlockSpec((TM, MAIN_COLS), lambda s: (mix_tile(s), 0)),
            pl.BlockSpec((TM, LANES), lambda s: (mix_tile(s), 0)),
            pl.BlockSpec((TM, D_MODEL), lambda s: (attn_tile(s), 0)),
            pl.BlockSpec((None, D_MODEL, mem_len), lambda s: (attn_tile(s) // tiles_per_batch, 0, 0)),
            pl.BlockSpec((None, mem_len, D_MODEL), lambda s: (attn_tile(s) // tiles_per_batch, 0, 0)),
            _const_spec((len(POOL_WINDOWS), POOL_GROUP_DIM, POOL_GROUP_DIM)),
            _const_spec((1, POOL_WIDTH)),
            _const_spec((CONV_K, 3 * DN_WIDTH)),
            _const_spec((8, LANES)),
            _const_spec((1, DN_HEAD_DIM)),
            _const_spec((D_MODEL, D_MODEL)),
            _const_spec((1, D_MODEL)),
            _const_spec((D_MODEL, D_MODEL)),
            _const_spec((D_MODEL, D_MODEL)),
            _const_spec((1, D_MODEL)),
            _const_spec((D_MODEL, D_FF)),
            _const_spec((D_FF, D_MODEL)),
            _const_spec((1, D_MODEL)),
        ],
        out_specs=pl.BlockSpec((TM, D_MODEL), lambda s: (mlp_tile(s), 0)),
        out_shape=jax.ShapeDtypeStruct((t, D_MODEL), F32),
        scratch_shapes=[
            pltpu.VMEM((POOL_HALO + TM, POOL_WIDTH), F32),
            pltpu.VMEM((CONV_HALO + TM, 3 * DN_WIDTH), F32),
            pltpu.VMEM((8, TM), F32),
            pltpu.VMEM((TM, LANES), F32),
            pltpu.VMEM((DN_HEADS, DN_HEAD_DIM, DN_HEAD_DIM), F32),
            pltpu.VMEM((2, TM, D_MODEL), BF16),
            pltpu.VMEM((TM, D_MODEL), F32),
            pltpu.VMEM((TM, D_MODEL), BF16),
            pltpu.VMEM((TM, D_MODEL), BF16),
            pltpu.VMEM((TM, D_MODEL), BF16),
            pltpu.VMEM((TM, D_MODEL), BF16),
            pltpu.VMEM((2, TM, FF_BLOCK), BF16),
        ],
        compiler_params=pltpu.CompilerParams(
            dimension_semantics=("arbitrary",), vmem_limit_bytes=VMEM_LIMIT),
        name="mix_post",
    )(main, ba, x2, kt3, v3, w_pool, pool_scale, conv_w, gate_params, o_norm_g,
      w_out, g_xa, w_q, w_o, g_mlp, w_up, w_dn, g_fin)


def kernel(x, mem, norm_mix_g, w_in, w_pool, pool_scale, conv_w, a_log, dt_bias, dn_out_norm_g, w_out,
           norm_xattn_g, mem_norm_g, w_xq, w_xk, w_xv, w_xo, norm_mlp_g, w_up, w_down, final_norm_g):
    b, s, d = x.shape
    assert w_in.shape[0] == 1, "single-layer problem: the final RMSNorm is fused into the layer's post stage"
    row = lambda v: v.reshape(1, -1).astype(F32)

    w_main = w_in[0][:, :MAIN_COLS].astype(BF16)
    w_ba = jnp.pad(w_in[0][:, MAIN_COLS:], ((0, 0), (0, LANES - 2 * DN_HEADS))).astype(BF16)
    gate_params = jnp.zeros((8, LANES), F32)
    gate_params = gate_params.at[DN_HEADS:, 0].set(a_log[0]).at[DN_HEADS:, 1].set(dt_bias[0])

    x2 = x.reshape(b * s, d)
    main, ba = _in_proj(x2, row(norm_mix_g[0]), w_main, w_ba)
    kt3, v3 = _mem_kv(mem, row(mem_norm_g[0]), w_xk[0].astype(BF16), w_xv[0].astype(BF16))
    out = _mix_post(main, ba, x2, kt3, v3, w_pool[0].astype(BF16), row(pool_scale[0]), conv_w[0].astype(F32),
                    gate_params, row(dn_out_norm_g[0]), w_out[0].astype(BF16), row(norm_xattn_g[0]),
                    w_xq[0].astype(BF16), w_xo[0].astype(BF16), row(norm_mlp_g[0]), w_up[0].astype(BF16),
                    w_down[0].astype(BF16), row(final_norm_g), s)
    return out.reshape(b, s, d)
```

```python
import functools

import jax
import jax.numpy as jnp
from jax import lax
from jax.experimental import pallas as pl
from jax.experimental.pallas import tpu as pltpu

D_MODEL = 1024
POOL_WIDTH = 512
POOL_WINDOWS = (2, 4, 8, 16)
POOL_GROUP_DIM = 128
DN_WIDTH = 512
DN_HEADS = 4
DN_HEAD_DIM = 128
CONV_K = 4
MAIN_COLS = POOL_WIDTH + 4 * DN_WIDTH
QKV_OFF = POOL_WIDTH
Z_OFF = POOL_WIDTH + 3 * DN_WIDTH
XA_HEADS = 4
XA_HEAD_DIM = 256
D_FF = 4096
EPS = 1e-6

LANES = 128
GDN_CHUNK = 128
POOL_HALO = 16
CONV_HALO = 8
TM = 512
CHUNKS_PER_SWEEP = 2
FF_BLOCK = 1024
N_IN_PROJ_SEGMENTS_BEFORE_PREP = 4
PREP_SEGMENTS_PER_IN_PROJ_SEGMENT = 7
VMEM_LIMIT = 56 * 1024 * 1024

F32 = jnp.float32
BF16 = jnp.bfloat16


def _dot(a, b):
    return jnp.dot(a, b, preferred_element_type=F32)


def _rms(x, g):
    return x * lax.rsqrt(jnp.mean(x * x, axis=-1, keepdims=True) + EPS) * g


def _silu(x):
    return x * jax.nn.sigmoid(x)


def _const_spec(shape):
    zeros = (0,) * len(shape)
    return pl.BlockSpec(shape, lambda *_: zeros, pipeline_mode=pl.Buffered(1))


def _interleave(first, second, second_per_first=1):
    for _ in first:
        for _ in range(second_per_first):
            next(second, None)
    for _ in second:
        pass


def _in_proj_matmuls(x_ref, g_ref, wm_ref, wba_ref, u_ref, z_ref, hn_ref, qkv_dst, ba_dst):
    hn_ref[...] = _rms(x_ref[...], g_ref[...]).astype(BF16)
    ba_dst[...] = _dot(hn_ref[...], wba_ref[...])
    yield
    for part in range(3):
        lanes = slice(part * DN_WIDTH, (part + 1) * DN_WIDTH)
        qkv_dst[CONV_HALO:CONV_HALO + TM, lanes] = _dot(
            hn_ref[...], wm_ref[:, QKV_OFF + part * DN_WIDTH:QKV_OFF + (part + 1) * DN_WIDTH])
        yield
    u_ref[...] = _dot(hn_ref[...], wm_ref[:, 0:POOL_WIDTH])
    yield
    z_ref[...] = _dot(hn_ref[...], wm_ref[:, Z_OFF:Z_OFF + DN_WIDTH])
    yield


def _in_proj_prep(qkv_src, ba_src, halo_ref, convw_ref, gp_ref, q_ref, k_ref, v_ref, grow_ref, gcol_ref):
    c = GDN_CHUNK
    qkv_src[0:CONV_HALO, :] = halo_ref[...]

    bat = ba_src[...].T[0:8, :]
    gp = gp_ref[...]
    a_log = gp[:, 0:1]
    dt_bias = gp[:, 1:2]
    xa = bat + dt_bias
    softplus = jnp.maximum(xa, 0.0) + jnp.log1p(jnp.exp(-jnp.abs(xa)))
    cs = -jnp.exp(a_log) * softplus
    lane = lax.broadcasted_iota(jnp.int32, (8, TM), 1) & (c - 1)
    sh = 1
    while sh < c:
        cs = cs + jnp.where(lane >= sh, pltpu.roll(cs, sh, axis=1), 0.0)
        sh *= 2
    row8 = lax.broadcasted_iota(jnp.int32, (8, TM), 0)
    stack = jnp.where(row8 < DN_HEADS, jax.nn.sigmoid(bat), cs)
    grow_ref[...] = stack
    gcol_ref[...] = jnp.concatenate([stack, jnp.zeros((LANES - 8, TM), F32)], axis=0).T
    yield

    q_scale = DN_HEAD_DIM ** -0.5
    for part, (dst, scale) in enumerate(((q_ref, q_scale), (k_ref, 1.0), (v_ref, None))):
        lanes = slice(part * DN_WIDTH, (part + 1) * DN_WIDTH)
        for r0 in range(0, TM, c):
            base = CONV_HALO - (CONV_K - 1) + r0
            xc = convw_ref[0:1, lanes] * qkv_src[base:base + c, lanes]
            for j in range(1, CONV_K):
                xc = xc + convw_ref[j:j + 1, lanes] * qkv_src[base + j:base + j + c, lanes]
            act = _silu(xc)
            if scale is None:
                dst[r0:r0 + c, :] = act
            else:
                for h in range(DN_HEADS):
                    lo = h * DN_HEAD_DIM
                    a = act[:, lo:lo + DN_HEAD_DIM]
                    a = a * lax.rsqrt(jnp.sum(a * a, axis=-1, keepdims=True) + EPS)
                    dst[r0:r0 + c, lo:lo + DN_HEAD_DIM] = a if scale == 1.0 else a * scale
            yield
    halo_ref[...] = qkv_src[TM:TM + CONV_HALO, :]


def _in_proj_kernel(tiles_per_batch, x_ref, g_ref, wm_ref, wba_ref, convw_ref, gp_ref,
                    u_ref, z_ref, q_ref, k_ref, v_ref, grow_ref, gcol_ref,
                    hn_ref, qkv_ref, ba_ref, halo_ref):

    @pl.when(lax.rem(pl.program_id(0), jnp.int32(tiles_per_batch)) == 0)
    def _():
        halo_ref[...] = jnp.zeros_like(halo_ref)

    matmuls = _in_proj_matmuls(x_ref, g_ref, wm_ref, wba_ref, u_ref, z_ref, hn_ref, qkv_ref, ba_ref)
    prep = _in_proj_prep(qkv_ref, ba_ref, halo_ref, convw_ref, gp_ref, q_ref, k_ref, v_ref, grow_ref, gcol_ref)
    for _ in range(N_IN_PROJ_SEGMENTS_BEFORE_PREP):
        next(matmuls)
    _interleave(matmuls, prep, second_per_first=PREP_SEGMENTS_PER_IN_PROJ_SEGMENT)


def _in_proj(x2, g, w_main, w_ba, conv_w, gate_params, seq):
    t = x2.shape[0]
    n_tiles = t // TM
    tile = lambda width: pl.BlockSpec((TM, width), lambda i: (i, 0))
    wide = jax.ShapeDtypeStruct((t, DN_WIDTH), F32)
    return pl.pallas_call(
        functools.partial(_in_proj_kernel, seq // TM),
        grid=(n_tiles,),
        in_specs=[
            tile(D_MODEL),
            _const_spec((1, D_MODEL)),
            _const_spec((D_MODEL, MAIN_COLS)),
            _const_spec((D_MODEL, LANES)),
            _const_spec((CONV_K, 3 * DN_WIDTH)),
            _const_spec((8, LANES)),
        ],
        out_specs=[
            tile(POOL_WIDTH), tile(DN_WIDTH), tile(DN_WIDTH), tile(DN_WIDTH), tile(DN_WIDTH),
            pl.BlockSpec((None, 8, TM), lambda i: (i, 0, 0)),
            tile(LANES),
        ],
        out_shape=[
            jax.ShapeDtypeStruct((t, POOL_WIDTH), F32), wide, wide, wide, wide,
            jax.ShapeDtypeStruct((n_tiles, 8, TM), F32),
            jax.ShapeDtypeStruct((t, LANES), F32),
        ],
        scratch_shapes=[
            pltpu.VMEM((TM, D_MODEL), BF16),
            pltpu.VMEM((CONV_HALO + TM, 3 * DN_WIDTH), F32),
            pltpu.VMEM((TM, LANES), F32),
            pltpu.VMEM((CONV_HALO, 3 * DN_WIDTH), F32),
        ],
        compiler_params=pltpu.CompilerParams(
            dimension_semantics=("arbitrary",), vmem_limit_bytes=VMEM_LIMIT),
        name="in_proj",
    )(x2, g, w_main, w_ba, conv_w, gate_params)


def _gdn_phase1(chunks, q_ref, k_ref, v_ref, grow_ref, gcol_ref, u_ref, wq_ref, ak_ref):
    c = GDN_CHUNK
    ii = lax.broadcasted_iota(jnp.int32, (c, c), 0)
    jj = lax.broadcasted_iota(jnp.int32, (c, c), 1)
    tril = ii >= jj
    strict = ii > jj
    eye = jnp.where(ii == jj, 1.0, 0.0).astype(F32)

    def col(r0, j):
        return jnp.broadcast_to(gcol_ref[r0:r0 + c, j:j + 1], (c, c))

    units = [(ci, h) for ci in chunks for h in range(DN_HEADS)]
    n = len(units)
    kt16, lhs_kq, dec, neg_d, vbkg, qg16, kdt16 = [], [], [], [], [], [], []
    for ci, h in units:
        r0 = ci * c
        lo = h * DN_HEAD_DIM
        qn = q_ref[r0:r0 + c, lo:lo + DN_HEAD_DIM]
        kn = k_ref[r0:r0 + c, lo:lo + DN_HEAD_DIM]
        vh = v_ref[r0:r0 + c, lo:lo + DN_HEAD_DIM]
        beta_b = col(r0, h)
        gc_b = col(r0, DN_HEADS + h)
        gc_r = grow_ref[DN_HEADS + h:DN_HEADS + h + 1, r0:r0 + c]
        g_last = gcol_ref[r0 + c - 1:r0 + c, DN_HEADS + h:DN_HEADS + h + 1]
        d = jnp.where(tril, jnp.exp(gc_b - gc_r), 0.0)
        eg = jnp.exp(gc_b)
        kb = kn * beta_b
        kt = kn.T
        dec.append(d)
        neg_d.append(jnp.where(strict, -d, 0.0))
        kt16.append(kt.astype(BF16))
        lhs_kq.append(jnp.concatenate([kb, qn], axis=0).astype(BF16))
        vbkg.append(jnp.concatenate([vh * beta_b, kb * eg], axis=1).astype(BF16))
        qg16.append((qn * eg).astype(BF16))
        kdt16.append((kt * jnp.exp(g_last - gc_r)).astype(BF16))
        if h % 2 == 1:
            yield
    kq = [_dot(lhs_kq[i], kt16[i]) for i in range(n)]
    yield
    xs = [kq[i][:c] * neg_d[i] for i in range(n)]
    ps = [eye + xs[i] for i in range(n)]
    att16 = [(kq[i][c:] * dec[i]).astype(BF16) for i in range(n)]
    x16 = [x.astype(BF16) for x in xs]
    xs = [_dot(x, x) for x in x16]
    yield
    span = 2
    while 2 * span < c:
        x16 = [x.astype(BF16) for x in xs]
        both = [_dot(jnp.concatenate([ps[i].astype(BF16), x16[i]], axis=0), x16[i]) for i in range(n)]
        ps = [ps[i] + both[i][:c] for i in range(n)]
        xs = [both[i][c:] for i in range(n)]
        span *= 2
        yield
    ps = [ps[i] + _dot(ps[i].astype(BF16), xs[i].astype(BF16)) for i in range(n)]
    yield
    uw = [_dot(ps[i].astype(BF16), vbkg[i]) for i in range(n)]
    for i, (ci, h) in enumerate(units):
        idx = ci * DN_HEADS + h
        u_ref[idx] = uw[i][:, :DN_HEAD_DIM]
        wq_ref[idx] = jnp.concatenate([uw[i][:, DN_HEAD_DIM:].astype(BF16), qg16[i]], axis=0)
        ak_ref[idx] = jnp.concatenate([att16[i], kdt16[i]], axis=0)
    yield


def _gdn_phase2(chunks, gcol_ref, state_ref, u_ref, wq_ref, ak_ref, o_ref):
    c = GDN_CHUNK
    heads = range(DN_HEADS)
    for ci in chunks:
        r0 = ci * c
        st = [state_ref[h] for h in heads]
        st16 = [s.astype(BF16) for s in st]
        r1 = [_dot(wq_ref[ci * DN_HEADS + h], st16[h]) for h in heads]
        yield
        vn16 = [(u_ref[ci * DN_HEADS + h] - r1[h][:c]).astype(BF16) for h in heads]
        r2 = [_dot(ak_ref[ci * DN_HEADS + h], vn16[h]) for h in heads]
        yield
        for h in heads:
            lo = h * DN_HEAD_DIM
            g_last = gcol_ref[r0 + c - 1:r0 + c, DN_HEADS + h:DN_HEADS + h + 1]
            state_ref[h] = st[h] * jnp.exp(g_last) + r2[h][c:]
            o_ref[r0:r0 + c, lo:lo + DN_HEAD_DIM] = r1[h][c:] + r2[h][:c]


def _gdn_kernel(q_ref, k_ref, v_ref, grow_ref, gcol_ref, o_ref, state_ref, u_ref, wq_ref, ak_ref):
    @pl.when(pl.program_id(1) == 0)
    def _():
        state_ref[...] = jnp.zeros_like(state_ref)

    n_chunks = TM // GDN_CHUNK
    sweeps = [range(g0, g0 + CHUNKS_PER_SWEEP) for g0 in range(0, n_chunks, CHUNKS_PER_SWEEP)]
    phase1 = lambda chunks: _gdn_phase1(chunks, q_ref, k_ref, v_ref, grow_ref, gcol_ref, u_ref, wq_ref, ak_ref)
    phase2 = lambda chunks: _gdn_phase2(chunks, gcol_ref, state_ref, u_ref, wq_ref, ak_ref, o_ref)
    for _ in phase1(sweeps[0]):
        pass
    for prev, cur in zip(sweeps[:-1], sweeps[1:]):
        _interleave(phase1(cur), phase2(prev))
    for _ in phase2(sweeps[-1]):
        pass


def _gdn(q3, k3, v3, grow3, gcol3):
    b, s, _ = q3.shape
    tiles_per_batch = s // TM
    units = (TM // GDN_CHUNK) * DN_HEADS
    tile = lambda width: pl.BlockSpec((None, TM, width), lambda bi, si: (bi, si, 0))
    return pl.pallas_call(
        _gdn_kernel,
        grid=(b, tiles_per_batch),
        in_specs=[tile(DN_WIDTH), tile(DN_WIDTH), tile(DN_WIDTH),
                  pl.BlockSpec((None, 8, TM), lambda bi, si: (bi * tiles_per_batch + si, 0, 0)),
                  tile(LANES)],
        out_specs=tile(DN_WIDTH),
        out_shape=jax.ShapeDtypeStruct((b, s, DN_WIDTH), F32),
        scratch_shapes=[
            pltpu.VMEM((DN_HEADS, DN_HEAD_DIM, DN_HEAD_DIM), F32),
            pltpu.VMEM((units, GDN_CHUNK, DN_HEAD_DIM), F32),
            pltpu.VMEM((units, 2 * GDN_CHUNK, DN_HEAD_DIM), BF16),
            pltpu.VMEM((units, 2 * GDN_CHUNK, DN_HEAD_DIM), BF16),
        ],
        compiler_params=pltpu.CompilerParams(
            dimension_semantics=("arbitrary", "arbitrary"), vmem_limit_bytes=VMEM_LIMIT),
        name="gdn",
    )(q3, k3, v3, grow3, gcol3)


def _mem_kv_kernel(m_ref, g_ref, wk_ref, wv_ref, kt_ref, v_ref):
    mn = _rms(m_ref[...], g_ref[...]).astype(BF16)
    kt_ref[...] = _dot(mn, wk_ref[...]).T.astype(kt_ref.dtype)
    v_ref[...] = _dot(mn, wv_ref[...]).astype(v_ref.dtype)


def _mem_kv(mem, g, w_k, w_v):
    b, mem_len, _ = mem.shape
    return pl.pallas_call(
        _mem_kv_kernel,
        grid=(b,),
        in_specs=[
            pl.BlockSpec((None, mem_len, D_MODEL), lambda i: (i, 0, 0)),
            _const_spec((1, D_MODEL)),
            _const_spec((D_MODEL, D_MODEL)),
            _const_spec((D_MODEL, D_MODEL)),
        ],
        out_specs=[
            pl.BlockSpec((None, D_MODEL, mem_len), lambda i: (i, 0, 0)),
            pl.BlockSpec((None, mem_len, D_MODEL), lambda i: (i, 0, 0)),
        ],
        out_shape=[
            jax.ShapeDtypeStruct((b, D_MODEL, mem_len), BF16),
            jax.ShapeDtypeStruct((b, mem_len, D_MODEL), BF16),
        ],
        compiler_params=pltpu.CompilerParams(
            dimension_semantics=("arbitrary",), vmem_limit_bytes=VMEM_LIMIT),
        name="mem_kv",
    )(mem, g, w_k, w_v)


def _post_kernel(tiles_per_batch, x_ref, u_ref, o_ref, z_ref, kt_ref, v_ref,
                 wpool_ref, pscale_ref, og_ref, wout_ref, gxa_ref, wq_ref, wo_ref, gmlp_ref,
                 wup_ref, wdn_ref, gfin_ref, out_ref, extp_ref):
    tm = TM
    t_in_batch = lax.rem(pl.program_id(0), jnp.int32(tiles_per_batch))

    @pl.when(t_in_batch == 0)
    def _():
        extp_ref[0:POOL_HALO, :] = jnp.zeros((POOL_HALO, POOL_WIDTH), F32)

    extp_ref[POOL_HALO:POOL_HALO + tm, :] = u_ref[...]
    t_glob = lax.broadcasted_iota(jnp.int32, (tm, POOL_GROUP_DIM), 0) + t_in_batch * tm
    y_parts = []
    for gi, win in enumerate(POOL_WINDOWS):
        lo = gi * POOL_GROUP_DIM
        u = extp_ref[POOL_HALO:POOL_HALO + tm, lo:lo + POOL_GROUP_DIM]
        acc = u
        for j in range(1, win):
            acc = acc + extp_ref[POOL_HALO - j:POOL_HALO - j + tm, lo:lo + POOL_GROUP_DIM]
        cnt = jnp.minimum(t_glob + 1, win).astype(F32)
        pooled = acc / cnt - u
        mixed = _dot(pooled.astype(BF16), wpool_ref[gi])
        y_parts.append((mixed * pscale_ref[:, lo:lo + POOL_GROUP_DIM]).astype(BF16))
    extp_ref[0:POOL_HALO, :] = extp_ref[tm:tm + POOL_HALO, :]

    for h in range(DN_HEADS):
        lo = h * DN_HEAD_DIM
        o = o_ref[:, lo:lo + DN_HEAD_DIM]
        on = o * lax.rsqrt(jnp.mean(o * o, axis=-1, keepdims=True) + EPS) * og_ref[...]
        y_parts.append((on * _silu(z_ref[:, lo:lo + DN_HEAD_DIM])).astype(BF16))
    y = jnp.concatenate(y_parts, axis=-1)

    h1 = x_ref[...] + _dot(y, wout_ref[...])
    q = _dot(_rms(h1, gxa_ref[...]).astype(BF16), wq_ref[...]).astype(BF16)
    head_lanes = [slice(hd * XA_HEAD_DIM, (hd + 1) * XA_HEAD_DIM) for hd in range(XA_HEADS)]
    scores = [_dot(q[:, hl], kt_ref[hl, :]) * (XA_HEAD_DIM ** -0.5) for hl in head_lanes]
    expd = [jnp.exp(s - jnp.max(s, axis=-1, keepdims=True)) for s in scores]
    pv = [_dot(e.astype(BF16), v_ref[:, hl]) for e, hl in zip(expd, head_lanes)]
    att = jnp.concatenate([(o / jnp.sum(e, axis=-1, keepdims=True)).astype(BF16) for e, o in zip(expd, pv)],
                          axis=-1)
    h2 = h1 + _dot(att, wo_ref[...])
    hn2 = _rms(h2, gmlp_ref[...]).astype(BF16)
    acc = h2
    for j in range(D_FF // FF_BLOCK):
        a = jnp.maximum(_dot(hn2, wup_ref[:, j * FF_BLOCK:(j + 1) * FF_BLOCK]), 0.0)
        acc = acc + _dot((a * a).astype(BF16), wdn_ref[j * FF_BLOCK:(j + 1) * FF_BLOCK, :])
    out_ref[...] = _rms(acc, gfin_ref[...])


def _post(x2, u, o, z, kt3, v3, w_pool, pool_scale, o_norm_g, w_out, g_xa, w_q, w_o, g_mlp, w_up, w_dn,
          g_fin, seq):
    t = x2.shape[0]
    mem_len = v3.shape[1]
    tiles_per_batch = seq // TM
    tile = lambda width: pl.BlockSpec((TM, width), lambda i: (i, 0))
    return pl.pallas_call(
        functools.partial(_post_kernel, tiles_per_batch),
        grid=(t // TM,),
        in_specs=[
            tile(D_MODEL), tile(POOL_WIDTH), tile(DN_WIDTH), tile(DN_WIDTH),
            pl.BlockSpec((None, D_MODEL, mem_len), lambda i: (i // tiles_per_batch, 0, 0)),
            pl.BlockSpec((None, mem_len, D_MODEL), lambda i: (i // tiles_per_batch, 0, 0)),
            _const_spec((len(POOL_WINDOWS), POOL_GROUP_DIM, POOL_GROUP_DIM)),
            _const_spec((1, POOL_WIDTH)),
            _const_spec((1, DN_HEAD_DIM)),
            _const_spec((D_MODEL, D_MODEL)),
            _const_spec((1, D_MODEL)),
            _const_spec((D_MODEL, D_MODEL)),
            _const_spec((D_MODEL, D_MODEL)),
            _const_spec((1, D_MODEL)),
            _const_spec((D_MODEL, D_FF)),
            _const_spec((D_FF, D_MODEL)),
            _const_spec((1, D_MODEL)),
        ],
        out_specs=tile(D_MODEL),
        out_shape=jax.ShapeDtypeStruct((t, D_MODEL), F32),
        scratch_shapes=[pltpu.VMEM((POOL_HALO + TM, POOL_WIDTH), F32)],
        compiler_params=pltpu.CompilerParams(
            dimension_semantics=("arbitrary",), vmem_limit_bytes=VMEM_LIMIT),
        name="post",
    )(x2, u, o, z, kt3, v3, w_pool, pool_scale, o_norm_g, w_out, g_xa, w_q, w_o, g_mlp, w_up, w_dn, g_fin)


def kernel(x, mem, norm_mix_g, w_in, w_pool, pool_scale, conv_w, a_log, dt_bias, dn_out_norm_g, w_out,
           norm_xattn_g, mem_norm_g, w_xq, w_xk, w_xv, w_xo, norm_mlp_g, w_up, w_down, final_norm_g):
    b, s, d = x.shape
    assert w_in.shape[0] == 1, "single-layer problem: the final RMSNorm is fused into the layer's post stage"
    row = lambda v: v.reshape(1, -1).astype(F32)

    w_main = w_in[0][:, :MAIN_COLS].astype(BF16)
    w_ba = jnp.pad(w_in[0][:, MAIN_COLS:], ((0, 0), (0, LANES - 2 * DN_HEADS))).astype(BF16)
    gate_params = jnp.zeros((8, LANES), F32)
    gate_params = gate_params.at[DN_HEADS:, 0].set(a_log[0]).at[DN_HEADS:, 1].set(dt_bias[0])

    x2 = x.reshape(b * s, d)
    u, z, q, k, v, grow, gcol = _in_proj(x2, row(norm_mix_g[0]), w_main, w_ba, conv_w[0].astype(F32),
                                         gate_params, s)
    per_batch = lambda a: a.reshape(b, s, a.shape[-1])
    o = _gdn(per_batch(q), per_batch(k), per_batch(v), grow, per_batch(gcol))
    kt3, v3 = _mem_kv(mem, row(mem_norm_g[0]), w_xk[0].astype(BF16), w_xv[0].astype(BF16))
    out = _post(x2, u, o.reshape(b * s, DN_WIDTH), z, kt3, v3, w_pool[0].astype(BF16), row(pool_scale[0]),
                row(dn_out_norm_g[0]), w_out[0].astype(BF16), row(norm_xattn_g[0]), w_xq[0].astype(BF16),
                w_xo[0].astype(BF16), row(norm_mlp_g[0]), w_up[0].astype(BF16), w_down[0].astype(BF16),
                row(final_norm_g), s)
    return out.reshape(b, s, d)
```

```python
import functools

import jax
import jax.numpy as jnp
from jax import lax
from jax.experimental import pallas as pl
from jax.experimental.pallas import tpu as pltpu

D_MODEL = 1024
POOL_WIDTH = 512
POOL_WINDOWS = (2, 4, 8, 16)
POOL_GROUP_DIM = 128
DN_WIDTH = 512
DN_HEADS = 4
DN_HEAD_DIM = 128
CONV_K = 4
MAIN_COLS = POOL_WIDTH + 4 * DN_WIDTH
QKV_OFF = POOL_WIDTH
Z_OFF = POOL_WIDTH + 3 * DN_WIDTH
XA_HEADS = 4
XA_HEAD_DIM = 256
D_FF = 4096
EPS = 1e-6

LANES = 128
GDN_CHUNK = 128
POOL_HALO = 16
CONV_HALO = 8
TM = 512
TM_GDN = 1024
CHUNKS_PER_SWEEP = 2
FF_BLOCK = 1024
N_IN_PROJ_SEGMENTS_BEFORE_PREP = 4
PREP_SEGMENTS_PER_IN_PROJ_SEGMENT = 7
VMEM_LIMIT = 56 * 1024 * 1024

F32 = jnp.float32
BF16 = jnp.bfloat16


def _dot(a, b):
    return jnp.dot(a, b, preferred_element_type=F32)


def _rms(x, g):
    return x * lax.rsqrt(jnp.mean(x * x, axis=-1, keepdims=True) + EPS) * g


def _silu(x):
    return x * jax.nn.sigmoid(x)


def _const_spec(shape):
    zeros = (0,) * len(shape)
    return pl.BlockSpec(shape, lambda *_: zeros, pipeline_mode=pl.Buffered(1))


def _round_robin(generators):
    live = list(generators)
    while live:
        for g in list(live):
            try:
                yield next(g)
            except StopIteration:
                live.remove(g)


def _interleave(first, second, second_per_first=1):
    for _ in first:
        for _ in range(second_per_first):
            next(second, None)
    for _ in second:
        pass


def _in_proj_matmuls(x_ref, g_ref, wm_ref, wba_ref, u_ref, z_ref, hn_ref, delayed_ref, ba_ref):
    hn_ref[...] = _rms(x_ref[...], g_ref[...]).astype(BF16)
    ba_ref[...] = _dot(hn_ref[...], wba_ref[...])
    yield
    for part in range(3):
        lanes = slice(part * DN_WIDTH, (part + 1) * DN_WIDTH)
        res = _dot(hn_ref[...], wm_ref[:, QKV_OFF + part * DN_WIDTH:QKV_OFF + (part + 1) * DN_WIDTH])
        for d in range(CONV_K):
            if d:
                delayed_ref[d, 0:CONV_HALO, lanes] = delayed_ref[d, TM:TM + CONV_HALO, lanes]
            delayed_ref[d, d:d + TM, lanes] = res
        yield
    u_ref[...] = _dot(hn_ref[...], wm_ref[:, 0:POOL_WIDTH])
    yield
    z_ref[...] = _dot(hn_ref[...], wm_ref[:, Z_OFF:Z_OFF + DN_WIDTH])
    yield


def _in_proj_prep(delayed_ref, ba_ref, convw_ref, gp_ref, q_ref, k_ref, v_ref, grow_ref, gcol_ref):
    c = GDN_CHUNK

    bat = ba_ref[...].T[0:8, :]
    gp = gp_ref[...]
    a_log = gp[:, 0:1]
    dt_bias = gp[:, 1:2]
    xa = bat + dt_bias
    softplus = jnp.maximum(xa, 0.0) + jnp.log1p(jnp.exp(-jnp.abs(xa)))
    cs = -jnp.exp(a_log) * softplus
    lane = lax.broadcasted_iota(jnp.int32, (8, TM), 1) & (c - 1)
    sh = 1
    while sh < c:
        cs = cs + jnp.where(lane >= sh, pltpu.roll(cs, sh, axis=1), 0.0)
        sh *= 2
    row8 = lax.broadcasted_iota(jnp.int32, (8, TM), 0)
    stack = jnp.where(row8 < DN_HEADS, jax.nn.sigmoid(bat), cs)
    grow_ref[...] = stack
    gcol_ref[...] = jnp.concatenate([stack, jnp.zeros((LANES - 8, TM), F32)], axis=0).T
    yield

    q_scale = DN_HEAD_DIM ** -0.5
    for part, (dst, scale) in enumerate(((q_ref, q_scale), (k_ref, 1.0), (v_ref, None))):
        lanes = slice(part * DN_WIDTH, (part + 1) * DN_WIDTH)
        for r0 in range(0, TM, c):
            xc = convw_ref[CONV_K - 1:CONV_K, lanes] * delayed_ref[0, r0:r0 + c, lanes]
            for d in range(1, CONV_K):
                xc = xc + convw_ref[CONV_K - 1 - d:CONV_K - d, lanes] * delayed_ref[d, r0:r0 + c, lanes]
            act = _silu(xc)
            if scale is None:
                dst[r0:r0 + c, :] = act
            else:
                for h in range(DN_HEADS):
                    lo = h * DN_HEAD_DIM
                    a = act[:, lo:lo + DN_HEAD_DIM]
                    a = a * lax.rsqrt(jnp.sum(a * a, axis=-1, keepdims=True) + EPS)
                    dst[r0:r0 + c, lo:lo + DN_HEAD_DIM] = a if scale == 1.0 else a * scale
            yield


def _in_proj_kernel(tiles_per_batch, x_ref, g_ref, wm_ref, wba_ref, convw_ref, gp_ref,
                    u_ref, z_ref, q_ref, k_ref, v_ref, grow_ref, gcol_ref,
                    hn_ref, delayed_ref, ba_ref):

    @pl.when(lax.rem(pl.program_id(0), jnp.int32(tiles_per_batch)) == 0)
    def _():
        delayed_ref[:, TM:TM + CONV_HALO, :] = jnp.zeros((CONV_K, CONV_HALO, 3 * DN_WIDTH), F32)

    matmuls = _in_proj_matmuls(x_ref, g_ref, wm_ref, wba_ref, u_ref, z_ref, hn_ref, delayed_ref, ba_ref)
    prep = _in_proj_prep(delayed_ref, ba_ref, convw_ref, gp_ref, q_ref, k_ref, v_ref, grow_ref, gcol_ref)
    for _ in range(N_IN_PROJ_SEGMENTS_BEFORE_PREP):
        next(matmuls)
    _interleave(matmuls, prep, second_per_first=PREP_SEGMENTS_PER_IN_PROJ_SEGMENT)


def _in_proj(x2, g, w_main, w_ba, conv_w, gate_params, seq):
    t = x2.shape[0]
    n_tiles = t // TM
    tiles_per_gdn_tile = TM_GDN // TM
    tile = lambda width: pl.BlockSpec((TM, width), lambda i: (i, 0))
    wide = jax.ShapeDtypeStruct((t, DN_WIDTH), F32)
    return pl.pallas_call(
        functools.partial(_in_proj_kernel, seq // TM),
        grid=(n_tiles,),
        in_specs=[
            tile(D_MODEL),
            _const_spec((1, D_MODEL)),
            _const_spec((D_MODEL, MAIN_COLS)),
            _const_spec((D_MODEL, LANES)),
            _const_spec((CONV_K, 3 * DN_WIDTH)),
            _const_spec((8, LANES)),
        ],
        out_specs=[
            tile(POOL_WIDTH), tile(DN_WIDTH), tile(DN_WIDTH), tile(DN_WIDTH), tile(DN_WIDTH),
            pl.BlockSpec((None, 8, TM), lambda i: (i // tiles_per_gdn_tile, 0, i % tiles_per_gdn_tile)),
            tile(LANES),
        ],
        out_shape=[
            jax.ShapeDtypeStruct((t, POOL_WIDTH), F32), wide, wide, wide, wide,
            jax.ShapeDtypeStruct((t // TM_GDN, 8, TM_GDN), F32),
            jax.ShapeDtypeStruct((t, LANES), F32),
        ],
        scratch_shapes=[
            pltpu.VMEM((TM, D_MODEL), BF16),
            pltpu.VMEM((CONV_K, TM + CONV_HALO, 3 * DN_WIDTH), F32),
            pltpu.VMEM((TM, LANES), F32),
        ],
        compiler_params=pltpu.CompilerParams(
            dimension_semantics=("arbitrary",), vmem_limit_bytes=VMEM_LIMIT),
        name="in_proj",
    )(x2, g, w_main, w_ba, conv_w, gate_params)


def _gdn_prep(chunks, q_ref, k_ref, v_ref, grow_ref, gcol_ref, ops):
    c = GDN_CHUNK
    ii = lax.broadcasted_iota(jnp.int32, (c, c), 0)
    jj = lax.broadcasted_iota(jnp.int32, (c, c), 1)
    tril = ii >= jj
    strict = ii > jj

    def col(r0, j):
        return jnp.broadcast_to(gcol_ref[r0:r0 + c, j:j + 1], (c, c))

    for ci in chunks:
        for h in range(DN_HEADS):
            r0 = ci * c
            lo = h * DN_HEAD_DIM
            qn = q_ref[r0:r0 + c, lo:lo + DN_HEAD_DIM]
            kn = k_ref[r0:r0 + c, lo:lo + DN_HEAD_DIM]
            vh = v_ref[r0:r0 + c, lo:lo + DN_HEAD_DIM]
            beta_b = col(r0, h)
            gc_b = col(r0, DN_HEADS + h)
            gc_r = grow_ref[DN_HEADS + h:DN_HEADS + h + 1, r0:r0 + c]
            g_last = gcol_ref[r0 + c - 1:r0 + c, DN_HEADS + h:DN_HEADS + h + 1]
            d = jnp.where(tril, jnp.exp(gc_b - gc_r), 0.0)
            eg = jnp.exp(gc_b)
            kb = kn * beta_b
            kt = kn.T
            ops["unit"].append((ci, h))
            ops["dec"].append(d)
            ops["neg_d"].append(jnp.where(strict, -d, 0.0))
            ops["kt16"].append(kt.astype(BF16))
            ops["lhs_kq"].append(jnp.concatenate([kb, qn], axis=0).astype(BF16))
            ops["vbkg"].append(jnp.concatenate([vh * beta_b, kb * eg], axis=1).astype(BF16))
            ops["qg16"].append((qn * eg).astype(BF16))
            ops["kdt16"].append((kt * jnp.exp(g_last - gc_r)).astype(BF16))
            if h % 2 == 1:
                yield


def _gdn_stages(ops, u_ref, wq_ref, ak_ref):
    c = GDN_CHUNK
    ii = lax.broadcasted_iota(jnp.int32, (c, c), 0)
    jj = lax.broadcasted_iota(jnp.int32, (c, c), 1)
    eye = jnp.where(ii == jj, 1.0, 0.0).astype(F32)
    n = len(ops["unit"])
    kq = [_dot(ops["lhs_kq"][i], ops["kt16"][i]) for i in range(n)]
    yield
    xs = [kq[i][:c] * ops["neg_d"][i] for i in range(n)]
    ps = [eye + xs[i] for i in range(n)]
    att16 = [(kq[i][c:] * ops["dec"][i]).astype(BF16) for i in range(n)]
    x16 = [x.astype(BF16) for x in xs]
    xs = [_dot(x, x) for x in x16]
    yield
    span = 2
    while 2 * span < c:
        x16 = [x.astype(BF16) for x in xs]
        both = [_dot(jnp.concatenate([ps[i].astype(BF16), x16[i]], axis=0), x16[i]) for i in range(n)]
        ps = [ps[i] + both[i][:c] for i in range(n)]
        xs = [both[i][c:] for i in range(n)]
        span *= 2
        yield
    ps = [ps[i] + _dot(ps[i].astype(BF16), xs[i].astype(BF16)) for i in range(n)]
    yield
    uw = [_dot(ps[i].astype(BF16), ops["vbkg"][i]) for i in range(n)]
    for i, (ci, h) in enumerate(ops["unit"]):
        idx = ci * DN_HEADS + h
        u_ref[idx] = uw[i][:, :DN_HEAD_DIM]
        wq_ref[idx] = jnp.concatenate([uw[i][:, DN_HEAD_DIM:].astype(BF16), ops["qg16"][i]], axis=0)
        ak_ref[idx] = jnp.concatenate([att16[i], ops["kdt16"][i]], axis=0)
    yield


def _gdn_phase2(chunks, gcol_ref, state_ref, u_ref, wq_ref, ak_ref, o_ref):
    c = GDN_CHUNK
    heads = range(DN_HEADS)
    for ci in chunks:
        r0 = ci * c
        st = [state_ref[h] for h in heads]
        st16 = [s.astype(BF16) for s in st]
        r1 = [_dot(wq_ref[ci * DN_HEADS + h], st16[h]) for h in heads]
        yield
        vn16 = [(u_ref[ci * DN_HEADS + h] - r1[h][:c]).astype(BF16) for h in heads]
        r2 = [_dot(ak_ref[ci * DN_HEADS + h], vn16[h]) for h in heads]
        yield
        for h in heads:
            lo = h * DN_HEAD_DIM
            g_last = gcol_ref[r0 + c - 1:r0 + c, DN_HEADS + h:DN_HEADS + h + 1]
            state_ref[h] = st[h] * jnp.exp(g_last) + r2[h][c:]
            o_ref[r0:r0 + c, lo:lo + DN_HEAD_DIM] = r1[h][c:] + r2[h][:c]


def _gdn_kernel(q_ref, k_ref, v_ref, grow_ref, gcol_ref, o_ref, state_ref, u_ref, wq_ref, ak_ref):
    @pl.when(pl.program_id(1) == 0)
    def _():
        state_ref[...] = jnp.zeros_like(state_ref)

    n_chunks = TM_GDN // GDN_CHUNK
    sweeps = [range(g0, g0 + CHUNKS_PER_SWEEP) for g0 in range(0, n_chunks, CHUNKS_PER_SWEEP)]
    new_ops = lambda: {name: [] for name in ("unit", "dec", "neg_d", "kt16", "lhs_kq", "vbkg", "qg16", "kdt16")}
    prep = lambda chunks, ops: _gdn_prep(chunks, q_ref, k_ref, v_ref, grow_ref, gcol_ref, ops)
    phase2 = lambda chunks: _gdn_phase2(chunks, gcol_ref, state_ref, u_ref, wq_ref, ak_ref, o_ref)
    ops = new_ops()
    for _ in prep(sweeps[0], ops):
        pass
    for i, sweep in enumerate(sweeps):
        next_ops = new_ops()
        side = []
        if i + 1 < len(sweeps):
            side.append(prep(sweeps[i + 1], next_ops))
        if i > 0:
            side.append(phase2(sweeps[i - 1]))
        _interleave(_gdn_stages(ops, u_ref, wq_ref, ak_ref), _round_robin(side))
        ops = next_ops
    for _ in phase2(sweeps[-1]):
        pass


def _gdn(q3, k3, v3, grow3, gcol3):
    b, s, _ = q3.shape
    tiles_per_batch = s // TM_GDN
    units = (TM_GDN // GDN_CHUNK) * DN_HEADS
    tile = lambda width: pl.BlockSpec((None, TM_GDN, width), lambda bi, si: (bi, si, 0))
    return pl.pallas_call(
        _gdn_kernel,
        grid=(b, tiles_per_batch),
        in_specs=[tile(DN_WIDTH), tile(DN_WIDTH), tile(DN_WIDTH),
                  pl.BlockSpec((None, 8, TM_GDN), lambda bi, si: (bi * tiles_per_batch + si, 0, 0)),
                  tile(LANES)],
        out_specs=tile(DN_WIDTH),
        out_shape=jax.ShapeDtypeStruct((b, s, DN_WIDTH), F32),
        scratch_shapes=[
            pltpu.VMEM((DN_HEADS, DN_HEAD_DIM, DN_HEAD_DIM), F32),
            pltpu.VMEM((units, GDN_CHUNK, DN_HEAD_DIM), F32),
            pltpu.VMEM((units, 2 * GDN_CHUNK, DN_HEAD_DIM), BF16),
            pltpu.VMEM((units, 2 * GDN_CHUNK, DN_HEAD_DIM), BF16),
        ],
        compiler_params=pltpu.CompilerParams(
            dimension_semantics=("arbitrary", "arbitrary"), vmem_limit_bytes=VMEM_LIMIT),
        name="gdn",
    )(q3, k3, v3, grow3, gcol3)


def _mem_kv_kernel(m_ref, g_ref, wk_ref, wv_ref, kt_ref, v_ref):
    mn = _rms(m_ref[...], g_ref[...]).astype(BF16)
    kt_ref[...] = _dot(mn, wk_ref[...]).T.astype(kt_ref.dtype)
    v_ref[...] = _dot(mn, wv_ref[...]).astype(v_ref.dtype)


def _mem_kv(mem, g, w_k, w_v):
    b, mem_len, _ = mem.shape
    return pl.pallas_call(
        _mem_kv_kernel,
        grid=(b,),
        in_specs=[
            pl.BlockSpec((None, mem_len, D_MODEL), lambda i: (i, 0, 0)),
            _const_spec((1, D_MODEL)),
            _const_spec((D_MODEL, D_MODEL)),
            _const_spec((D_MODEL, D_MODEL)),
        ],
        out_specs=[
            pl.BlockSpec((None, D_MODEL, mem_len), lambda i: (i, 0, 0)),
            pl.BlockSpec((None, mem_len, D_MODEL), lambda i: (i, 0, 0)),
        ],
        out_shape=[
            jax.ShapeDtypeStruct((b, D_MODEL, mem_len), BF16),
            jax.ShapeDtypeStruct((b, mem_len, D_MODEL), BF16),
        ],
        compiler_params=pltpu.CompilerParams(
            dimension_semantics=("arbitrary",), vmem_limit_bytes=VMEM_LIMIT),
        name="mem_kv",
    )(mem, g, w_k, w_v)


def _post_mixer_out(t_in_batch, u_ref, o_ref, z_ref, wpool_ref, pscale_ref, og_ref, y_ref, extp_ref):
    tm = TM
    extp_ref[POOL_HALO:POOL_HALO + tm, :] = u_ref[...]
    t_glob = lax.broadcasted_iota(jnp.int32, (tm, POOL_GROUP_DIM), 0) + t_in_batch * tm
    for gi, win in enumerate(POOL_WINDOWS):
        lo = gi * POOL_GROUP_DIM
        u = extp_ref[POOL_HALO:POOL_HALO + tm, lo:lo + POOL_GROUP_DIM]
        acc = u
        for j in range(1, win):
            acc = acc + extp_ref[POOL_HALO - j:POOL_HALO - j + tm, lo:lo + POOL_GROUP_DIM]
        cnt = jnp.minimum(t_glob + 1, win).astype(F32)
        pooled = acc / cnt - u
        mixed = _dot(pooled.astype(BF16), wpool_ref[gi])
        y_ref[:, lo:lo + POOL_GROUP_DIM] = (mixed * pscale_ref[:, lo:lo + POOL_GROUP_DIM]).astype(BF16)
        yield
    extp_ref[0:POOL_HALO, :] = extp_ref[tm:tm + POOL_HALO, :]
    for h in range(DN_HEADS):
        lo = h * DN_HEAD_DIM
        o = o_ref[:, lo:lo + DN_HEAD_DIM]
        on = o * lax.rsqrt(jnp.mean(o * o, axis=-1, keepdims=True) + EPS) * og_ref[...]
        y_ref[:, POOL_WIDTH + lo:POOL_WIDTH + lo + DN_HEAD_DIM] = (
            on * _silu(z_ref[:, lo:lo + DN_HEAD_DIM])).astype(BF16)
        yield


def _post_attn(x_ref, y_ref, kt_ref, v_ref, wout_ref, gxa_ref, wq_ref, wo_ref):
    h1 = x_ref[...] + _dot(y_ref[...], wout_ref[...])
    q = _dot(_rms(h1, gxa_ref[...]).astype(BF16), wq_ref[...]).astype(BF16)
    head_lanes = [slice(hd * XA_HEAD_DIM, (hd + 1) * XA_HEAD_DIM) for hd in range(XA_HEADS)]
    scores = [_dot(q[:, hl], kt_ref[hl, :]) * (XA_HEAD_DIM ** -0.5) for hl in head_lanes]
    expd = [jnp.exp(s - jnp.max(s, axis=-1, keepdims=True)) for s in scores]
    pv = [_dot(e.astype(BF16), v_ref[:, hl]) for e, hl in zip(expd, head_lanes)]
    att = jnp.concatenate([(o / jnp.sum(e, axis=-1, keepdims=True)).astype(BF16) for e, o in zip(expd, pv)],
                          axis=-1)
    return h1 + _dot(att, wo_ref[...])


def _post_mlp(h2, gmlp_ref, wup_ref, wdn_ref, gfin_ref, out_ref):
    hn2 = _rms(h2, gmlp_ref[...]).astype(BF16)
    acc = h2
    for j in range(D_FF // FF_BLOCK):
        a = jnp.maximum(_dot(hn2, wup_ref[:, j * FF_BLOCK:(j + 1) * FF_BLOCK]), 0.0)
        yield
        acc = acc + _dot((a * a).astype(BF16), wdn_ref[j * FF_BLOCK:(j + 1) * FF_BLOCK, :])
        yield
    out_ref[...] = _rms(acc, gfin_ref[...])


def _post_kernel(n_tiles, tiles_per_batch, x_ref, u_ref, o_ref, z_ref, kt_ref, v_ref,
                 wpool_ref, pscale_ref, og_ref, wout_ref, gxa_ref, wq_ref, wo_ref, gmlp_ref,
                 wup_ref, wdn_ref, gfin_ref, out_ref, extp_ref, y_ring):
    s = pl.program_id(0)
    t_in_batch = lax.rem(jnp.minimum(s, n_tiles - 1), jnp.int32(tiles_per_batch))
    wslot = lax.rem(s, jnp.int32(2))

    @pl.when(s == 0)
    def _():
        y_ring[1] = jnp.zeros(y_ring.shape[1:], y_ring.dtype)

    @pl.when(t_in_batch == 0)
    def _():
        extp_ref[0:POOL_HALO, :] = jnp.zeros((POOL_HALO, POOL_WIDTH), F32)

    h2 = _post_attn(x_ref, y_ring.at[1 - wslot], kt_ref, v_ref, wout_ref, gxa_ref, wq_ref, wo_ref)
    mixer_out = _post_mixer_out(t_in_batch, u_ref, o_ref, z_ref, wpool_ref, pscale_ref, og_ref,
                                y_ring.at[wslot], extp_ref)
    _interleave(_post_mlp(h2, gmlp_ref, wup_ref, wdn_ref, gfin_ref, out_ref), mixer_out)


def _post(x2, u, o, z, kt3, v3, w_pool, pool_scale, o_norm_g, w_out, g_xa, w_q, w_o, g_mlp, w_up, w_dn,
          g_fin, seq):
    t = x2.shape[0]
    mem_len = v3.shape[1]
    n_tiles = t // TM
    tiles_per_batch = seq // TM
    ahead = lambda width: pl.BlockSpec((TM, width), lambda s: (jnp.minimum(s, n_tiles - 1), 0))
    main_tile = lambda s: jnp.maximum(s - 1, 0)
    return pl.pallas_call(
        functools.partial(_post_kernel, n_tiles, tiles_per_batch),
        grid=(n_tiles + 1,),
        in_specs=[
            pl.BlockSpec((TM, D_MODEL), lambda s: (main_tile(s), 0)),
            ahead(POOL_WIDTH), ahead(DN_WIDTH), ahead(DN_WIDTH),
            pl.BlockSpec((None, D_MODEL, mem_len), lambda s: (main_tile(s) // tiles_per_batch, 0, 0)),
            pl.BlockSpec((None, mem_len, D_MODEL), lambda s: (main_tile(s) // tiles_per_batch, 0, 0)),
            _const_spec((len(POOL_WINDOWS), POOL_GROUP_DIM, POOL_GROUP_DIM)),
            _const_spec((1, POOL_WIDTH)),
            _const_spec((1, DN_HEAD_DIM)),
            _const_spec((D_MODEL, D_MODEL)),
            _const_spec((1, D_MODEL)),
            _const_spec((D_MODEL, D_MODEL)),
            _const_spec((D_MODEL, D_MODEL)),
            _const_spec((1, D_MODEL)),
            _const_spec((D_MODEL, D_FF)),
            _const_spec((D_FF, D_MODEL)),
            _const_spec((1, D_MODEL)),
        ],
        out_specs=pl.BlockSpec((TM, D_MODEL), lambda s: (main_tile(s), 0)),
        out_shape=jax.ShapeDtypeStruct((t, D_MODEL), F32),
        scratch_shapes=[
            pltpu.VMEM((POOL_HALO + TM, POOL_WIDTH), F32),
            pltpu.VMEM((2, TM, D_MODEL), BF16),
        ],
        compiler_params=pltpu.CompilerParams(
            dimension_semantics=("arbitrary",), vmem_limit_bytes=VMEM_LIMIT),
        name="post",
    )(x2, u, o, z, kt3, v3, w_pool, pool_scale, o_norm_g, w_out, g_xa, w_q, w_o, g_mlp, w_up, w_dn, g_fin)


def kernel(x, mem, norm_mix_g, w_in, w_pool, pool_scale, conv_w, a_log, dt_bias, dn_out_norm_g, w_out,
           norm_xattn_g, mem_norm_g, w_xq, w_xk, w_xv, w_xo, norm_mlp_g, w_up, w_down, final_norm_g):
    b, s, d = x.shape
    assert w_in.shape[0] == 1, "single-layer problem: the final RMSNorm is fused into the layer's post stage"
    row = lambda v: v.reshape(1, -1).astype(F32)

    w_main = w_in[0][:, :MAIN_COLS].astype(BF16)
    w_ba = jnp.pad(w_in[0][:, MAIN_COLS:], ((0, 0), (0, LANES - 2 * DN_HEADS))).astype(BF16)
    gate_params = jnp.zeros((8, LANES), F32)
    gate_params = gate_params.at[DN_HEADS:, 0].set(a_log[0]).at[DN_HEADS:, 1].set(dt_bias[0])

    x2 = x.reshape(b * s, d)
    u, z, q, k, v, grow, gcol = _in_proj(x2, row(norm_mix_g[0]), w_main, w_ba, conv_w[0].astype(F32),
                                         gate_params, s)
    per_batch = lambda a: a.reshape(b, s, a.shape[-1])
    o = _gdn(per_batch(q), per_batch(k), per_batch(v), grow, per_batch(gcol))
    kt3, v3 = _mem_kv(mem, row(mem_norm_g[0]), w_xk[0].astype(BF16), w_xv[0].astype(BF16))
    out = _post(x2, u, o.reshape(b * s, DN_WIDTH), z, kt3, v3, w_pool[0].astype(BF16), row(pool_scale[0]),
                row(dn_out_norm_g[0]), w_out[0].astype(BF16), row(norm_xattn_g[0]), w_xq[0].astype(BF16),
                w_xo[0].astype(BF16), row(norm_mlp_g[0]), w_up[0].astype(BF16), w_down[0].astype(BF16),
                row(final_norm_g), s)
    return out.reshape(b, s, d)
```

```python
import functools

import jax
import jax.numpy as jnp
from jax import lax
from jax.experimental import pallas as pl
from jax.experimental.pallas import tpu as pltpu

D_MODEL = 1024
POOL_WIDTH = 512
POOL_WINDOWS = (2, 4, 8, 16)
POOL_GROUP_DIM = 128
DN_WIDTH = 512
DN_HEADS = 4
DN_HEAD_DIM = 128
CONV_K = 4
MAIN_COLS = POOL_WIDTH + 4 * DN_WIDTH
QKV_OFF = POOL_WIDTH
Z_OFF = POOL_WIDTH + 3 * DN_WIDTH
XA_HEADS = 4
XA_HEAD_DIM = 256
D_FF = 4096
EPS = 1e-6

LANES = 128
GDN_CHUNK = 128
POOL_HALO = 16
CONV_HALO = 8
TM = 512
TM_GDN = 1024
CHUNKS_PER_SWEEP = 2
FF_BLOCK = 1024
N_IN_PROJ_SEGMENTS_BEFORE_PREP = 4
PREP_SEGMENTS_PER_IN_PROJ_SEGMENT = 7
VMEM_LIMIT = 56 * 1024 * 1024

F32 = jnp.float32
BF16 = jnp.bfloat16


def _dot(a, b):
    return jnp.dot(a, b, preferred_element_type=F32)


def _rms(x, g):
    return x * lax.rsqrt(jnp.mean(x * x, axis=-1, keepdims=True) + EPS) * g


def _silu(x):
    return x * jax.nn.sigmoid(x)


def _const_spec(shape):
    zeros = (0,) * len(shape)
    return pl.BlockSpec(shape, lambda *_: zeros, pipeline_mode=pl.Buffered(1))


def _round_robin(generators):
    live = list(generators)
    while live:
        for g in list(live):
            try:
                yield next(g)
            except StopIteration:
                live.remove(g)


def _interleave(first, second, second_per_first=1):
    for _ in first:
        for _ in range(second_per_first):
            next(second, None)
    for _ in second:
        pass


def _in_proj_matmuls(x_ref, g_ref, wm_ref, wba_ref, u_ref, z_ref, hn_ref, delayed_ref, ba_ref):
    hn_ref[...] = _rms(x_ref[...], g_ref[...]).astype(BF16)
    ba_ref[...] = _dot(hn_ref[...], wba_ref[...])
    yield
    for part in range(3):
        lanes = slice(part * DN_WIDTH, (part + 1) * DN_WIDTH)
        res = _dot(hn_ref[...], wm_ref[:, QKV_OFF + part * DN_WIDTH:QKV_OFF + (part + 1) * DN_WIDTH])
        for d in range(CONV_K):
            if d:
                delayed_ref[d, 0:CONV_HALO, lanes] = delayed_ref[d, TM:TM + CONV_HALO, lanes]
            delayed_ref[d, d:d + TM, lanes] = res
        yield
    u_ref[...] = _dot(hn_ref[...], wm_ref[:, 0:POOL_WIDTH])
    yield
    z_ref[...] = _dot(hn_ref[...], wm_ref[:, Z_OFF:Z_OFF + DN_WIDTH])
    yield


def _in_proj_prep(delayed_ref, ba_ref, convw_ref, gp_ref, q_ref, k_ref, v_ref, grow_ref, gcol_ref):
    c = GDN_CHUNK

    bat = ba_ref[...].T[0:8, :]
    gp = gp_ref[...]
    a_log = gp[:, 0:1]
    dt_bias = gp[:, 1:2]
    xa = bat + dt_bias
    softplus = jnp.maximum(xa, 0.0) + jnp.log1p(jnp.exp(-jnp.abs(xa)))
    cs = -jnp.exp(a_log) * softplus
    lane = lax.broadcasted_iota(jnp.int32, (8, TM), 1) & (c - 1)
    sh = 1
    while sh < c:
        cs = cs + jnp.where(lane >= sh, pltpu.roll(cs, sh, axis=1), 0.0)
        sh *= 2
    row8 = lax.broadcasted_iota(jnp.int32, (8, TM), 0)
    stack = jnp.where(row8 < DN_HEADS, jax.nn.sigmoid(bat), cs)
    grow_ref[...] = stack
    gcol_ref[...] = jnp.concatenate([stack, jnp.zeros((LANES - 8, TM), F32)], axis=0).T
    yield

    q_scale = DN_HEAD_DIM ** -0.5
    for part, (dst, scale) in enumerate(((q_ref, q_scale), (k_ref, 1.0), (v_ref, None))):
        lanes = slice(part * DN_WIDTH, (part + 1) * DN_WIDTH)
        for r0 in range(0, TM, c):
            xc = convw_ref[CONV_K - 1:CONV_K, lanes] * delayed_ref[0, r0:r0 + c, lanes]
            for d in range(1, CONV_K):
                xc = xc + convw_ref[CONV_K - 1 - d:CONV_K - d, lanes] * delayed_ref[d, r0:r0 + c, lanes]
            act = _silu(xc)
            if scale is None:
                dst[r0:r0 + c, :] = act
            else:
                for h in range(DN_HEADS):
                    lo = h * DN_HEAD_DIM
                    a = act[:, lo:lo + DN_HEAD_DIM]
                    a = a * lax.rsqrt(jnp.sum(a * a, axis=-1, keepdims=True) + EPS)
                    dst[r0:r0 + c, lo:lo + DN_HEAD_DIM] = a if scale == 1.0 else a * scale
            yield


def _in_proj_kernel(tiles_per_batch, x_ref, g_ref, wm_ref, wba_ref, convw_ref, gp_ref,
                    u_ref, z_ref, q_ref, k_ref, v_ref, grow_ref, gcol_ref,
                    hn_ref, delayed_ref, ba_ref):

    @pl.when(lax.rem(pl.program_id(0), jnp.int32(tiles_per_batch)) == 0)
    def _():
        delayed_ref[:, TM:TM + CONV_HALO, :] = jnp.zeros((CONV_K, CONV_HALO, 3 * DN_WIDTH), F32)

    matmuls = _in_proj_matmuls(x_ref, g_ref, wm_ref, wba_ref, u_ref, z_ref, hn_ref, delayed_ref, ba_ref)
    prep = _in_proj_prep(delayed_ref, ba_ref, convw_ref, gp_ref, q_ref, k_ref, v_ref, grow_ref, gcol_ref)
    for _ in range(N_IN_PROJ_SEGMENTS_BEFORE_PREP):
        next(matmuls)
    _interleave(matmuls, prep, second_per_first=PREP_SEGMENTS_PER_IN_PROJ_SEGMENT)


def _in_proj(x2, g, w_main, w_ba, conv_w, gate_params, seq):
    t = x2.shape[0]
    n_tiles = t // TM
    tiles_per_gdn_tile = TM_GDN // TM
    tile = lambda width: pl.BlockSpec((TM, width), lambda i: (i, 0))
    wide = jax.ShapeDtypeStruct((t, DN_WIDTH), F32)
    return pl.pallas_call(
        functools.partial(_in_proj_kernel, seq // TM),
        grid=(n_tiles,),
        in_specs=[
            tile(D_MODEL),
            _const_spec((1, D_MODEL)),
            _const_spec((D_MODEL, MAIN_COLS)),
            _const_spec((D_MODEL, LANES)),
            _const_spec((CONV_K, 3 * DN_WIDTH)),
            _const_spec((8, LANES)),
        ],
        out_specs=[
            tile(POOL_WIDTH), tile(DN_WIDTH), tile(DN_WIDTH), tile(DN_WIDTH), tile(DN_WIDTH),
            pl.BlockSpec((None, 8, TM), lambda i: (i // tiles_per_gdn_tile, 0, i % tiles_per_gdn_tile)),
            tile(LANES),
        ],
        out_shape=[
            jax.ShapeDtypeStruct((t, POOL_WIDTH), F32), wide, wide, wide, wide,
            jax.ShapeDtypeStruct((t // TM_GDN, 8, TM_GDN), F32),
            jax.ShapeDtypeStruct((t, LANES), F32),
        ],
        scratch_shapes=[
            pltpu.VMEM((TM, D_MODEL), BF16),
            pltpu.VMEM((CONV_K, TM + CONV_HALO, 3 * DN_WIDTH), F32),
            pltpu.VMEM((TM, LANES), F32),
        ],
        compiler_params=pltpu.CompilerParams(
            dimension_semantics=("arbitrary",), vmem_limit_bytes=VMEM_LIMIT),
        name="in_proj",
    )(x2, g, w_main, w_ba, conv_w, gate_params)


def _gdn_prep(chunks, q_ref, k_ref, v_ref, grow_ref, gcol_ref, ops):
    c = GDN_CHUNK
    ii = lax.broadcasted_iota(jnp.int32, (c, c), 0)
    jj = lax.broadcasted_iota(jnp.int32, (c, c), 1)
    tril = ii >= jj
    strict = ii > jj

    def col(r0, j):
        return jnp.broadcast_to(gcol_ref[r0:r0 + c, j:j + 1], (c, c))

    for ci in chunks:
        for h in range(DN_HEADS):
            r0 = ci * c
            lo = h * DN_HEAD_DIM
            qn = q_ref[r0:r0 + c, lo:lo + DN_HEAD_DIM]
            kn = k_ref[r0:r0 + c, lo:lo + DN_HEAD_DIM]
            vh = v_ref[r0:r0 + c, lo:lo + DN_HEAD_DIM]
            beta_b = col(r0, h)
            gc_b = col(r0, DN_HEADS + h)
            gc_r = grow_ref[DN_HEADS + h:DN_HEADS + h + 1, r0:r0 + c]
            g_last = gcol_ref[r0 + c - 1:r0 + c, DN_HEADS + h:DN_HEADS + h + 1]
            d = jnp.where(tril, jnp.exp(gc_b - gc_r), 0.0)
            eg = jnp.exp(gc_b)
            kb = kn * beta_b
            kt = kn.T
            ops["unit"].append((ci, h))
            ops["dec"].append(d)
            ops["neg_d"].append(jnp.where(strict, -d, 0.0))
            ops["kt16"].append(kt.astype(BF16))
            ops["lhs_kq"].append(jnp.concatenate([kb, qn], axis=0).astype(BF16))
            ops["vbkg"].append(jnp.concatenate([vh * beta_b, kb * eg], axis=1).astype(BF16))
            ops["qg16"].append((qn * eg).astype(BF16))
            ops["kdt16"].append((kt * jnp.exp(g_last - gc_r)).astype(BF16))
            if h % 2 == 1:
                yield


def _gdn_stages(ops, u_ref, wq_ref, ak_ref):
    c = GDN_CHUNK
    ii = lax.broadcasted_iota(jnp.int32, (c, c), 0)
    jj = lax.broadcasted_iota(jnp.int32, (c, c), 1)
    eye = jnp.where(ii == jj, 1.0, 0.0).astype(F32)
    n = len(ops["unit"])
    kq = [_dot(ops["lhs_kq"][i], ops["kt16"][i]) for i in range(n)]
    yield
    xs = [kq[i][:c] * ops["neg_d"][i] for i in range(n)]
    ps = [eye + xs[i] for i in range(n)]
    att16 = [(kq[i][c:] * ops["dec"][i]).astype(BF16) for i in range(n)]
    x16 = [x.astype(BF16) for x in xs]
    xs = [_dot(x, x) for x in x16]
    yield
    span = 2
    while 2 * span < c:
        x16 = [x.astype(BF16) for x in xs]
        both = [_dot(jnp.concatenate([ps[i].astype(BF16), x16[i]], axis=0), x16[i]) for i in range(n)]
        ps = [ps[i] + both[i][:c] for i in range(n)]
        xs = [both[i][c:] for i in range(n)]
        span *= 2
        yield
    ps = [ps[i] + _dot(ps[i].astype(BF16), xs[i].astype(BF16)) for i in range(n)]
    yield
    uw = [_dot(ps[i].astype(BF16), ops["vbkg"][i]) for i in range(n)]
    for i, (ci, h) in enumerate(ops["unit"]):
        idx = ci * DN_HEADS + h
        u_ref[idx] = uw[i][:, :DN_HEAD_DIM]
        wq_ref[idx] = jnp.concatenate([uw[i][:, DN_HEAD_DIM:].astype(BF16), ops["qg16"][i]], axis=0)
        ak_ref[idx] = jnp.concatenate([att16[i], ops["kdt16"][i]], axis=0)
    yield


def _pool_mixer(t0, u_ref, wpool_ref, pscale_ref, y_ref, extp_ref):
    tm = u_ref.shape[0]
    extp_ref[POOL_HALO:POOL_HALO + tm, :] = u_ref[...]
    t_glob = lax.broadcasted_iota(jnp.int32, (tm, POOL_GROUP_DIM), 0) + t0
    for gi, win in enumerate(POOL_WINDOWS):
        lo = gi * POOL_GROUP_DIM
        acc = extp_ref[:, lo:lo + POOL_GROUP_DIM]
        shift = 1
        while shift < win:
            acc = acc + pltpu.roll(acc, shift, axis=0)
            shift *= 2
        acc = acc[POOL_HALO:, :]
        u = extp_ref[POOL_HALO:POOL_HALO + tm, lo:lo + POOL_GROUP_DIM]
        cnt = jnp.minimum(t_glob + 1, win).astype(F32)
        pooled = acc / cnt - u
        mixed = _dot(pooled.astype(BF16), wpool_ref[gi])
        y_ref[:, lo:lo + POOL_GROUP_DIM] = (mixed * pscale_ref[:, lo:lo + POOL_GROUP_DIM]).astype(y_ref.dtype)
        yield
    extp_ref[0:POOL_HALO, :] = extp_ref[tm:tm + POOL_HALO, :]


def _gdn_phase2(chunks, gcol_ref, z_ref, og_ref, state_ref, u_ref, wq_ref, ak_ref, y_ref):
    c = GDN_CHUNK
    heads = range(DN_HEADS)
    for ci in chunks:
        r0 = ci * c
        st = [state_ref[h] for h in heads]
        st16 = [s.astype(BF16) for s in st]
        r1 = [_dot(wq_ref[ci * DN_HEADS + h], st16[h]) for h in heads]
        yield
        vn16 = [(u_ref[ci * DN_HEADS + h] - r1[h][:c]).astype(BF16) for h in heads]
        r2 = [_dot(ak_ref[ci * DN_HEADS + h], vn16[h]) for h in heads]
        yield
        for h in heads:
            lo = h * DN_HEAD_DIM
            g_last = gcol_ref[r0 + c - 1:r0 + c, DN_HEADS + h:DN_HEADS + h + 1]
            state_ref[h] = st[h] * jnp.exp(g_last) + r2[h][c:]
            o = r1[h][c:] + r2[h][:c]
            on = o * lax.rsqrt(jnp.mean(o * o, axis=-1, keepdims=True) + EPS) * og_ref[...]
            y_ref[r0:r0 + c, POOL_WIDTH + lo:POOL_WIDTH + lo + DN_HEAD_DIM] = (
                on * _silu(z_ref[r0:r0 + c, lo:lo + DN_HEAD_DIM])).astype(y_ref.dtype)


def _gdn_kernel(q_ref, k_ref, v_ref, grow_ref, gcol_ref, z_ref, up_ref, wpool_ref, pscale_ref, og_ref, y_ref,
                state_ref, u_ref, wq_ref, ak_ref, extp_ref):
    @pl.when(pl.program_id(1) == 0)
    def _():
        state_ref[...] = jnp.zeros_like(state_ref)
        extp_ref[0:POOL_HALO, :] = jnp.zeros((POOL_HALO, POOL_WIDTH), F32)

    n_chunks = TM_GDN // GDN_CHUNK
    sweeps = [range(g0, g0 + CHUNKS_PER_SWEEP) for g0 in range(0, n_chunks, CHUNKS_PER_SWEEP)]
    new_ops = lambda: {name: [] for name in ("unit", "dec", "neg_d", "kt16", "lhs_kq", "vbkg", "qg16", "kdt16")}
    prep = lambda chunks, ops: _gdn_prep(chunks, q_ref, k_ref, v_ref, grow_ref, gcol_ref, ops)
    phase2 = lambda chunks: _gdn_phase2(chunks, gcol_ref, z_ref, og_ref, state_ref, u_ref, wq_ref, ak_ref, y_ref)
    pool = _pool_mixer(pl.program_id(1) * TM_GDN, up_ref, wpool_ref, pscale_ref, y_ref, extp_ref)
    ops = new_ops()
    for _ in prep(sweeps[0], ops):
        pass
    for i, sweep in enumerate(sweeps):
        next_ops = new_ops()
        side = [pool]
        if i + 1 < len(sweeps):
            side.append(prep(sweeps[i + 1], next_ops))
        if i > 0:
            side.append(phase2(sweeps[i - 1]))
        _interleave(_gdn_stages(ops, u_ref, wq_ref, ak_ref), _round_robin(side))
        ops = next_ops
    for _ in phase2(sweeps[-1]):
        pass


def _gdn(q3, k3, v3, grow3, gcol3, z3, up3, w_pool, pool_scale, o_norm_g):
    b, s, _ = q3.shape
    tiles_per_batch = s // TM_GDN
    units = (TM_GDN // GDN_CHUNK) * DN_HEADS
    tile = lambda width: pl.BlockSpec((None, TM_GDN, width), lambda bi, si: (bi, si, 0))
    return pl.pallas_call(
        _gdn_kernel,
        grid=(b, tiles_per_batch),
        in_specs=[tile(DN_WIDTH), tile(DN_WIDTH), tile(DN_WIDTH),
                  pl.BlockSpec((None, 8, TM_GDN), lambda bi, si: (bi * tiles_per_batch + si, 0, 0)),
                  tile(LANES), tile(DN_WIDTH), tile(POOL_WIDTH),
                  _const_spec((len(POOL_WINDOWS), POOL_GROUP_DIM, POOL_GROUP_DIM)),
                  _const_spec((1, POOL_WIDTH)),
                  _const_spec((1, DN_HEAD_DIM))],
        out_specs=tile(D_MODEL),
        out_shape=jax.ShapeDtypeStruct((b, s, D_MODEL), BF16),
        scratch_shapes=[
            pltpu.VMEM((DN_HEADS, DN_HEAD_DIM, DN_HEAD_DIM), F32),
            pltpu.VMEM((units, GDN_CHUNK, DN_HEAD_DIM), F32),
            pltpu.VMEM((units, 2 * GDN_CHUNK, DN_HEAD_DIM), BF16),
            pltpu.VMEM((units, 2 * GDN_CHUNK, DN_HEAD_DIM), BF16),
            pltpu.VMEM((POOL_HALO + TM_GDN, POOL_WIDTH), F32),
        ],
        compiler_params=pltpu.CompilerParams(
            dimension_semantics=("arbitrary", "arbitrary"), vmem_limit_bytes=VMEM_LIMIT),
        name="gdn",
    )(q3, k3, v3, grow3, gcol3, z3, up3, w_pool, pool_scale, o_norm_g)


def _mem_kv_kernel(m_ref, g_ref, wk_ref, wv_ref, kt_ref, v_ref):
    mn = _rms(m_ref[...], g_ref[...]).astype(BF16)
    kt_ref[...] = _dot(mn, wk_ref[...]).T.astype(kt_ref.dtype)
    v_ref[...] = _dot(mn, wv_ref[...]).astype(v_ref.dtype)


def _mem_kv(mem, g, w_k, w_v):
    b, mem_len, _ = mem.shape
    return pl.pallas_call(
        _mem_kv_kernel,
        grid=(b,),
        in_specs=[
            pl.BlockSpec((None, mem_len, D_MODEL), lambda i: (i, 0, 0)),
            _const_spec((1, D_MODEL)),
            _const_spec((D_MODEL, D_MODEL)),
            _const_spec((D_MODEL, D_MODEL)),
        ],
        out_specs=[
            pl.BlockSpec((None, D_MODEL, mem_len), lambda i: (i, 0, 0)),
            pl.BlockSpec((None, mem_len, D_MODEL), lambda i: (i, 0, 0)),
        ],
        out_shape=[
            jax.ShapeDtypeStruct((b, D_MODEL, mem_len), BF16),
            jax.ShapeDtypeStruct((b, mem_len, D_MODEL), BF16),
        ],
        compiler_params=pltpu.CompilerParams(
            dimension_semantics=("arbitrary",), vmem_limit_bytes=VMEM_LIMIT),
        name="mem_kv",
    )(mem, g, w_k, w_v)


def _post_kernel(x_ref, y_ref, kt_ref, v_ref, wout_ref, gxa_ref, wq_ref, wo_ref, gmlp_ref,
                 wup_ref, wdn_ref, gfin_ref, out_ref):
    h1 = x_ref[...] + _dot(y_ref[...], wout_ref[...])
    q = _dot(_rms(h1, gxa_ref[...]).astype(BF16), wq_ref[...]).astype(BF16)
    head_lanes = [slice(hd * XA_HEAD_DIM, (hd + 1) * XA_HEAD_DIM) for hd in range(XA_HEADS)]
    scores = [_dot(q[:, hl], kt_ref[hl, :]) * (XA_HEAD_DIM ** -0.5) for hl in head_lanes]
    expd = [jnp.exp(s - jnp.max(s, axis=-1, keepdims=True)) for s in scores]
    pv = [_dot(e.astype(BF16), v_ref[:, hl]) for e, hl in zip(expd, head_lanes)]
    att = jnp.concatenate([(o / jnp.sum(e, axis=-1, keepdims=True)).astype(BF16) for e, o in zip(expd, pv)],
                          axis=-1)
    h2 = h1 + _dot(att, wo_ref[...])
    hn2 = _rms(h2, gmlp_ref[...]).astype(BF16)
    acc = h2
    for j in range(D_FF // FF_BLOCK):
        a = jnp.maximum(_dot(hn2, wup_ref[:, j * FF_BLOCK:(j + 1) * FF_BLOCK]), 0.0)
        acc = acc + _dot((a * a).astype(BF16), wdn_ref[j * FF_BLOCK:(j + 1) * FF_BLOCK, :])
    out_ref[...] = _rms(acc, gfin_ref[...])


def _post(x2, y2, kt3, v3, w_out, g_xa, w_q, w_o, g_mlp, w_up, w_dn, g_fin, seq):
    t = x2.shape[0]
    mem_len = v3.shape[1]
    tiles_per_batch = seq // TM
    tile = pl.BlockSpec((TM, D_MODEL), lambda i: (i, 0))
    return pl.pallas_call(
        _post_kernel,
        grid=(t // TM,),
        in_specs=[
            tile, tile,
            pl.BlockSpec((None, D_MODEL, mem_len), lambda i: (i // tiles_per_batch, 0, 0)),
            pl.BlockSpec((None, mem_len, D_MODEL), lambda i: (i // tiles_per_batch, 0, 0)),
            _const_spec((D_MODEL, D_MODEL)),
            _const_spec((1, D_MODEL)),
            _const_spec((D_MODEL, D_MODEL)),
            _const_spec((D_MODEL, D_MODEL)),
            _const_spec((1, D_MODEL)),
            _const_spec((D_MODEL, D_FF)),
            _const_spec((D_FF, D_MODEL)),
            _const_spec((1, D_MODEL)),
        ],
        out_specs=tile,
        out_shape=jax.ShapeDtypeStruct((t, D_MODEL), F32),
        compiler_params=pltpu.CompilerParams(
            dimension_semantics=("arbitrary",), vmem_limit_bytes=VMEM_LIMIT),
        name="post",
    )(x2, y2, kt3, v3, w_out, g_xa, w_q, w_o, g_mlp, w_up, w_dn, g_fin)


def kernel(x, mem, norm_mix_g, w_in, w_pool, pool_scale, conv_w, a_log, dt_bias, dn_out_norm_g, w_out,
           norm_xattn_g, mem_norm_g, w_xq, w_xk, w_xv, w_xo, norm_mlp_g, w_up, w_down, final_norm_g):
    b, s, d = x.shape
    assert w_in.shape[0] == 1, "single-layer problem: the final RMSNorm is fused into the layer's post stage"
    row = lambda v: v.reshape(1, -1).astype(F32)

    w_main = w_in[0][:, :MAIN_COLS].astype(BF16)
    w_ba = jnp.pad(w_in[0][:, MAIN_COLS:], ((0, 0), (0, LANES - 2 * DN_HEADS))).astype(BF16)
    gate_params = jnp.zeros((8, LANES), F32)
    gate_params = gate_params.at[DN_HEADS:, 0].set(a_log[0]).at[DN_HEADS:, 1].set(dt_bias[0])

    x2 = x.reshape(b * s, d)
    u, z, q, k, v, grow, gcol = _in_proj(x2, row(norm_mix_g[0]), w_main, w_ba, conv_w[0].astype(F32),
                                         gate_params, s)
    per_batch = lambda a: a.reshape(b, s, a.shape[-1])
    y = _gdn(per_batch(q), per_batch(k), per_batch(v), grow, per_batch(gcol), per_batch(z), per_batch(u),
             w_pool[0].astype(BF16), row(pool_scale[0]), row(dn_out_norm_g[0]))
    kt3, v3 = _mem_kv(mem, row(mem_norm_g[0]), w_xk[0].astype(BF16), w_xv[0].astype(BF16))
    out = _post(x2, y.reshape(b * s, d), kt3, v3, w_out[0].astype(BF16), row(norm_xattn_g[0]),
                w_xq[0].astype(BF16), w_xo[0].astype(BF16), row(norm_mlp_g[0]), w_up[0].astype(BF16),
                w_down[0].astype(BF16), row(final_norm_g), s)
    return out.reshape(b, s, d)
```

```python
import functools
import itertools

import jax
import jax.numpy as jnp
from jax import lax
from jax.experimental import pallas as pl
from jax.experimental.pallas import tpu as pltpu

D_MODEL = 1024
POOL_WIDTH = 512
POOL_WINDOWS = (2, 4, 8, 16)
POOL_GROUP_DIM = 128
DN_WIDTH = 512
DN_HEADS = 4
DN_HEAD_DIM = 128
CONV_K = 4
MAIN_COLS = POOL_WIDTH + 4 * DN_WIDTH
QKV_OFF = POOL_WIDTH
Z_OFF = POOL_WIDTH + 3 * DN_WIDTH
XA_HEADS = 4
XA_HEAD_DIM = 256
D_FF = 4096
EPS = 1e-6

LANES = 128
GDN_CHUNK = 128
POOL_HALO = 16
CONV_HALO = 8
TM = 512
TM_GDN = 1024
CHUNKS_PER_SWEEP = 2
FF_BLOCK = 1024
PREP_SEGMENTS_AFTER_IN_PROJ_SEGMENT = (0, 0, 5, 8)
VMEM_LIMIT = 56 * 1024 * 1024

F32 = jnp.float32
BF16 = jnp.bfloat16


def _dot(a, b):
    return jnp.dot(a, b, preferred_element_type=F32)


def _rms(x, g):
    return x * lax.rsqrt(jnp.mean(x * x, axis=-1, keepdims=True) + EPS) * g


def _silu(x):
    return x * jax.nn.sigmoid(x)


def _const_spec(shape):
    zeros = (0,) * len(shape)
    return pl.BlockSpec(shape, lambda *_: zeros, pipeline_mode=pl.Buffered(1))


def _round_robin(generators):
    live = list(generators)
    while live:
        for g in list(live):
            try:
                yield next(g)
            except StopIteration:
                live.remove(g)


def _interleave(first, second, second_per_first=1):
    for _ in first:
        for _ in range(second_per_first):
            next(second, None)
    for _ in second:
        pass


def _in_proj_matmuls(x_ref, g_ref, wqkv_ref, wba_ref, hn_ref, qkv_ref, ba_ref):
    hn_ref[...] = _rms(x_ref[...], g_ref[...]).astype(BF16)
    ba_ref[...] = _dot(hn_ref[...], wba_ref[...])
    yield
    for part in range(3):
        lanes = slice(part * DN_WIDTH, (part + 1) * DN_WIDTH)
        qkv_ref[CONV_HALO:CONV_HALO + TM, lanes] = _dot(hn_ref[...], wqkv_ref[:, lanes])
        yield


def _in_proj_prep(qkv_ref, ba_ref, convw_ref, gp_ref, q_ref, k_ref, v_ref, grow_ref, gcol_ref):
    c = GDN_CHUNK

    bat = ba_ref[...].T[0:8, :]
    gp = gp_ref[...]
    a_log = gp[:, 0:1]
    dt_bias = gp[:, 1:2]
    xa = bat + dt_bias
    softplus = jnp.maximum(xa, 0.0) + jnp.log1p(jnp.exp(-jnp.abs(xa)))
    cs = -jnp.exp(a_log) * softplus
    lane = lax.broadcasted_iota(jnp.int32, (8, TM), 1) & (c - 1)
    sh = 1
    while sh < c:
        cs = cs + jnp.where(lane >= sh, pltpu.roll(cs, sh, axis=1), 0.0)
        sh *= 2
    row8 = lax.broadcasted_iota(jnp.int32, (8, TM), 0)
    stack = jnp.where(row8 < DN_HEADS, jax.nn.sigmoid(bat), cs)
    grow_ref[...] = stack
    gcol_ref[...] = jnp.concatenate([stack, jnp.zeros((LANES - 8, TM), F32)], axis=0).T
    yield

    q_scale = DN_HEAD_DIM ** -0.5
    for part, (dst, scale) in enumerate(((q_ref, q_scale), (k_ref, 1.0), (v_ref, None))):
        for r0 in range(0, TM, c):
            for h in range(DN_HEADS):
                lo = h * DN_HEAD_DIM
                lanes = slice(part * DN_WIDTH + lo, part * DN_WIDTH + lo + DN_HEAD_DIM)
                xe = qkv_ref[r0:r0 + CONV_HALO + c, lanes]
                xc = convw_ref[CONV_K - 1:CONV_K, lanes] * xe
                for d in range(1, CONV_K):
                    xc = xc + convw_ref[CONV_K - 1 - d:CONV_K - d, lanes] * pltpu.roll(xe, d, axis=0)
                a = _silu(xc[CONV_HALO:, :])
                if scale is not None:
                    a = a * lax.rsqrt(jnp.sum(a * a, axis=-1, keepdims=True) + EPS)
                    if scale != 1.0:
                        a = a * scale
                dst[r0:r0 + c, lo:lo + DN_HEAD_DIM] = a
            yield


def _in_proj_kernel(tiles_per_batch, x_ref, g_ref, wqkv_ref, wba_ref, convw_ref, gp_ref,
                    hn_ref, q_ref, k_ref, v_ref, grow_ref, gcol_ref,
                    qkv_ref, ba_ref):
    first_of_batch = lax.rem(pl.program_id(0), jnp.int32(tiles_per_batch)) == 0

    @pl.when(first_of_batch)
    def _():
        qkv_ref[0:CONV_HALO, :] = jnp.zeros((CONV_HALO, 3 * DN_WIDTH), F32)

    @pl.when(jnp.logical_not(first_of_batch))
    def _():
        qkv_ref[0:CONV_HALO, :] = qkv_ref[TM:TM + CONV_HALO, :]

    matmuls = _in_proj_matmuls(x_ref, g_ref, wqkv_ref, wba_ref, hn_ref, qkv_ref, ba_ref)
    prep = _in_proj_prep(qkv_ref, ba_ref, convw_ref, gp_ref, q_ref, k_ref, v_ref, grow_ref, gcol_ref)
    for n_prep_segments in PREP_SEGMENTS_AFTER_IN_PROJ_SEGMENT:
        next(matmuls)
        for _ in range(n_prep_segments):
            next(prep)
    done = object()
    assert next(matmuls, done) is done and next(prep, done) is done


def _in_proj(x2, g, w_qkv, w_ba, conv_w, gate_params, seq):
    t = x2.shape[0]
    n_tiles = t // TM
    tiles_per_gdn_tile = TM_GDN // TM
    tile = lambda width: pl.BlockSpec((TM, width), lambda i: (i, 0))
    wide = jax.ShapeDtypeStruct((t, DN_WIDTH), F32)
    return pl.pallas_call(
        functools.partial(_in_proj_kernel, seq // TM),
        grid=(n_tiles,),
        in_specs=[
            tile(D_MODEL),
            _const_spec((1, D_MODEL)),
            _const_spec((D_MODEL, 3 * DN_WIDTH)),
            _const_spec((D_MODEL, LANES)),
            _const_spec((CONV_K, 3 * DN_WIDTH)),
            _const_spec((8, LANES)),
        ],
        out_specs=[
            tile(D_MODEL), tile(DN_WIDTH), tile(DN_WIDTH), tile(DN_WIDTH),
            pl.BlockSpec((None, 8, TM), lambda i: (i // tiles_per_gdn_tile, 0, i % tiles_per_gdn_tile)),
            tile(LANES),
        ],
        out_shape=[
            jax.ShapeDtypeStruct((t, D_MODEL), BF16), wide, wide, wide,
            jax.ShapeDtypeStruct((t // TM_GDN, 8, TM_GDN), F32),
            jax.ShapeDtypeStruct((t, LANES), F32),
        ],
        scratch_shapes=[
            pltpu.VMEM((CONV_HALO + TM, 3 * DN_WIDTH), F32),
            pltpu.VMEM((TM, LANES), F32),
        ],
        compiler_params=pltpu.CompilerParams(
            dimension_semantics=("arbitrary",), vmem_limit_bytes=VMEM_LIMIT),
        name="in_proj",
    )(x2, g, w_qkv, w_ba, conv_w, gate_params)


def _gdn_prep(chunks, q_ref, k_ref, v_ref, grow_ref, gcol_ref, ops):
    c = GDN_CHUNK
    ii = lax.broadcasted_iota(jnp.int32, (c, c), 0)
    jj = lax.broadcasted_iota(jnp.int32, (c, c), 1)
    tril = ii >= jj
    strict = ii > jj

    def col(r0, j):
        return jnp.broadcast_to(gcol_ref[r0:r0 + c, j:j + 1], (c, c))

    for ci in chunks:
        for h in range(DN_HEADS):
            r0 = ci * c
            lo = h * DN_HEAD_DIM
            qn = q_ref[r0:r0 + c, lo:lo + DN_HEAD_DIM]
            kn = k_ref[r0:r0 + c, lo:lo + DN_HEAD_DIM]
            vh = v_ref[r0:r0 + c, lo:lo + DN_HEAD_DIM]
            beta_b = col(r0, h)
            gc_b = col(r0, DN_HEADS + h)
            gc_r = grow_ref[DN_HEADS + h:DN_HEADS + h + 1, r0:r0 + c]
            g_last = gcol_ref[r0 + c - 1:r0 + c, DN_HEADS + h:DN_HEADS + h + 1]
            d = jnp.where(tril, jnp.exp(gc_b - gc_r), 0.0)
            eg = jnp.exp(gc_b)
            kb = kn * beta_b
            kt = kn.T
            ops["unit"].append((ci, h))
            ops["dec"].append(d)
            ops["neg_d"].append(jnp.where(strict, -d, 0.0))
            ops["kt16"].append(kt.astype(BF16))
            ops["lhs_kq"].append(jnp.concatenate([kb, qn], axis=0).astype(BF16))
            ops["vbkg"].append(jnp.concatenate([vh * beta_b, kb * eg], axis=1).astype(BF16))
            ops["qg16"].append((qn * eg).astype(BF16))
            ops["kdt16"].append((kt * jnp.exp(g_last - gc_r)).astype(BF16))
            if h % 2 == 1:
                yield


def _gdn_stages(ops, u_ref, wq_ref, ak_ref):
    c = GDN_CHUNK
    ii = lax.broadcasted_iota(jnp.int32, (c, c), 0)
    jj = lax.broadcasted_iota(jnp.int32, (c, c), 1)
    eye = jnp.where(ii == jj, 1.0, 0.0).astype(F32)
    n = len(ops["unit"])
    kq = [_dot(ops["lhs_kq"][i], ops["kt16"][i]) for i in range(n)]
    yield
    xs = [kq[i][:c] * ops["neg_d"][i] for i in range(n)]
    ps = [eye + xs[i] for i in range(n)]
    att16 = [(kq[i][c:] * ops["dec"][i]).astype(BF16) for i in range(n)]
    x16 = [x.astype(BF16) for x in xs]
    xs = [_dot(x, x) for x in x16]
    yield
    span = 2
    while 2 * span < c:
        x16 = [x.astype(BF16) for x in xs]
        both = [_dot(jnp.concatenate([ps[i].astype(BF16), x16[i]], axis=0), x16[i]) for i in range(n)]
        ps = [ps[i] + both[i][:c] for i in range(n)]
        xs = [both[i][c:] for i in range(n)]
        span *= 2
        yield
    ps = [ps[i] + _dot(ps[i].astype(BF16), xs[i].astype(BF16)) for i in range(n)]
    yield
    uw = [_dot(ps[i].astype(BF16), ops["vbkg"][i]) for i in range(n)]
    for i, (ci, h) in enumerate(ops["unit"]):
        idx = ci * DN_HEADS + h
        u_ref[idx] = uw[i][:, :DN_HEAD_DIM]
        wq_ref[idx] = jnp.concatenate([uw[i][:, DN_HEAD_DIM:].astype(BF16), ops["qg16"][i]], axis=0)
        ak_ref[idx] = jnp.concatenate([att16[i], ops["kdt16"][i]], axis=0)
    yield


def _pool_gate_proj(sweep_rows, sweep, hn_ref, wuz_ref, extp_ref, z_ref):
    half = POOL_WIDTH // 2
    n_sweeps = TM_GDN // sweep_rows
    row_blocks = lambda rb: slice(rb * sweep_rows, (rb + 1) * sweep_rows)
    for lo in (0, half):
        rows = row_blocks(sweep)
        z_ref[rows, lo:lo + half] = _dot(hn_ref[rows, :], wuz_ref[:, POOL_WIDTH + lo:POOL_WIDTH + lo + half])
        yield
    if 2 * sweep < n_sweeps:
        for rb in (2 * sweep, 2 * sweep + 1):
            for lo in (0, half):
                rows = row_blocks(rb)
                extp_ref[POOL_HALO + rb * sweep_rows:POOL_HALO + (rb + 1) * sweep_rows, lo:lo + half] = _dot(
                    hn_ref[rows, :], wuz_ref[:, lo:lo + half])
                yield


def _pool_mixer(t0, wpool_ref, pscale_ref, y_ref, extp_ref):
    tm = y_ref.shape[0]
    t_glob = lax.broadcasted_iota(jnp.int32, (tm, POOL_GROUP_DIM), 0) + t0
    for gi, win in enumerate(POOL_WINDOWS):
        lo = gi * POOL_GROUP_DIM
        acc = extp_ref[:, lo:lo + POOL_GROUP_DIM]
        shift = 1
        while shift < win:
            acc = acc + pltpu.roll(acc, shift, axis=0)
            shift *= 2
        acc = acc[POOL_HALO:, :]
        u = extp_ref[POOL_HALO:POOL_HALO + tm, lo:lo + POOL_GROUP_DIM]
        cnt = jnp.minimum(t_glob + 1, win).astype(F32)
        pooled = acc / cnt - u
        mixed = _dot(pooled.astype(BF16), wpool_ref[gi])
        y_ref[:, lo:lo + POOL_GROUP_DIM] = (mixed * pscale_ref[:, lo:lo + POOL_GROUP_DIM]).astype(y_ref.dtype)
        yield
    extp_ref[0:POOL_HALO, :] = extp_ref[tm:tm + POOL_HALO, :]


def _gdn_phase2(chunks, gcol_ref, z_ref, og_ref, state_ref, u_ref, wq_ref, ak_ref, y_ref):
    c = GDN_CHUNK
    heads = range(DN_HEADS)
    for ci in chunks:
        r0 = ci * c
        st = [state_ref[h] for h in heads]
        st16 = [s.astype(BF16) for s in st]
        r1 = [_dot(wq_ref[ci * DN_HEADS + h], st16[h]) for h in heads]
        yield
        vn16 = [(u_ref[ci * DN_HEADS + h] - r1[h][:c]).astype(BF16) for h in heads]
        r2 = [_dot(ak_ref[ci * DN_HEADS + h], vn16[h]) for h in heads]
        yield
        for h in heads:
            lo = h * DN_HEAD_DIM
            g_last = gcol_ref[r0 + c - 1:r0 + c, DN_HEADS + h:DN_HEADS + h + 1]
            state_ref[h] = st[h] * jnp.exp(g_last) + r2[h][c:]
            o = r1[h][c:] + r2[h][:c]
            on = o * lax.rsqrt(jnp.mean(o * o, axis=-1, keepdims=True) + EPS) * og_ref[...]
            y_ref[r0:r0 + c, POOL_WIDTH + lo:POOL_WIDTH + lo + DN_HEAD_DIM] = (
                on * _silu(z_ref[r0:r0 + c, lo:lo + DN_HEAD_DIM])).astype(y_ref.dtype)


def _gdn_kernel(q_ref, k_ref, v_ref, grow_ref, gcol_ref, hn_ref, wuz_ref, wpool_ref, pscale_ref, og_ref, y_ref,
                state_ref, u_ref, wq_ref, ak_ref, extp_ref, z_ref):
    @pl.when(pl.program_id(1) == 0)
    def _():
        state_ref[...] = jnp.zeros_like(state_ref)
        extp_ref[0:POOL_HALO, :] = jnp.zeros((POOL_HALO, POOL_WIDTH), F32)

    n_chunks = TM_GDN // GDN_CHUNK
    sweeps = [range(g0, g0 + CHUNKS_PER_SWEEP) for g0 in range(0, n_chunks, CHUNKS_PER_SWEEP)]
    new_ops = lambda: {name: [] for name in ("unit", "dec", "neg_d", "kt16", "lhs_kq", "vbkg", "qg16", "kdt16")}
    prep = lambda chunks, ops: _gdn_prep(chunks, q_ref, k_ref, v_ref, grow_ref, gcol_ref, ops)
    phase2 = lambda chunks: _gdn_phase2(chunks, gcol_ref, z_ref, og_ref, state_ref, u_ref, wq_ref, ak_ref, y_ref)
    pool = _pool_mixer(pl.program_id(1) * TM_GDN, wpool_ref, pscale_ref, y_ref, extp_ref)
    ops = new_ops()
    for _ in prep(sweeps[0], ops):
        pass
    for i, sweep in enumerate(sweeps):
        next_ops = new_ops()
        side = [_pool_gate_proj(CHUNKS_PER_SWEEP * GDN_CHUNK, i, hn_ref, wuz_ref, extp_ref, z_ref)]
        if 2 * i >= len(sweeps):
            side.append(itertools.islice(pool, len(POOL_WINDOWS) // (len(sweeps) - len(sweeps) // 2)))
        if i + 1 < len(sweeps):
            side.append(prep(sweeps[i + 1], next_ops))
        if i > 0:
            side.append(phase2(sweeps[i - 1]))
        _interleave(_gdn_stages(ops, u_ref, wq_ref, ak_ref), _round_robin(side))
        ops = next_ops
    for _ in itertools.chain(pool, phase2(sweeps[-1])):
        pass


def _gdn(q3, k3, v3, grow3, gcol3, hn3, w_uz, w_pool, pool_scale, o_norm_g):
    b, s, _ = q3.shape
    tiles_per_batch = s // TM_GDN
    units = (TM_GDN // GDN_CHUNK) * DN_HEADS
    tile = lambda width: pl.BlockSpec((None, TM_GDN, width), lambda bi, si: (bi, si, 0))
    return pl.pallas_call(
        _gdn_kernel,
        grid=(b, tiles_per_batch),
        in_specs=[tile(DN_WIDTH), tile(DN_WIDTH), tile(DN_WIDTH),
                  pl.BlockSpec((None, 8, TM_GDN), lambda bi, si: (bi * tiles_per_batch + si, 0, 0)),
                  tile(LANES), tile(D_MODEL),
                  _const_spec((D_MODEL, POOL_WIDTH + DN_WIDTH)),
                  _const_spec((len(POOL_WINDOWS), POOL_GROUP_DIM, POOL_GROUP_DIM)),
                  _const_spec((1, POOL_WIDTH)),
                  _const_spec((1, DN_HEAD_DIM))],
        out_specs=tile(D_MODEL),
        out_shape=jax.ShapeDtypeStruct((b, s, D_MODEL), BF16),
        scratch_shapes=[
            pltpu.VMEM((DN_HEADS, DN_HEAD_DIM, DN_HEAD_DIM), F32),
            pltpu.VMEM((units, GDN_CHUNK, DN_HEAD_DIM), F32),
            pltpu.VMEM((units, 2 * GDN_CHUNK, DN_HEAD_DIM), BF16),
            pltpu.VMEM((units, 2 * GDN_CHUNK, DN_HEAD_DIM), BF16),
            pltpu.VMEM((POOL_HALO + TM_GDN, POOL_WIDTH), F32),
            pltpu.VMEM((TM_GDN, DN_WIDTH), F32),
        ],
        compiler_params=pltpu.CompilerParams(
            dimension_semantics=("arbitrary", "arbitrary"), vmem_limit_bytes=VMEM_LIMIT),
        name="gdn",
    )(q3, k3, v3, grow3, gcol3, hn3, w_uz, w_pool, pool_scale, o_norm_g)


def _mem_kv_kernel(m_ref, g_ref, wk_ref, wv_ref, kt_ref, v_ref):
    mn = _rms(m_ref[...], g_ref[...]).astype(BF16)
    kt_ref[...] = _dot(mn, wk_ref[...]).T.astype(kt_ref.dtype)
    v_ref[...] = _dot(mn, wv_ref[...]).astype(v_ref.dtype)


def _mem_kv(mem, g, w_k, w_v):
    b, mem_len, _ = mem.shape
    return pl.pallas_call(
        _mem_kv_kernel,
        grid=(b,),
        in_specs=[
            pl.BlockSpec((None, mem_len, D_MODEL), lambda i: (i, 0, 0)),
            _const_spec((1, D_MODEL)),
            _const_spec((D_MODEL, D_MODEL)),
            _const_spec((D_MODEL, D_MODEL)),
        ],
        out_specs=[
            pl.BlockSpec((None, D_MODEL, mem_len), lambda i: (i, 0, 0)),
            pl.BlockSpec((None, mem_len, D_MODEL), lambda i: (i, 0, 0)),
        ],
        out_shape=[
            jax.ShapeDtypeStruct((b, D_MODEL, mem_len), BF16),
            jax.ShapeDtypeStruct((b, mem_len, D_MODEL), BF16),
        ],
        compiler_params=pltpu.CompilerParams(
            dimension_semantics=("arbitrary",), vmem_limit_bytes=VMEM_LIMIT),
        name="mem_kv",
    )(mem, g, w_k, w_v)


def _post_kernel(x_ref, y_ref, kt_ref, v_ref, wout_ref, gxa_ref, wq_ref, wo_ref, gmlp_ref,
                 wup_ref, wdn_ref, gfin_ref, out_ref):
    h1 = x_ref[...] + _dot(y_ref[...], wout_ref[...])
    q = _dot(_rms(h1, gxa_ref[...]).astype(BF16), wq_ref[...]).astype(BF16)
    head_lanes = [slice(hd * XA_HEAD_DIM, (hd + 1) * XA_HEAD_DIM) for hd in range(XA_HEADS)]
    scores = [_dot(q[:, hl], kt_ref[hl, :]) * (XA_HEAD_DIM ** -0.5) for hl in head_lanes]
    expd = [jnp.exp(s - jnp.max(s, axis=-1, keepdims=True)) for s in scores]
    pv = [_dot(e.astype(BF16), v_ref[:, hl]) for e, hl in zip(expd, head_lanes)]
    att = jnp.concatenate([(o / jnp.sum(e, axis=-1, keepdims=True)).astype(BF16) for e, o in zip(expd, pv)],
                          axis=-1)
    h2 = h1 + _dot(att, wo_ref[...])
    hn2 = _rms(h2, gmlp_ref[...]).astype(BF16)
    acc = h2
    for j in range(D_FF // FF_BLOCK):
        a = jnp.maximum(_dot(hn2, wup_ref[:, j * FF_BLOCK:(j + 1) * FF_BLOCK]), 0.0)
        acc = acc + _dot((a * a).astype(BF16), wdn_ref[j * FF_BLOCK:(j + 1) * FF_BLOCK, :])
    out_ref[...] = _rms(acc, gfin_ref[...])


def _post(x2, y2, kt3, v3, w_out, g_xa, w_q, w_o, g_mlp, w_up, w_dn, g_fin, seq):
    t = x2.shape[0]
    mem_len = v3.shape[1]
    tiles_per_batch = seq // TM
    tile = pl.BlockSpec((TM, D_MODEL), lambda i: (i, 0))
    return pl.pallas_call(
        _post_kernel,
        grid=(t // TM,),
        in_specs=[
            tile, tile,
            pl.BlockSpec((None, D_MODEL, mem_len), lambda i: (i // tiles_per_batch, 0, 0)),
            pl.BlockSpec((None, mem_len, D_MODEL), lambda i: (i // tiles_per_batch, 0, 0)),
            _const_spec((D_MODEL, D_MODEL)),
            _const_spec((1, D_MODEL)),
            _const_spec((D_MODEL, D_MODEL)),
            _const_spec((D_MODEL, D_MODEL)),
            _const_spec((1, D_MODEL)),
            _const_spec((D_MODEL, D_FF)),
            _const_spec((D_FF, D_MODEL)),
            _const_spec((1, D_MODEL)),
        ],
        out_specs=tile,
        out_shape=jax.ShapeDtypeStruct((t, D_MODEL), F32),
        compiler_params=pltpu.CompilerParams(
            dimension_semantics=("arbitrary",), vmem_limit_bytes=VMEM_LIMIT),
        name="post",
    )(x2, y2, kt3, v3, w_out, g_xa, w_q, w_o, g_mlp, w_up, w_dn, g_fin)


def kernel(x, mem, norm_mix_g, w_in, w_pool, pool_scale, conv_w, a_log, dt_bias, dn_out_norm_g, w_out,
           norm_xattn_g, mem_norm_g, w_xq, w_xk, w_xv, w_xo, norm_mlp_g, w_up, w_down, final_norm_g):
    b, s, d = x.shape
    assert w_in.shape[0] == 1, "single-layer problem: the final RMSNorm is fused into the layer's post stage"
    row = lambda v: v.reshape(1, -1).astype(F32)

    w_qkv = w_in[0][:, QKV_OFF:QKV_OFF + 3 * DN_WIDTH].astype(BF16)
    w_uz = jnp.concatenate([w_in[0][:, :POOL_WIDTH], w_in[0][:, Z_OFF:Z_OFF + DN_WIDTH]], axis=1).astype(BF16)
    w_ba = jnp.pad(w_in[0][:, MAIN_COLS:], ((0, 0), (0, LANES - 2 * DN_HEADS))).astype(BF16)
    gate_params = jnp.zeros((8, LANES), F32)
    gate_params = gate_params.at[DN_HEADS:, 0].set(a_log[0]).at[DN_HEADS:, 1].set(dt_bias[0])

    x2 = x.reshape(b * s, d)
    hn, q, k, v, grow, gcol = _in_proj(x2, row(norm_mix_g[0]), w_qkv, w_ba, conv_w[0].astype(F32), gate_params, s)
    per_batch = lambda a: a.reshape(b, s, a.shape[-1])
    y = _gdn(per_batch(q), per_batch(k), per_batch(v), grow, per_batch(gcol), per_batch(hn), w_uz,
             w_pool[0].astype(BF16), row(pool_scale[0]), row(dn_out_norm_g[0]))
    kt3, v3 = _mem_kv(mem, row(mem_norm_g[0]), w_xk[0].astype(BF16), w_xv[0].astype(BF16))
    out = _post(x2, y.reshape(b * s, d), kt3, v3, w_out[0].astype(BF16), row(norm_xattn_g[0]),
                w_xq[0].astype(BF16), w_xo[0].astype(BF16), row(norm_mlp_g[0]), w_up[0].astype(BF16),
                w_down[0].astype(BF16), row(final_norm_g), s)
    return out.reshape(b, s, d)
```

```python
import functools
import itertools

import jax
import jax.numpy as jnp
from jax import lax
from jax.experimental import pallas as pl
from jax.experimental.pallas import tpu as pltpu

D_MODEL = 1024
POOL_WIDTH = 512
POOL_WINDOWS = (2, 4, 8, 16)
POOL_GROUP_DIM = 128
DN_WIDTH = 512
DN_HEADS = 4
DN_HEAD_DIM = 128
CONV_K = 4
MAIN_COLS = POOL_WIDTH + 4 * DN_WIDTH
QKV_OFF = POOL_WIDTH
Z_OFF = POOL_WIDTH + 3 * DN_WIDTH
XA_HEADS = 4
XA_HEAD_DIM = 256
D_FF = 4096
EPS = 1e-6

LANES = 128
GDN_CHUNK = 128
POOL_HALO = 16
CONV_HALO = 8
TM = 512
TM_GDN = 1024
CHUNKS_PER_SWEEP = 2
FF_BLOCK = 1024
PREP_SEGMENTS_AFTER_IN_PROJ_SEGMENT = (0, 0, 5, 8)
VMEM_LIMIT = 56 * 1024 * 1024

F32 = jnp.float32
BF16 = jnp.bfloat16


def _dot(a, b):
    return jnp.dot(a, b, preferred_element_type=F32)


def _rms(x, g):
    return x * lax.rsqrt(jnp.mean(x * x, axis=-1, keepdims=True) + EPS) * g


def _silu(x):
    return x * jax.nn.sigmoid(x)


def _const_spec(shape):
    zeros = (0,) * len(shape)
    return pl.BlockSpec(shape, lambda *_: zeros, pipeline_mode=pl.Buffered(1))


def _round_robin(generators):
    live = list(generators)
    while live:
        for g in list(live):
            try:
                yield next(g)
            except StopIteration:
                live.remove(g)


def _interleave(first, second, second_per_first=1):
    for _ in first:
        for _ in range(second_per_first):
            next(second, None)
    for _ in second:
        pass


def _in_proj_matmuls(x_ref, g_ref, wqkv_ref, wba_ref, hn_ref, qkv_ref, ba_ref):
    hn_ref[...] = _rms(x_ref[...], g_ref[...]).astype(BF16)
    ba_ref[...] = _dot(hn_ref[...], wba_ref[...])
    yield
    for part in range(3):
        lanes = slice(part * DN_WIDTH, (part + 1) * DN_WIDTH)
        qkv_ref[CONV_HALO:CONV_HALO + TM, lanes] = _dot(hn_ref[...], wqkv_ref[:, lanes])
        yield


def _in_proj_prep(qkv_ref, ba_ref, convw_ref, gp_ref, q_ref, k_ref, v_ref, grow_ref, gcol_ref):
    c = GDN_CHUNK

    bat = ba_ref[...].T[0:8, :]
    gp = gp_ref[...]
    a_log = gp[:, 0:1]
    dt_bias = gp[:, 1:2]
    xa = bat + dt_bias
    softplus = jnp.maximum(xa, 0.0) + jnp.log1p(jnp.exp(-jnp.abs(xa)))
    cs = -jnp.exp(a_log) * softplus
    lane = lax.broadcasted_iota(jnp.int32, (8, TM), 1) & (c - 1)
    sh = 1
    while sh < c:
        cs = cs + jnp.where(lane >= sh, pltpu.roll(cs, sh, axis=1), 0.0)
        sh *= 2
    row8 = lax.broadcasted_iota(jnp.int32, (8, TM), 0)
    stack = jnp.where(row8 < DN_HEADS, jax.nn.sigmoid(bat), cs)
    grow_ref[...] = stack
    gcol_ref[...] = jnp.concatenate([stack, jnp.zeros((LANES - 8, TM), F32)], axis=0).T
    yield

    for part, dst in enumerate((q_ref, k_ref, v_ref)):
        for r0 in range(0, TM, c):
            for h in range(DN_HEADS):
                lo = h * DN_HEAD_DIM
                lanes = slice(part * DN_WIDTH + lo, part * DN_WIDTH + lo + DN_HEAD_DIM)
                xe = qkv_ref[r0:r0 + CONV_HALO + c, lanes]
                xc = convw_ref[CONV_K - 1:CONV_K, lanes] * xe
                for d in range(1, CONV_K):
                    xc = xc + convw_ref[CONV_K - 1 - d:CONV_K - d, lanes] * pltpu.roll(xe, d, axis=0)
                dst[r0:r0 + c, lo:lo + DN_HEAD_DIM] = xc[CONV_HALO:, :]
            yield


def _in_proj_kernel(tiles_per_batch, x_ref, g_ref, wqkv_ref, wba_ref, convw_ref, gp_ref,
                    hn_ref, q_ref, k_ref, v_ref, grow_ref, gcol_ref,
                    qkv_ref, ba_ref):
    first_of_batch = lax.rem(pl.program_id(0), jnp.int32(tiles_per_batch)) == 0

    @pl.when(first_of_batch)
    def _():
        qkv_ref[0:CONV_HALO, :] = jnp.zeros((CONV_HALO, 3 * DN_WIDTH), F32)

    @pl.when(jnp.logical_not(first_of_batch))
    def _():
        qkv_ref[0:CONV_HALO, :] = qkv_ref[TM:TM + CONV_HALO, :]

    matmuls = _in_proj_matmuls(x_ref, g_ref, wqkv_ref, wba_ref, hn_ref, qkv_ref, ba_ref)
    prep = _in_proj_prep(qkv_ref, ba_ref, convw_ref, gp_ref, q_ref, k_ref, v_ref, grow_ref, gcol_ref)
    for n_prep_segments in PREP_SEGMENTS_AFTER_IN_PROJ_SEGMENT:
        next(matmuls)
        for _ in range(n_prep_segments):
            next(prep)
    done = object()
    assert next(matmuls, done) is done and next(prep, done) is done


def _in_proj(x2, g, w_qkv, w_ba, conv_w, gate_params, seq):
    t = x2.shape[0]
    n_tiles = t // TM
    tiles_per_gdn_tile = TM_GDN // TM
    tile = lambda width: pl.BlockSpec((TM, width), lambda i: (i, 0))
    wide = jax.ShapeDtypeStruct((t, DN_WIDTH), F32)
    return pl.pallas_call(
        functools.partial(_in_proj_kernel, seq // TM),
        grid=(n_tiles,),
        in_specs=[
            tile(D_MODEL),
            _const_spec((1, D_MODEL)),
            _const_spec((D_MODEL, 3 * DN_WIDTH)),
            _const_spec((D_MODEL, LANES)),
            _const_spec((CONV_K, 3 * DN_WIDTH)),
            _const_spec((8, LANES)),
        ],
        out_specs=[
            tile(D_MODEL), tile(DN_WIDTH), tile(DN_WIDTH), tile(DN_WIDTH),
            pl.BlockSpec((None, 8, TM), lambda i: (i // tiles_per_gdn_tile, 0, i % tiles_per_gdn_tile)),
            tile(LANES),
        ],
        out_shape=[
            jax.ShapeDtypeStruct((t, D_MODEL), BF16), wide, wide, wide,
            jax.ShapeDtypeStruct((t // TM_GDN, 8, TM_GDN), F32),
            jax.ShapeDtypeStruct((t, LANES), F32),
        ],
        scratch_shapes=[
            pltpu.VMEM((CONV_HALO + TM, 3 * DN_WIDTH), F32),
            pltpu.VMEM((TM, LANES), F32),
        ],
        compiler_params=pltpu.CompilerParams(
            dimension_semantics=("arbitrary",), vmem_limit_bytes=VMEM_LIMIT),
        name="in_proj",
    )(x2, g, w_qkv, w_ba, conv_w, gate_params)


def _gdn_prep(chunks, q_ref, k_ref, v_ref, grow_ref, gcol_ref, ops):
    c = GDN_CHUNK
    ii = lax.broadcasted_iota(jnp.int32, (c, c), 0)
    jj = lax.broadcasted_iota(jnp.int32, (c, c), 1)
    tril = ii >= jj
    strict = ii > jj

    def col(r0, j):
        return jnp.broadcast_to(gcol_ref[r0:r0 + c, j:j + 1], (c, c))

    for ci in chunks:
        for h in range(DN_HEADS):
            r0 = ci * c
            lo = h * DN_HEAD_DIM
            qh = _silu(q_ref[r0:r0 + c, lo:lo + DN_HEAD_DIM])
            kh = _silu(k_ref[r0:r0 + c, lo:lo + DN_HEAD_DIM])
            vh = _silu(v_ref[r0:r0 + c, lo:lo + DN_HEAD_DIM])
            qn = qh * lax.rsqrt(jnp.sum(qh * qh, axis=-1, keepdims=True) + EPS) * (DN_HEAD_DIM ** -0.5)
            kn = kh * lax.rsqrt(jnp.sum(kh * kh, axis=-1, keepdims=True) + EPS)
            beta_b = col(r0, h)
            gc_b = col(r0, DN_HEADS + h)
            gc_r = grow_ref[DN_HEADS + h:DN_HEADS + h + 1, r0:r0 + c]
            g_last = gcol_ref[r0 + c - 1:r0 + c, DN_HEADS + h:DN_HEADS + h + 1]
            d = jnp.where(tril, jnp.exp(gc_b - gc_r), 0.0)
            eg = jnp.exp(gc_b)
            kb = kn * beta_b
            kt = kn.T
            ops["unit"].append((ci, h))
            ops["dec"].append(d)
            ops["neg_d"].append(jnp.where(strict, -d, 0.0))
            ops["kt16"].append(kt.astype(BF16))
            ops["lhs_kq"].append(jnp.concatenate([kb, qn], axis=0).astype(BF16))
            ops["vbkg"].append(jnp.concatenate([vh * beta_b, kb * eg], axis=1).astype(BF16))
            ops["qg16"].append((qn * eg).astype(BF16))
            ops["kdt16"].append((kt * jnp.exp(g_last - gc_r)).astype(BF16))
            if h % 2 == 1:
                yield


def _gdn_stages(ops, u_ref, wq_ref, ak_ref):
    c = GDN_CHUNK
    ii = lax.broadcasted_iota(jnp.int32, (c, c), 0)
    jj = lax.broadcasted_iota(jnp.int32, (c, c), 1)
    eye = jnp.where(ii == jj, 1.0, 0.0).astype(F32)
    n = len(ops["unit"])
    kq = [_dot(ops["lhs_kq"][i], ops["kt16"][i]) for i in range(n)]
    yield
    xs = [kq[i][:c] * ops["neg_d"][i] for i in range(n)]
    ps = [eye + xs[i] for i in range(n)]
    att16 = [(kq[i][c:] * ops["dec"][i]).astype(BF16) for i in range(n)]
    x16 = [x.astype(BF16) for x in xs]
    xs = [_dot(x, x) for x in x16]
    yield
    span = 2
    while 2 * span < c:
        x16 = [x.astype(BF16) for x in xs]
        both = [_dot(jnp.concatenate([ps[i].astype(BF16), x16[i]], axis=0), x16[i]) for i in range(n)]
        ps = [ps[i] + both[i][:c] for i in range(n)]
        xs = [both[i][c:] for i in range(n)]
        span *= 2
        yield
    ps = [ps[i] + _dot(ps[i].astype(BF16), xs[i].astype(BF16)) for i in range(n)]
    yield
    uw = [_dot(ps[i].astype(BF16), ops["vbkg"][i]) for i in range(n)]
    for i, (ci, h) in enumerate(ops["unit"]):
        idx = ci * DN_HEADS + h
        u_ref[idx] = uw[i][:, :DN_HEAD_DIM]
        wq_ref[idx] = jnp.concatenate([uw[i][:, DN_HEAD_DIM:].astype(BF16), ops["qg16"][i]], axis=0)
        ak_ref[idx] = jnp.concatenate([att16[i], ops["kdt16"][i]], axis=0)
    yield


def _pool_gate_proj(sweep_rows, sweep, hn_ref, wuz_ref, extp_ref, z_ref):
    half = POOL_WIDTH // 2
    n_sweeps = TM_GDN // sweep_rows
    row_blocks = lambda rb: slice(rb * sweep_rows, (rb + 1) * sweep_rows)
    for lo in (0, half):
        rows = row_blocks(sweep)
        z_ref[rows, lo:lo + half] = _dot(hn_ref[rows, :], wuz_ref[:, POOL_WIDTH + lo:POOL_WIDTH + lo + half])
        yield
    if 2 * sweep < n_sweeps:
        for rb in (2 * sweep, 2 * sweep + 1):
            for lo in (0, half):
                rows = row_blocks(rb)
                extp_ref[POOL_HALO + rb * sweep_rows:POOL_HALO + (rb + 1) * sweep_rows, lo:lo + half] = _dot(
                    hn_ref[rows, :], wuz_ref[:, lo:lo + half])
                yield


def _pool_mixer(t0, wpool_ref, pscale_ref, y_ref, extp_ref):
    tm = y_ref.shape[0]
    t_glob = lax.broadcasted_iota(jnp.int32, (tm, POOL_GROUP_DIM), 0) + t0
    for gi, win in enumerate(POOL_WINDOWS):
        lo = gi * POOL_GROUP_DIM
        assert win & (win - 1) == 0 and win <= POOL_HALO, "doubling needs power-of-two windows within the halo"
        acc = extp_ref[:, lo:lo + POOL_GROUP_DIM]
        shift = 1
        while shift < win:
            acc = acc + pltpu.roll(acc, shift, axis=0)
            shift *= 2
        acc = acc[POOL_HALO:, :]
        u = extp_ref[POOL_HALO:POOL_HALO + tm, lo:lo + POOL_GROUP_DIM]
        cnt = jnp.minimum(t_glob + 1, win).astype(F32)
        pooled = acc / cnt - u
        mixed = _dot(pooled.astype(BF16), wpool_ref[gi])
        y_ref[:, lo:lo + POOL_GROUP_DIM] = (mixed * pscale_ref[:, lo:lo + POOL_GROUP_DIM]).astype(y_ref.dtype)
        yield
    extp_ref[0:POOL_HALO, :] = extp_ref[tm:tm + POOL_HALO, :]


def _gdn_phase2(chunks, gcol_ref, z_ref, og_ref, state_ref, u_ref, wq_ref, ak_ref, y_ref):
    c = GDN_CHUNK
    heads = range(DN_HEADS)
    for ci in chunks:
        r0 = ci * c
        st = [state_ref[h] for h in heads]
        st16 = [s.astype(BF16) for s in st]
        r1 = [_dot(wq_ref[ci * DN_HEADS + h], st16[h]) for h in heads]
        yield
        vn16 = [(u_ref[ci * DN_HEADS + h] - r1[h][:c]).astype(BF16) for h in heads]
        r2 = [_dot(ak_ref[ci * DN_HEADS + h], vn16[h]) for h in heads]
        yield
        for h in heads:
            lo = h * DN_HEAD_DIM
            g_last = gcol_ref[r0 + c - 1:r0 + c, DN_HEADS + h:DN_HEADS + h + 1]
            state_ref[h] = st[h] * jnp.exp(g_last) + r2[h][c:]
            o = r1[h][c:] + r2[h][:c]
            on = o * lax.rsqrt(jnp.mean(o * o, axis=-1, keepdims=True) + EPS) * og_ref[...]
            y_ref[r0:r0 + c, POOL_WIDTH + lo:POOL_WIDTH + lo + DN_HEAD_DIM] = (
                on * _silu(z_ref[r0:r0 + c, lo:lo + DN_HEAD_DIM])).astype(y_ref.dtype)


def _gdn_kernel(q_ref, k_ref, v_ref, grow_ref, gcol_ref, hn_ref, wuz_ref, wpool_ref, pscale_ref, og_ref, y_ref,
                state_ref, u_ref, wq_ref, ak_ref, extp_ref, z_ref):
    @pl.when(pl.program_id(1) == 0)
    def _():
        state_ref[...] = jnp.zeros_like(state_ref)
        extp_ref[0:POOL_HALO, :] = jnp.zeros((POOL_HALO, POOL_WIDTH), F32)

    n_chunks = TM_GDN // GDN_CHUNK
    sweeps = [range(g0, g0 + CHUNKS_PER_SWEEP) for g0 in range(0, n_chunks, CHUNKS_PER_SWEEP)]
    new_ops = lambda: {name: [] for name in ("unit", "dec", "neg_d", "kt16", "lhs_kq", "vbkg", "qg16", "kdt16")}
    prep = lambda chunks, ops: _gdn_prep(chunks, q_ref, k_ref, v_ref, grow_ref, gcol_ref, ops)
    phase2 = lambda chunks: _gdn_phase2(chunks, gcol_ref, z_ref, og_ref, state_ref, u_ref, wq_ref, ak_ref, y_ref)
    pool = _pool_mixer(pl.program_id(1) * TM_GDN, wpool_ref, pscale_ref, y_ref, extp_ref)
    ops = new_ops()
    for _ in prep(sweeps[0], ops):
        pass
    for i, sweep in enumerate(sweeps):
        next_ops = new_ops()
        side = [_pool_gate_proj(CHUNKS_PER_SWEEP * GDN_CHUNK, i, hn_ref, wuz_ref, extp_ref, z_ref)]
        if 2 * i >= len(sweeps):
            side.append(itertools.islice(pool, len(POOL_WINDOWS) // (len(sweeps) - len(sweeps) // 2)))
        if i + 1 < len(sweeps):
            side.append(prep(sweeps[i + 1], next_ops))
        if i > 0:
            side.append(phase2(sweeps[i - 1]))
        _interleave(_gdn_stages(ops, u_ref, wq_ref, ak_ref), _round_robin(side))
        ops = next_ops
    for _ in itertools.chain(pool, phase2(sweeps[-1])):
        pass


def _gdn(q3, k3, v3, grow3, gcol3, hn3, w_uz, w_pool, pool_scale, o_norm_g):
    b, s, _ = q3.shape
    tiles_per_batch = s // TM_GDN
    units = (TM_GDN // GDN_CHUNK) * DN_HEADS
    tile = lambda width: pl.BlockSpec((None, TM_GDN, width), lambda bi, si: (bi, si, 0))
    return pl.pallas_call(
        _gdn_kernel,
        grid=(b, tiles_per_batch),
        in_specs=[tile(DN_WIDTH), tile(DN_WIDTH), tile(DN_WIDTH),
                  pl.BlockSpec((None, 8, TM_GDN), lambda bi, si: (bi * tiles_per_batch + si, 0, 0)),
                  tile(LANES), tile(D_MODEL),
                  _const_spec((D_MODEL, POOL_WIDTH + DN_WIDTH)),
                  _const_spec((len(POOL_WINDOWS), POOL_GROUP_DIM, POOL_GROUP_DIM)),
                  _const_spec((1, POOL_WIDTH)),
                  _const_spec((1, DN_HEAD_DIM))],
        out_specs=tile(D_MODEL),
        out_shape=jax.ShapeDtypeStruct((b, s, D_MODEL), BF16),
        scratch_shapes=[
            pltpu.VMEM((DN_HEADS, DN_HEAD_DIM, DN_HEAD_DIM), F32),
            pltpu.VMEM((units, GDN_CHUNK, DN_HEAD_DIM), F32),
            pltpu.VMEM((units, 2 * GDN_CHUNK, DN_HEAD_DIM), BF16),
            pltpu.VMEM((units, 2 * GDN_CHUNK, DN_HEAD_DIM), BF16),
            pltpu.VMEM((POOL_HALO + TM_GDN, POOL_WIDTH), F32),
            pltpu.VMEM((TM_GDN, DN_WIDTH), F32),
        ],
        compiler_params=pltpu.CompilerParams(
            dimension_semantics=("arbitrary", "arbitrary"), vmem_limit_bytes=VMEM_LIMIT),
        name="gdn",
    )(q3, k3, v3, grow3, gcol3, hn3, w_uz, w_pool, pool_scale, o_norm_g)


def _mem_kv_kernel(m_ref, g_ref, wk_ref, wv_ref, kt_ref, v_ref):
    mn = _rms(m_ref[...], g_ref[...]).astype(BF16)
    kt_ref[...] = _dot(mn, wk_ref[...]).T.astype(kt_ref.dtype)
    v_ref[...] = _dot(mn, wv_ref[...]).astype(v_ref.dtype)


def _mem_kv(mem, g, w_k, w_v):
    b, mem_len, _ = mem.shape
    return pl.pallas_call(
        _mem_kv_kernel,
        grid=(b,),
        in_specs=[
            pl.BlockSpec((None, mem_len, D_MODEL), lambda i: (i, 0, 0)),
            _const_spec((1, D_MODEL)),
            _const_spec((D_MODEL, D_MODEL)),
            _const_spec((D_MODEL, D_MODEL)),
        ],
        out_specs=[
            pl.BlockSpec((None, D_MODEL, mem_len), lambda i: (i, 0, 0)),
            pl.BlockSpec((None, mem_len, D_MODEL), lambda i: (i, 0, 0)),
        ],
        out_shape=[
            jax.ShapeDtypeStruct((b, D_MODEL, mem_len), BF16),
            jax.ShapeDtypeStruct((b, mem_len, D_MODEL), BF16),
        ],
        compiler_params=pltpu.CompilerParams(
            dimension_semantics=("arbitrary",), vmem_limit_bytes=VMEM_LIMIT),
        name="mem_kv",
    )(mem, g, w_k, w_v)


def _post_kernel(x_ref, y_ref, kt_ref, v_ref, wout_ref, gxa_ref, wq_ref, wo_ref, gmlp_ref,
                 wup_ref, wdn_ref, gfin_ref, out_ref):
    h1 = x_ref[...] + _dot(y_ref[...], wout_ref[...])
    q = _dot(_rms(h1, gxa_ref[...]).astype(BF16), wq_ref[...]).astype(BF16)
    head_lanes = [slice(hd * XA_HEAD_DIM, (hd + 1) * XA_HEAD_DIM) for hd in range(XA_HEADS)]
    scores = [_dot(q[:, hl], kt_ref[hl, :]) * (XA_HEAD_DIM ** -0.5) for hl in head_lanes]
    expd = [jnp.exp(s - jnp.max(s, axis=-1, keepdims=True)) for s in scores]
    pv = [_dot(e.astype(BF16), v_ref[:, hl]) for e, hl in zip(expd, head_lanes)]
    att = jnp.concatenate([(o / jnp.sum(e, axis=-1, keepdims=True)).astype(BF16) for e, o in zip(expd, pv)],
                          axis=-1)
    h2 = h1 + _dot(att, wo_ref[...])
    hn2 = _rms(h2, gmlp_ref[...]).astype(BF16)
    acc = h2
    for j in range(D_FF // FF_BLOCK):
        a = jnp.maximum(_dot(hn2, wup_ref[:, j * FF_BLOCK:(j + 1) * FF_BLOCK]), 0.0)
        acc = acc + _dot((a * a).astype(BF16), wdn_ref[j * FF_BLOCK:(j + 1) * FF_BLOCK, :])
    out_ref[...] = _rms(acc, gfin_ref[...])


def _post(x2, y2, kt3, v3, w_out, g_xa, w_q, w_o, g_mlp, w_up, w_dn, g_fin, seq):
    t = x2.shape[0]
    mem_len = v3.shape[1]
    tiles_per_batch = seq // TM
    tile = pl.BlockSpec((TM, D_MODEL), lambda i: (i, 0))
    return pl.pallas_call(
        _post_kernel,
        grid=(t // TM,),
        in_specs=[
            tile, tile,
            pl.BlockSpec((None, D_MODEL, mem_len), lambda i: (i // tiles_per_batch, 0, 0)),
            pl.BlockSpec((None, mem_len, D_MODEL), lambda i: (i // tiles_per_batch, 0, 0)),
            _const_spec((D_MODEL, D_MODEL)),
            _const_spec((1, D_MODEL)),
            _const_spec((D_MODEL, D_MODEL)),
            _const_spec((D_MODEL, D_MODEL)),
            _const_spec((1, D_MODEL)),
            _const_spec((D_MODEL, D_FF)),
            _const_spec((D_FF, D_MODEL)),
            _const_spec((1, D_MODEL)),
        ],
        out_specs=tile,
        out_shape=jax.ShapeDtypeStruct((t, D_MODEL), F32),
        compiler_params=pltpu.CompilerParams(
            dimension_semantics=("arbitrary",), vmem_limit_bytes=VMEM_LIMIT),
        name="post",
    )(x2, y2, kt3, v3, w_out, g_xa, w_q, w_o, g_mlp, w_up, w_dn, g_fin)


def kernel(x, mem, norm_mix_g, w_in, w_pool, pool_scale, conv_w, a_log, dt_bias, dn_out_norm_g, w_out,
           norm_xattn_g, mem_norm_g, w_xq, w_xk, w_xv, w_xo, norm_mlp_g, w_up, w_down, final_norm_g):
    b, s, d = x.shape
    assert w_in.shape[0] == 1, "single-layer problem: the final RMSNorm is fused into the layer's post stage"
    row = lambda v: v.reshape(1, -1).astype(F32)

    w_qkv = w_in[0][:, QKV_OFF:QKV_OFF + 3 * DN_WIDTH].astype(BF16)
    w_uz = jnp.concatenate([w_in[0][:, :POOL_WIDTH], w_in[0][:, Z_OFF:Z_OFF + DN_WIDTH]], axis=1).astype(BF16)
    w_ba = jnp.pad(w_in[0][:, MAIN_COLS:], ((0, 0), (0, LANES - 2 * DN_HEADS))).astype(BF16)
    gate_params = jnp.zeros((8, LANES), F32)
    gate_params = gate_params.at[DN_HEADS:, 0].set(a_log[0]).at[DN_HEADS:, 1].set(dt_bias[0])

    x2 = x.reshape(b * s, d)
    hn, q, k, v, grow, gcol = _in_proj(x2, row(norm_mix_g[0]), w_qkv, w_ba, conv_w[0].astype(F32), gate_params, s)
    per_batch = lambda a: a.reshape(b, s, a.shape[-1])
    y = _gdn(per_batch(q), per_batch(k), per_batch(v), grow, per_batch(gcol), per_batch(hn), w_uz,
             w_pool[0].astype(BF16), row(pool_scale[0]), row(dn_out_norm_g[0]))
    kt3, v3 = _mem_kv(mem, row(mem_norm_g[0]), w_xk[0].astype(BF16), w_xv[0].astype(BF16))
    out = _post(x2, y.reshape(b * s, d), kt3, v3, w_out[0].astype(BF16), row(norm_xattn_g[0]),
                w_xq[0].astype(BF16), w_xo[0].astype(BF16), row(norm_mlp_g[0]), w_up[0].astype(BF16),
                w_down[0].astype(BF16), row(final_norm_g), s)
    return out.reshape(b, s, d)
```

```python
import functools
import itertools

import jax
import jax.numpy as jnp
from jax import lax
from jax.experimental import pallas as pl
from jax.experimental.pallas import tpu as pltpu

D_MODEL = 1024
POOL_WIDTH = 512
POOL_WINDOWS = (2, 4, 8, 16)
POOL_GROUP_DIM = 128
DN_WIDTH = 512
DN_HEADS = 4
DN_HEAD_DIM = 128
CONV_K = 4
MAIN_COLS = POOL_WIDTH + 4 * DN_WIDTH
QKV_OFF = POOL_WIDTH
Z_OFF = POOL_WIDTH + 3 * DN_WIDTH
XA_HEADS = 4
XA_HEAD_DIM = 256
D_FF = 4096
EPS = 1e-6

LANES = 128
GDN_CHUNK = 128
POOL_HALO = 16
CONV_HALO = 8
TM = 512
TM_GDN = 1024
CHUNKS_PER_SWEEP = 2
FF_BLOCK = 1024
PREP_SEGMENTS_AFTER_IN_PROJ_SEGMENT = (0, 0, 5, 4, 4, 4)
VMEM_LIMIT = 56 * 1024 * 1024

F32 = jnp.float32
BF16 = jnp.bfloat16


def _dot(a, b):
    return jnp.dot(a, b, preferred_element_type=F32)


def _rms(x, g):
    return x * lax.rsqrt(jnp.mean(x * x, axis=-1, keepdims=True) + EPS) * g


def _silu(x):
    return x * jax.nn.sigmoid(x)


def _const_spec(shape):
    zeros = (0,) * len(shape)
    return pl.BlockSpec(shape, lambda *_: zeros, pipeline_mode=pl.Buffered(1))


def _round_robin(generators):
    live = list(generators)
    while live:
        for g in list(live):
            try:
                yield next(g)
            except StopIteration:
                live.remove(g)


def _interleave(first, second, second_per_first=1):
    for _ in first:
        for _ in range(second_per_first):
            next(second, None)
    for _ in second:
        pass


def _pool_mixer(t0, wpool_ref, pscale_ref, y_ref, extp_ref):
    tm = y_ref.shape[0]
    t_glob = lax.broadcasted_iota(jnp.int32, (tm, POOL_GROUP_DIM), 0) + t0
    for gi, win in enumerate(POOL_WINDOWS):
        lo = gi * POOL_GROUP_DIM
        assert win & (win - 1) == 0 and win <= POOL_HALO, "doubling needs power-of-two windows within the halo"
        acc = extp_ref[:, lo:lo + POOL_GROUP_DIM]
        shift = 1
        while shift < win:
            acc = acc + pltpu.roll(acc, shift, axis=0)
            shift *= 2
        acc = acc[POOL_HALO:, :]
        u = extp_ref[POOL_HALO:POOL_HALO + tm, lo:lo + POOL_GROUP_DIM]
        cnt = jnp.minimum(t_glob + 1, win).astype(F32)
        pooled = acc / cnt - u
        mixed = _dot(pooled.astype(BF16), wpool_ref[gi])
        y_ref[:, lo:lo + POOL_GROUP_DIM] = (mixed * pscale_ref[:, lo:lo + POOL_GROUP_DIM]).astype(y_ref.dtype)
        yield
    extp_ref[0:POOL_HALO, :] = extp_ref[tm:tm + POOL_HALO, :]


def _in_proj_matmuls(x_ref, g_ref, wqkv_ref, wba_ref, wuz_ref, hn_ref, qkv_ref, ba_ref, extp_ref, z_ref):
    hn_ref[...] = _rms(x_ref[...], g_ref[...]).astype(BF16)
    ba_ref[...] = _dot(hn_ref[...], wba_ref[...])
    yield
    for part in range(3):
        lanes = slice(part * DN_WIDTH, (part + 1) * DN_WIDTH)
        qkv_ref[CONV_HALO:CONV_HALO + TM, lanes] = _dot(hn_ref[...], wqkv_ref[:, lanes])
        yield
    extp_ref[POOL_HALO:, :] = _dot(hn_ref[...], wuz_ref[:, 0:POOL_WIDTH])
    yield
    z_ref[...] = _dot(hn_ref[...], wuz_ref[:, POOL_WIDTH:POOL_WIDTH + DN_WIDTH])
    yield


def _in_proj_prep(qkv_ref, ba_ref, z_ref, convw_ref, gp_ref, q_ref, k_ref, v_ref, grow_ref, gcol_ref, sz_ref, pool):
    c = GDN_CHUNK

    bat = ba_ref[...].T[0:8, :]
    gp = gp_ref[...]
    a_log = gp[:, 0:1]
    dt_bias = gp[:, 1:2]
    xa = bat + dt_bias
    softplus = jnp.maximum(xa, 0.0) + jnp.log1p(jnp.exp(-jnp.abs(xa)))
    cs = -jnp.exp(a_log) * softplus
    lane = lax.broadcasted_iota(jnp.int32, (8, TM), 1) & (c - 1)
    sh = 1
    while sh < c:
        cs = cs + jnp.where(lane >= sh, pltpu.roll(cs, sh, axis=1), 0.0)
        sh *= 2
    row8 = lax.broadcasted_iota(jnp.int32, (8, TM), 0)
    stack = jnp.where(row8 < DN_HEADS, jax.nn.sigmoid(bat), cs)
    grow_ref[...] = stack
    gcol_ref[...] = jnp.concatenate([stack, jnp.zeros((LANES - 8, TM), F32)], axis=0).T
    yield

    for part, dst in enumerate((q_ref, k_ref, v_ref)):
        for r0 in range(0, TM, c):
            for h in range(DN_HEADS):
                lo = h * DN_HEAD_DIM
                lanes = slice(part * DN_WIDTH + lo, part * DN_WIDTH + lo + DN_HEAD_DIM)
                xe = qkv_ref[r0:r0 + CONV_HALO + c, lanes]
                xc = convw_ref[CONV_K - 1:CONV_K, lanes] * xe
                for d in range(1, CONV_K):
                    xc = xc + convw_ref[CONV_K - 1 - d:CONV_K - d, lanes] * pltpu.roll(xe, d, axis=0)
                dst[r0:r0 + c, lo:lo + DN_HEAD_DIM] = xc[CONV_HALO:, :]
            yield
    yield from pool
    for r0 in range(0, TM, c):
        sz_ref[r0:r0 + c, :] = _silu(z_ref[r0:r0 + c, :])
        yield


def _in_proj_kernel(tiles_per_batch, x_ref, g_ref, wqkv_ref, wba_ref, wuz_ref, convw_ref, gp_ref, wpool_ref, pscale_ref,
                    q_ref, k_ref, v_ref, grow_ref, gcol_ref, ypool_ref, sz_ref,
                    hn_ref, qkv_ref, ba_ref, extp_ref, z_ref):
    t_in_batch = lax.rem(pl.program_id(0), jnp.int32(tiles_per_batch))

    @pl.when(t_in_batch == 0)
    def _():
        qkv_ref[0:CONV_HALO, :] = jnp.zeros((CONV_HALO, 3 * DN_WIDTH), F32)
        extp_ref[0:POOL_HALO, :] = jnp.zeros((POOL_HALO, POOL_WIDTH), F32)

    @pl.when(t_in_batch != 0)
    def _():
        qkv_ref[0:CONV_HALO, :] = qkv_ref[TM:TM + CONV_HALO, :]

    matmuls = _in_proj_matmuls(x_ref, g_ref, wqkv_ref, wba_ref, wuz_ref, hn_ref, qkv_ref, ba_ref, extp_ref, z_ref)
    pool = _pool_mixer(t_in_batch * TM, wpool_ref, pscale_ref, ypool_ref, extp_ref)
    prep = _in_proj_prep(qkv_ref, ba_ref, z_ref, convw_ref, gp_ref, q_ref, k_ref, v_ref, grow_ref, gcol_ref,
                         sz_ref, pool)
    for n_prep_segments in PREP_SEGMENTS_AFTER_IN_PROJ_SEGMENT:
        next(matmuls)
        for _ in range(n_prep_segments):
            next(prep)
    done = object()
    assert next(matmuls, done) is done
    for _ in prep:
        pass


def _in_proj(x2, g, w_qkv, w_ba, w_uz, conv_w, gate_params, w_pool, pool_scale, seq):
    t = x2.shape[0]
    n_tiles = t // TM
    tiles_per_gdn_tile = TM_GDN // TM
    tile = lambda width: pl.BlockSpec((TM, width), lambda i: (i, 0))
    wide = jax.ShapeDtypeStruct((t, DN_WIDTH), F32)
    return pl.pallas_call(
        functools.partial(_in_proj_kernel, seq // TM),
        grid=(n_tiles,),
        in_specs=[
            tile(D_MODEL),
            _const_spec((1, D_MODEL)),
            _const_spec((D_MODEL, 3 * DN_WIDTH)),
            _const_spec((D_MODEL, LANES)),
            _const_spec((D_MODEL, POOL_WIDTH + DN_WIDTH)),
            _const_spec((CONV_K, 3 * DN_WIDTH)),
            _const_spec((8, LANES)),
            _const_spec((len(POOL_WINDOWS), POOL_GROUP_DIM, POOL_GROUP_DIM)),
            _const_spec((1, POOL_WIDTH)),
        ],
        out_specs=[
            tile(DN_WIDTH), tile(DN_WIDTH), tile(DN_WIDTH),
            pl.BlockSpec((None, 8, TM), lambda i: (i // tiles_per_gdn_tile, 0, i % tiles_per_gdn_tile)),
            tile(LANES), tile(POOL_WIDTH), tile(DN_WIDTH),
        ],
        out_shape=[
            wide, wide, wide,
            jax.ShapeDtypeStruct((t // TM_GDN, 8, TM_GDN), F32),
            jax.ShapeDtypeStruct((t, LANES), F32),
            jax.ShapeDtypeStruct((t, POOL_WIDTH), BF16),
            wide,
        ],
        scratch_shapes=[
            pltpu.VMEM((TM, D_MODEL), BF16),
            pltpu.VMEM((CONV_HALO + TM, 3 * DN_WIDTH), F32),
            pltpu.VMEM((TM, LANES), F32),
            pltpu.VMEM((POOL_HALO + TM, POOL_WIDTH), F32),
            pltpu.VMEM((TM, DN_WIDTH), F32),
        ],
        compiler_params=pltpu.CompilerParams(
            dimension_semantics=("arbitrary",), vmem_limit_bytes=VMEM_LIMIT),
        name="in_proj",
    )(x2, g, w_qkv, w_ba, w_uz, conv_w, gate_params, w_pool, pool_scale)


def _gdn_prep(chunks, q_ref, k_ref, v_ref, grow_ref, gcol_ref, ops):
    c = GDN_CHUNK
    ii = lax.broadcasted_iota(jnp.int32, (c, c), 0)
    jj = lax.broadcasted_iota(jnp.int32, (c, c), 1)
    tril = ii >= jj
    strict = ii > jj

    def col(r0, j):
        return jnp.broadcast_to(gcol_ref[r0:r0 + c, j:j + 1], (c, c))

    for ci in chunks:
        for h in range(DN_HEADS):
            r0 = ci * c
            lo = h * DN_HEAD_DIM
            qh = _silu(q_ref[r0:r0 + c, lo:lo + DN_HEAD_DIM])
            kh = _silu(k_ref[r0:r0 + c, lo:lo + DN_HEAD_DIM])
            vh = _silu(v_ref[r0:r0 + c, lo:lo + DN_HEAD_DIM])
            qn = qh * lax.rsqrt(jnp.sum(qh * qh, axis=-1, keepdims=True) + EPS) * (DN_HEAD_DIM ** -0.5)
            kn = kh * lax.rsqrt(jnp.sum(kh * kh, axis=-1, keepdims=True) + EPS)
            beta_b = col(r0, h)
            gc_b = col(r0, DN_HEADS + h)
            gc_r = grow_ref[DN_HEADS + h:DN_HEADS + h + 1, r0:r0 + c]
            g_last = gcol_ref[r0 + c - 1:r0 + c, DN_HEADS + h:DN_HEADS + h + 1]
            d = jnp.where(tril, jnp.exp(gc_b - gc_r), 0.0)
            eg = jnp.exp(gc_b)
            kb = kn * beta_b
            kt = kn.T
            ops["unit"].append((ci, h))
            ops["dec"].append(d)
            ops["neg_d"].append(jnp.where(strict, -d, 0.0))
            ops["kt16"].append(kt.astype(BF16))
            ops["lhs_kq"].append(jnp.concatenate([kb, qn], axis=0).astype(BF16))
            ops["vbkg"].append(jnp.concatenate([vh * beta_b, kb * eg], axis=1).astype(BF16))
            ops["qg16"].append((qn * eg).astype(BF16))
            ops["kdt16"].append((kt * jnp.exp(g_last - gc_r)).astype(BF16))
            if h % 2 == 1:
                yield


def _gdn_stages(ops, u_ref, wq_ref, ak_ref):
    c = GDN_CHUNK
    ii = lax.broadcasted_iota(jnp.int32, (c, c), 0)
    jj = lax.broadcasted_iota(jnp.int32, (c, c), 1)
    eye = jnp.where(ii == jj, 1.0, 0.0).astype(F32)
    n = len(ops["unit"])
    kq = [_dot(ops["lhs_kq"][i], ops["kt16"][i]) for i in range(n)]
    yield
    xs = [kq[i][:c] * ops["neg_d"][i] for i in range(n)]
    ps = [eye + xs[i] for i in range(n)]
    att16 = [(kq[i][c:] * ops["dec"][i]).astype(BF16) for i in range(n)]
    x16 = [x.astype(BF16) for x in xs]
    xs = [_dot(x, x) for x in x16]
    yield
    span = 2
    while 2 * span < c:
        x16 = [x.astype(BF16) for x in xs]
        both = [_dot(jnp.concatenate([ps[i].astype(BF16), x16[i]], axis=0), x16[i]) for i in range(n)]
        ps = [ps[i] + both[i][:c] for i in range(n)]
        xs = [both[i][c:] for i in range(n)]
        span *= 2
        yield
    ps = [ps[i] + _dot(ps[i].astype(BF16), xs[i].astype(BF16)) for i in range(n)]
    yield
    uw = [_dot(ps[i].astype(BF16), ops["vbkg"][i]) for i in range(n)]
    for i, (ci, h) in enumerate(ops["unit"]):
        idx = ci * DN_HEADS + h
        u_ref[idx] = uw[i][:, :DN_HEAD_DIM]
        wq_ref[idx] = jnp.concatenate([uw[i][:, DN_HEAD_DIM:].astype(BF16), ops["qg16"][i]], axis=0)
        ak_ref[idx] = jnp.concatenate([att16[i], ops["kdt16"][i]], axis=0)
    yield


def _gdn_phase2(chunks, gcol_ref, sz_ref, og_ref, state_ref, u_ref, wq_ref, ak_ref, y_ref):
    c = GDN_CHUNK
    heads = range(DN_HEADS)
    for ci in chunks:
        r0 = ci * c
        st = [state_ref[h] for h in heads]
        st16 = [s.astype(BF16) for s in st]
        r1 = [_dot(wq_ref[ci * DN_HEADS + h], st16[h]) for h in heads]
        yield
        vn16 = [(u_ref[ci * DN_HEADS + h] - r1[h][:c]).astype(BF16) for h in heads]
        r2 = [_dot(ak_ref[ci * DN_HEADS + h], vn16[h]) for h in heads]
        yield
        for h in heads:
            lo = h * DN_HEAD_DIM
            g_last = gcol_ref[r0 + c - 1:r0 + c, DN_HEADS + h:DN_HEADS + h + 1]
            state_ref[h] = st[h] * jnp.exp(g_last) + r2[h][c:]
            o = r1[h][c:] + r2[h][:c]
            on = o * lax.rsqrt(jnp.mean(o * o, axis=-1, keepdims=True) + EPS) * og_ref[...]
            y_ref[r0:r0 + c, lo:lo + DN_HEAD_DIM] = (on * sz_ref[r0:r0 + c, lo:lo + DN_HEAD_DIM]).astype(y_ref.dtype)


def _gdn_kernel(q_ref, k_ref, v_ref, grow_ref, gcol_ref, sz_ref, og_ref, y_ref, state_ref, u_ref, wq_ref, ak_ref):
    @pl.when(pl.program_id(1) == 0)
    def _():
        state_ref[...] = jnp.zeros_like(state_ref)

    n_chunks = TM_GDN // GDN_CHUNK
    sweeps = [range(g0, g0 + CHUNKS_PER_SWEEP) for g0 in range(0, n_chunks, CHUNKS_PER_SWEEP)]
    new_ops = lambda: {name: [] for name in ("unit", "dec", "neg_d", "kt16", "lhs_kq", "vbkg", "qg16", "kdt16")}
    prep = lambda chunks, ops: _gdn_prep(chunks, q_ref, k_ref, v_ref, grow_ref, gcol_ref, ops)
    phase2 = lambda chunks: _gdn_phase2(chunks, gcol_ref, sz_ref, og_ref, state_ref, u_ref, wq_ref, ak_ref, y_ref)
    ops = new_ops()
    for _ in prep(sweeps[0], ops):
        pass
    for i, sweep in enumerate(sweeps):
        next_ops = new_ops()
        side = []
        if i + 1 < len(sweeps):
            side.append(prep(sweeps[i + 1], next_ops))
        if i > 0:
            side.append(phase2(sweeps[i - 1]))
        _interleave(_gdn_stages(ops, u_ref, wq_ref, ak_ref), _round_robin(side))
        ops = next_ops
    for _ in phase2(sweeps[-1]):
        pass


def _gdn(q3, k3, v3, grow3, gcol3, sz3, o_norm_g):
    b, s, _ = q3.shape
    tiles_per_batch = s // TM_GDN
    units = (TM_GDN // GDN_CHUNK) * DN_HEADS
    tile = lambda width: pl.BlockSpec((None, TM_GDN, width), lambda bi, si: (bi, si, 0))
    return pl.pallas_call(
        _gdn_kernel,
        grid=(b, tiles_per_batch),
        in_specs=[tile(DN_WIDTH), tile(DN_WIDTH), tile(DN_WIDTH),
                  pl.BlockSpec((None, 8, TM_GDN), lambda bi, si: (bi * tiles_per_batch + si, 0, 0)),
                  tile(LANES), tile(DN_WIDTH),
                  _const_spec((1, DN_HEAD_DIM))],
        out_specs=tile(DN_WIDTH),
        out_shape=jax.ShapeDtypeStruct((b, s, DN_WIDTH), BF16),
        scratch_shapes=[
            pltpu.VMEM((DN_HEADS, DN_HEAD_DIM, DN_HEAD_DIM), F32),
            pltpu.VMEM((units, GDN_CHUNK, DN_HEAD_DIM), F32),
            pltpu.VMEM((units, 2 * GDN_CHUNK, DN_HEAD_DIM), BF16),
            pltpu.VMEM((units, 2 * GDN_CHUNK, DN_HEAD_DIM), BF16),
        ],
        compiler_params=pltpu.CompilerParams(
            dimension_semantics=("arbitrary", "arbitrary"), vmem_limit_bytes=VMEM_LIMIT),
        name="gdn",
    )(q3, k3, v3, grow3, gcol3, sz3, o_norm_g)


def _mem_kv_kernel(m_ref, g_ref, wk_ref, wv_ref, kt_ref, v_ref):
    mn = _rms(m_ref[...], g_ref[...]).astype(BF16)
    kt_ref[...] = _dot(mn, wk_ref[...]).T.astype(kt_ref.dtype)
    v_ref[...] = _dot(mn, wv_ref[...]).astype(v_ref.dtype)


def _mem_kv(mem, g, w_k, w_v):
    b, mem_len, _ = mem.shape
    return pl.pallas_call(
        _mem_kv_kernel,
        grid=(b,),
        in_specs=[
            pl.BlockSpec((None, mem_len, D_MODEL), lambda i: (i, 0, 0)),
            _const_spec((1, D_MODEL)),
            _const_spec((D_MODEL, D_MODEL)),
            _const_spec((D_MODEL, D_MODEL)),
        ],
        out_specs=[
            pl.BlockSpec((None, D_MODEL, mem_len), lambda i: (i, 0, 0)),
            pl.BlockSpec((None, mem_len, D_MODEL), lambda i: (i, 0, 0)),
        ],
        out_shape=[
            jax.ShapeDtypeStruct((b, D_MODEL, mem_len), BF16),
            jax.ShapeDtypeStruct((b, mem_len, D_MODEL), BF16),
        ],
        compiler_params=pltpu.CompilerParams(
            dimension_semantics=("arbitrary",), vmem_limit_bytes=VMEM_LIMIT),
        name="mem_kv",
    )(mem, g, w_k, w_v)


def _post_kernel(x_ref, ypool_ref, ydn_ref, kt_ref, v_ref, wout_ref, gxa_ref, wq_ref, wo_ref, gmlp_ref,
                 wup_ref, wdn_ref, gfin_ref, out_ref):
    y = jnp.concatenate([ypool_ref[...], ydn_ref[...]], axis=-1)
    h1 = x_ref[...] + _dot(y, wout_ref[...])
    q = _dot(_rms(h1, gxa_ref[...]).astype(BF16), wq_ref[...]).astype(BF16)
    head_lanes = [slice(hd * XA_HEAD_DIM, (hd + 1) * XA_HEAD_DIM) for hd in range(XA_HEADS)]
    scores = [_dot(q[:, hl], kt_ref[hl, :]) * (XA_HEAD_DIM ** -0.5) for hl in head_lanes]
    expd = [jnp.exp(s - jnp.max(s, axis=-1, keepdims=True)) for s in scores]
    pv = [_dot(e.astype(BF16), v_ref[:, hl]) for e, hl in zip(expd, head_lanes)]
    att = jnp.concatenate([(o / jnp.sum(e, axis=-1, keepdims=True)).astype(BF16) for e, o in zip(expd, pv)],
                          axis=-1)
    h2 = h1 + _dot(att, wo_ref[...])
    hn2 = _rms(h2, gmlp_ref[...]).astype(BF16)
    acc = h2
    for j in range(D_FF // FF_BLOCK):
        a = jnp.maximum(_dot(hn2, wup_ref[:, j * FF_BLOCK:(j + 1) * FF_BLOCK]), 0.0)
        acc = acc + _dot((a * a).astype(BF16), wdn_ref[j * FF_BLOCK:(j + 1) * FF_BLOCK, :])
    out_ref[...] = _rms(acc, gfin_ref[...])


def _post(x2, y_pool, y_dn, kt3, v3, w_out, g_xa, w_q, w_o, g_mlp, w_up, w_dn, g_fin, seq):
    t = x2.shape[0]
    mem_len = v3.shape[1]
    tiles_per_batch = seq // TM
    tile = pl.BlockSpec((TM, D_MODEL), lambda i: (i, 0))
    half_tile = pl.BlockSpec((TM, D_MODEL // 2), lambda i: (i, 0))
    return pl.pallas_call(
        _post_kernel,
        grid=(t // TM,),
        in_specs=[
            tile, half_tile, half_tile,
            pl.BlockSpec((None, D_MODEL, mem_len), lambda i: (i // tiles_per_batch, 0, 0)),
            pl.BlockSpec((None, mem_len, D_MODEL), lambda i: (i // tiles_per_batch, 0, 0)),
            _const_spec((D_MODEL, D_MODEL)),
            _const_spec((1, D_MODEL)),
            _const_spec((D_MODEL, D_MODEL)),
            _const_spec((D_MODEL, D_MODEL)),
            _const_spec((1, D_MODEL)),
            _const_spec((D_MODEL, D_FF)),
            _const_spec((D_FF, D_MODEL)),
            _const_spec((1, D_MODEL)),
        ],
        out_specs=tile,
        out_shape=jax.ShapeDtypeStruct((t, D_MODEL), F32),
        compiler_params=pltpu.CompilerParams(
            dimension_semantics=("arbitrary",), vmem_limit_bytes=VMEM_LIMIT),
        name="post",
    )(x2, y_pool, y_dn, kt3, v3, w_out, g_xa, w_q, w_o, g_mlp, w_up, w_dn, g_fin)


def kernel(x, mem, norm_mix_g, w_in, w_pool, pool_scale, conv_w, a_log, dt_bias, dn_out_norm_g, w_out,
           norm_xattn_g, mem_norm_g, w_xq, w_xk, w_xv, w_xo, norm_mlp_g, w_up, w_down, final_norm_g):
    b, s, d = x.shape
    assert w_in.shape[0] == 1, "single-layer problem: the final RMSNorm is fused into the layer's post stage"
    row = lambda v: v.reshape(1, -1).astype(F32)

    w_qkv = w_in[0][:, QKV_OFF:QKV_OFF + 3 * DN_WIDTH].astype(BF16)
    w_uz = jnp.concatenate([w_in[0][:, :POOL_WIDTH], w_in[0][:, Z_OFF:Z_OFF + DN_WIDTH]], axis=1).astype(BF16)
    w_ba = jnp.pad(w_in[0][:, MAIN_COLS:], ((0, 0), (0, LANES - 2 * DN_HEADS))).astype(BF16)
    gate_params = jnp.zeros((8, LANES), F32)
    gate_params = gate_params.at[DN_HEADS:, 0].set(a_log[0]).at[DN_HEADS:, 1].set(dt_bias[0])

    x2 = x.reshape(b * s, d)
    q, k, v, grow, gcol, y_pool, sz = _in_proj(x2, row(norm_mix_g[0]), w_qkv, w_ba, w_uz, conv_w[0].astype(F32),
                                               gate_params, w_pool[0].astype(BF16), row(pool_scale[0]), s)
    per_batch = lambda a: a.reshape(b, s, a.shape[-1])
    y_dn = _gdn(per_batch(q), per_batch(k), per_batch(v), grow, per_batch(gcol), per_batch(sz),
                row(dn_out_norm_g[0]))
    kt3, v3 = _mem_kv(mem, row(mem_norm_g[0]), w_xk[0].astype(BF16), w_xv[0].astype(BF16))
    out = _post(x2, y_pool, y_dn.reshape(b * s, DN_WIDTH), kt3, v3, w_out[0].astype(BF16), row(norm_xattn_g[0]),
                w_xq[0].astype(BF16), w_xo[0].astype(BF16), row(norm_mlp_g[0]), w_up[0].astype(BF16),
                w_down[0].astype(BF16), row(final_norm_g), s)
    return out.reshape(b, s, d)
```

```python
import functools
import itertools

import jax
import jax.numpy as jnp
from jax import lax
from jax.experimental import pallas as pl
from jax.experimental.pallas import tpu as pltpu

D_MODEL = 1024
POOL_WIDTH = 512
POOL_WINDOWS = (2, 4, 8, 16)
POOL_GROUP_DIM = 128
DN_WIDTH = 512
DN_HEADS = 4
DN_HEAD_DIM = 128
CONV_K = 4
MAIN_COLS = POOL_WIDTH + 4 * DN_WIDTH
QKV_OFF = POOL_WIDTH
Z_OFF = POOL_WIDTH + 3 * DN_WIDTH
XA_HEADS = 4
XA_HEAD_DIM = 256
D_FF = 4096
EPS = 1e-6

LANES = 128
GDN_CHUNK = 128
POOL_HALO = 16
CONV_HALO = 8
TM = 1024
TM_GDN = 1024
TM_POST = 512
CHUNKS_PER_SWEEP = 2
FF_BLOCK = 1024
_ROW_BLOCKS = TM // GDN_CHUNK
PREP_SEGMENTS_AFTER_IN_PROJ_SEGMENT = (0, 0, 1 + _ROW_BLOCKS, _ROW_BLOCKS, _ROW_BLOCKS, len(POOL_WINDOWS))
VMEM_LIMIT = 56 * 1024 * 1024

F32 = jnp.float32
BF16 = jnp.bfloat16


def _dot(a, b):
    return jnp.dot(a, b, preferred_element_type=F32)


def _rms(x, g):
    return x * lax.rsqrt(jnp.mean(x * x, axis=-1, keepdims=True) + EPS) * g


def _silu(x):
    return x * jax.nn.sigmoid(x)


def _const_spec(shape):
    zeros = (0,) * len(shape)
    return pl.BlockSpec(shape, lambda *_: zeros, pipeline_mode=pl.Buffered(1))


def _round_robin(generators):
    live = list(generators)
    while live:
        for g in list(live):
            try:
                yield next(g)
            except StopIteration:
                live.remove(g)


def _interleave(first, second, second_per_first=1):
    for _ in first:
        for _ in range(second_per_first):
            next(second, None)
    for _ in second:
        pass


def _pool_mixer(t0, wpool_ref, pscale_ref, y_ref, extp_ref):
    tm = y_ref.shape[0]
    t_glob = lax.broadcasted_iota(jnp.int32, (tm, POOL_GROUP_DIM), 0) + t0
    for gi, win in enumerate(POOL_WINDOWS):
        lo = gi * POOL_GROUP_DIM
        assert win & (win - 1) == 0 and win <= POOL_HALO, "doubling needs power-of-two windows within the halo"
        acc = extp_ref[:, lo:lo + POOL_GROUP_DIM]
        shift = 1
        while shift < win:
            acc = acc + pltpu.roll(acc, shift, axis=0)
            shift *= 2
        acc = acc[POOL_HALO:, :]
        u = extp_ref[POOL_HALO:POOL_HALO + tm, lo:lo + POOL_GROUP_DIM]
        cnt = jnp.minimum(t_glob + 1, win).astype(F32)
        pooled = acc / cnt - u
        mixed = _dot(pooled.astype(BF16), wpool_ref[gi])
        y_ref[:, lo:lo + POOL_GROUP_DIM] = (mixed * pscale_ref[:, lo:lo + POOL_GROUP_DIM]).astype(y_ref.dtype)
        yield
    extp_ref[0:POOL_HALO, :] = extp_ref[tm:tm + POOL_HALO, :]


def _in_proj_matmuls(x_ref, g_ref, wqkv_ref, wba_ref, wuz_ref, hn_ref, qkv_ref, ba_ref, extp_ref, z_ref):
    hn_ref[...] = _rms(x_ref[...], g_ref[...]).astype(BF16)
    ba_ref[...] = _dot(hn_ref[...], wba_ref[...])
    yield
    for part in range(3):
        lanes = slice(part * DN_WIDTH, (part + 1) * DN_WIDTH)
        qkv_ref[CONV_HALO:CONV_HALO + TM, lanes] = _dot(hn_ref[...], wqkv_ref[:, lanes])
        yield
    extp_ref[POOL_HALO:, :] = _dot(hn_ref[...], wuz_ref[:, 0:POOL_WIDTH])
    yield
    z_ref[...] = _dot(hn_ref[...], wuz_ref[:, POOL_WIDTH:POOL_WIDTH + DN_WIDTH])
    yield


def _in_proj_prep(qkv_ref, ba_ref, z_ref, convw_ref, gp_ref, q_ref, k_ref, v_ref, grow_ref, gcol_ref, sz_ref, pool):
    c = GDN_CHUNK

    bat = ba_ref[...].T[0:8, :]
    gp = gp_ref[...]
    a_log = gp[:, 0:1]
    dt_bias = gp[:, 1:2]
    xa = bat + dt_bias
    softplus = jnp.maximum(xa, 0.0) + jnp.log1p(jnp.exp(-jnp.abs(xa)))
    cs = -jnp.exp(a_log) * softplus
    lane = lax.broadcasted_iota(jnp.int32, (8, TM), 1) & (c - 1)
    sh = 1
    while sh < c:
        cs = cs + jnp.where(lane >= sh, pltpu.roll(cs, sh, axis=1), 0.0)
        sh *= 2
    row8 = lax.broadcasted_iota(jnp.int32, (8, TM), 0)
    stack = jnp.where(row8 < DN_HEADS, jax.nn.sigmoid(bat), cs)
    grow_ref[...] = stack
    gcol_ref[...] = jnp.concatenate([stack, jnp.zeros((LANES - 8, TM), F32)], axis=0).T
    yield

    for part, dst in enumerate((q_ref, k_ref, v_ref)):
        for r0 in range(0, TM, c):
            for h in range(DN_HEADS):
                lo = h * DN_HEAD_DIM
                lanes = slice(part * DN_WIDTH + lo, part * DN_WIDTH + lo + DN_HEAD_DIM)
                xe = qkv_ref[r0:r0 + CONV_HALO + c, lanes]
                xc = convw_ref[CONV_K - 1:CONV_K, lanes] * xe
                for d in range(1, CONV_K):
                    xc = xc + convw_ref[CONV_K - 1 - d:CONV_K - d, lanes] * pltpu.roll(xe, d, axis=0)
                dst[r0:r0 + c, lo:lo + DN_HEAD_DIM] = xc[CONV_HALO:, :]
            yield
    yield from pool
    for r0 in range(0, TM, c):
        sz_ref[r0:r0 + c, :] = _silu(z_ref[r0:r0 + c, :])
        yield


def _in_proj_kernel(tiles_per_batch, n_cast, *refs):
    x_ref, g_ref, wqkv_ref, wba_ref, wuz_ref, convw_ref, gp_ref, wpool_ref, pscale_ref = refs[:9]
    cast_src = refs[9:9 + n_cast]
    q_ref, k_ref, v_ref, grow_ref, gcol_ref, ypool_ref, sz_ref = refs[9 + n_cast:16 + n_cast]
    cast_dst = refs[16 + n_cast:16 + 2 * n_cast]
    hn_ref, qkv_ref, ba_ref, extp_ref, z_ref = refs[16 + 2 * n_cast:]
    t_in_batch = lax.rem(pl.program_id(0), jnp.int32(tiles_per_batch))

    for src, dst in zip(cast_src, cast_dst):
        dst[...] = src[...].astype(dst.dtype)

    @pl.when(t_in_batch == 0)
    def _():
        qkv_ref[0:CONV_HALO, :] = jnp.zeros((CONV_HALO, 3 * DN_WIDTH), F32)
        extp_ref[0:POOL_HALO, :] = jnp.zeros((POOL_HALO, POOL_WIDTH), F32)

    @pl.when(t_in_batch != 0)
    def _():
        qkv_ref[0:CONV_HALO, :] = qkv_ref[TM:TM + CONV_HALO, :]

    matmuls = _in_proj_matmuls(x_ref, g_ref, wqkv_ref, wba_ref, wuz_ref, hn_ref, qkv_ref, ba_ref, extp_ref, z_ref)
    pool = _pool_mixer(t_in_batch * TM, wpool_ref, pscale_ref, ypool_ref, extp_ref)
    prep = _in_proj_prep(qkv_ref, ba_ref, z_ref, convw_ref, gp_ref, q_ref, k_ref, v_ref, grow_ref, gcol_ref,
                         sz_ref, pool)
    for n_prep_segments in PREP_SEGMENTS_AFTER_IN_PROJ_SEGMENT:
        next(matmuls)
        for _ in range(n_prep_segments):
            next(prep)
    done = object()
    assert next(matmuls, done) is done
    for _ in prep:
        pass


def _in_proj(x2, g, w_qkv, w_ba, w_uz, conv_w, gate_params, w_pool, pool_scale, later_weights, seq):
    t = x2.shape[0]
    n_tiles = t // TM
    tiles_per_gdn_tile = TM_GDN // TM
    tile = lambda width: pl.BlockSpec((TM, width), lambda i: (i, 0))
    wide = jax.ShapeDtypeStruct((t, DN_WIDTH), F32)
    assert all(w.shape[0] % (16 * n_tiles) == 0 for w in later_weights)
    cast_specs = [pl.BlockSpec((w.shape[0] // n_tiles, w.shape[1]), lambda i: (i, 0)) for w in later_weights]
    outs = pl.pallas_call(
        functools.partial(_in_proj_kernel, seq // TM, len(later_weights)),
        grid=(n_tiles,),
        in_specs=[
            tile(D_MODEL),
            _const_spec((1, D_MODEL)),
            _const_spec((D_MODEL, 3 * DN_WIDTH)),
            _const_spec((D_MODEL, LANES)),
            _const_spec((D_MODEL, POOL_WIDTH + DN_WIDTH)),
            _const_spec((CONV_K, 3 * DN_WIDTH)),
            _const_spec((8, LANES)),
            _const_spec((len(POOL_WINDOWS), POOL_GROUP_DIM, POOL_GROUP_DIM)),
            _const_spec((1, POOL_WIDTH)),
        ] + cast_specs,
        out_specs=[
            tile(DN_WIDTH), tile(DN_WIDTH), tile(DN_WIDTH),
            pl.BlockSpec((None, 8, TM), lambda i: (i // tiles_per_gdn_tile, 0, i % tiles_per_gdn_tile)),
            tile(LANES), tile(POOL_WIDTH), tile(DN_WIDTH),
        ] + cast_specs,
        out_shape=[
            wide, wide, wide,
            jax.ShapeDtypeStruct((t // TM_GDN, 8, TM_GDN), F32),
            jax.ShapeDtypeStruct((t, LANES), F32),
            jax.ShapeDtypeStruct((t, POOL_WIDTH), BF16),
            wide,
        ] + [jax.ShapeDtypeStruct(w.shape, BF16) for w in later_weights],
        scratch_shapes=[
            pltpu.VMEM((TM, D_MODEL), BF16),
            pltpu.VMEM((CONV_HALO + TM, 3 * DN_WIDTH), F32),
            pltpu.VMEM((TM, LANES), F32),
            pltpu.VMEM((POOL_HALO + TM, POOL_WIDTH), F32),
            pltpu.VMEM((TM, DN_WIDTH), F32),
        ],
        compiler_params=pltpu.CompilerParams(
            dimension_semantics=("arbitrary",), vmem_limit_bytes=VMEM_LIMIT),
        name="in_proj",
    )(x2, g, w_qkv, w_ba, w_uz, conv_w, gate_params, w_pool, pool_scale, *later_weights)
    return (*outs[:7], tuple(outs[7:]))


def _gdn_prep(chunks, q_ref, k_ref, v_ref, grow_ref, gcol_ref, ops):
    c = GDN_CHUNK
    ii = lax.broadcasted_iota(jnp.int32, (c, c), 0)
    jj = lax.broadcasted_iota(jnp.int32, (c, c), 1)
    tril = ii >= jj
    strict = ii > jj

    def col(r0, j):
        return jnp.broadcast_to(gcol_ref[r0:r0 + c, j:j + 1], (c, c))

    for ci in chunks:
        for h in range(DN_HEADS):
            r0 = ci * c
            lo = h * DN_HEAD_DIM
            qh = _silu(q_ref[r0:r0 + c, lo:lo + DN_HEAD_DIM])
            kh = _silu(k_ref[r0:r0 + c, lo:lo + DN_HEAD_DIM])
            vh = _silu(v_ref[r0:r0 + c, lo:lo + DN_HEAD_DIM])
            qn = qh * lax.rsqrt(jnp.sum(qh * qh, axis=-1, keepdims=True) + EPS) * (DN_HEAD_DIM ** -0.5)
            kn = kh * lax.rsqrt(jnp.sum(kh * kh, axis=-1, keepdims=True) + EPS)
            beta_b = col(r0, h)
            gc_b = col(r0, DN_HEADS + h)
            gc_r = grow_ref[DN_HEADS + h:DN_HEADS + h + 1, r0:r0 + c]
            g_last = gcol_ref[r0 + c - 1:r0 + c, DN_HEADS + h:DN_HEADS + h + 1]
            d = jnp.where(tril, jnp.exp(gc_b - gc_r), 0.0)
            eg = jnp.exp(gc_b)
            kb = kn * beta_b
            kt = kn.T
            ops["unit"].append((ci, h))
            ops["dec"].append(d)
            ops["neg_d"].append(jnp.where(strict, -d, 0.0))
            ops["kt16"].append(kt.astype(BF16))
            ops["lhs_kq"].append(jnp.concatenate([kb, qn], axis=0).astype(BF16))
            ops["vbkg"].append(jnp.concatenate([vh * beta_b, kb * eg], axis=1).astype(BF16))
            ops["qg16"].append((qn * eg).astype(BF16))
            ops["kdt16"].append((kt * jnp.exp(g_last - gc_r)).astype(BF16))
            if h % 2 == 1:
                yield


def _gdn_stages(ops, u_ref, wq_ref, ak_ref):
    c = GDN_CHUNK
    ii = lax.broadcasted_iota(jnp.int32, (c, c), 0)
    jj = lax.broadcasted_iota(jnp.int32, (c, c), 1)
    eye = jnp.where(ii == jj, 1.0, 0.0).astype(F32)
    n = len(ops["unit"])
    kq = [_dot(ops["lhs_kq"][i], ops["kt16"][i]) for i in range(n)]
    yield
    xs = [kq[i][:c] * ops["neg_d"][i] for i in range(n)]
    ps = [eye + xs[i] for i in range(n)]
    att16 = [(kq[i][c:] * ops["dec"][i]).astype(BF16) for i in range(n)]
    x16 = [x.astype(BF16) for x in xs]
    x16 = [_dot(x, x).astype(BF16) for x in x16]
    yield
    span = 2
    while 2 * span < c:
        both = [_dot(jnp.concatenate([ps[i].astype(BF16), x16[i]], axis=0), x16[i]) for i in range(n)]
        ps = [ps[i] + both[i][:c] for i in range(n)]
        x16 = [both[i][c:].astype(BF16) for i in range(n)]
        span *= 2
        yield
    ps = [ps[i] + _dot(ps[i].astype(BF16), x16[i]) for i in range(n)]
    yield
    uw = [_dot(ps[i].astype(BF16), ops["vbkg"][i]) for i in range(n)]
    for i, (ci, h) in enumerate(ops["unit"]):
        idx = ci * DN_HEADS + h
        u_ref[idx] = uw[i][:, :DN_HEAD_DIM]
        wq_ref[idx] = jnp.concatenate([uw[i][:, DN_HEAD_DIM:].astype(BF16), ops["qg16"][i]], axis=0)
        ak_ref[idx] = jnp.concatenate([att16[i], ops["kdt16"][i]], axis=0)
    yield


def _gdn_phase2(chunks, gcol_ref, sz_ref, og_ref, state_ref, u_ref, wq_ref, ak_ref, y_ref):
    c = GDN_CHUNK
    heads = range(DN_HEADS)
    for ci in chunks:
        r0 = ci * c
        st = [state_ref[h] for h in heads]
        st16 = [s.astype(BF16) for s in st]
        r1 = [_dot(wq_ref[ci * DN_HEADS + h], st16[h]) for h in heads]
        yield
        vn16 = [(u_ref[ci * DN_HEADS + h] - r1[h][:c]).astype(BF16) for h in heads]
        r2 = [_dot(ak_ref[ci * DN_HEADS + h], vn16[h]) for h in heads]
        yield
        for h in heads:
            lo = h * DN_HEAD_DIM
            g_last = gcol_ref[r0 + c - 1:r0 + c, DN_HEADS + h:DN_HEADS + h + 1]
            state_ref[h] = st[h] * jnp.exp(g_last) + r2[h][c:]
            o = r1[h][c:] + r2[h][:c]
            on = o * lax.rsqrt(jnp.mean(o * o, axis=-1, keepdims=True) + EPS) * og_ref[...]
            y_ref[r0:r0 + c, lo:lo + DN_HEAD_DIM] = (on * sz_ref[r0:r0 + c, lo:lo + DN_HEAD_DIM]).astype(y_ref.dtype)


def _gdn_kernel(q_ref, k_ref, v_ref, grow_ref, gcol_ref, sz_ref, og_ref, y_ref, state_ref, u_ref, wq_ref, ak_ref):
    @pl.when(pl.program_id(1) == 0)
    def _():
        state_ref[...] = jnp.zeros_like(state_ref)

    n_chunks = TM_GDN // GDN_CHUNK
    sweeps = [range(g0, g0 + CHUNKS_PER_SWEEP) for g0 in range(0, n_chunks, CHUNKS_PER_SWEEP)]
    new_ops = lambda: {name: [] for name in ("unit", "dec", "neg_d", "kt16", "lhs_kq", "vbkg", "qg16", "kdt16")}
    prep = lambda chunks, ops: _gdn_prep(chunks, q_ref, k_ref, v_ref, grow_ref, gcol_ref, ops)
    phase2 = lambda chunks: _gdn_phase2(chunks, gcol_ref, sz_ref, og_ref, state_ref, u_ref, wq_ref, ak_ref, y_ref)
    ops = new_ops()
    for _ in prep(sweeps[0], ops):
        pass
    for i, sweep in enumerate(sweeps):
        next_ops = new_ops()
        side = []
        if i + 1 < len(sweeps):
            side.append(prep(sweeps[i + 1], next_ops))
        if i > 0:
            side.append(phase2(sweeps[i - 1]))
        _interleave(_gdn_stages(ops, u_ref, wq_ref, ak_ref), _round_robin(side))
        ops = next_ops
    for _ in phase2(sweeps[-1]):
        pass


def _gdn(q3, k3, v3, grow3, gcol3, sz3, o_norm_g):
    b, s, _ = q3.shape
    tiles_per_batch = s // TM_GDN
    units = (TM_GDN // GDN_CHUNK) * DN_HEADS
    tile = lambda width: pl.BlockSpec((None, TM_GDN, width), lambda bi, si: (bi, si, 0))
    return pl.pallas_call(
        _gdn_kernel,
        grid=(b, tiles_per_batch),
        in_specs=[tile(DN_WIDTH), tile(DN_WIDTH), tile(DN_WIDTH),
                  pl.BlockSpec((None, 8, TM_GDN), lambda bi, si: (bi * tiles_per_batch + si, 0, 0)),
                  tile(LANES), tile(DN_WIDTH),
                  _const_spec((1, DN_HEAD_DIM))],
        out_specs=tile(DN_WIDTH),
        out_shape=jax.ShapeDtypeStruct((b, s, DN_WIDTH), BF16),
        scratch_shapes=[
            pltpu.VMEM((DN_HEADS, DN_HEAD_DIM, DN_HEAD_DIM), F32),
            pltpu.VMEM((units, GDN_CHUNK, DN_HEAD_DIM), F32),
            pltpu.VMEM((units, 2 * GDN_CHUNK, DN_HEAD_DIM), BF16),
            pltpu.VMEM((units, 2 * GDN_CHUNK, DN_HEAD_DIM), BF16),
        ],
        compiler_params=pltpu.CompilerParams(
            dimension_semantics=("arbitrary", "arbitrary"), vmem_limit_bytes=VMEM_LIMIT),
        name="gdn",
    )(q3, k3, v3, grow3, gcol3, sz3, o_norm_g)


def _mem_kv_kernel(m_ref, g_ref, wk_ref, wv_ref, kt_ref, v_ref):
    mn = _rms(m_ref[...], g_ref[...]).astype(BF16)
    kt_ref[...] = _dot(mn, wk_ref[...]).T.astype(kt_ref.dtype)
    v_ref[...] = _dot(mn, wv_ref[...]).astype(v_ref.dtype)


def _mem_kv(mem, g, w_k, w_v):
    b, mem_len, _ = mem.shape
    return pl.pallas_call(
        _mem_kv_kernel,
        grid=(b,),
        in_specs=[
            pl.BlockSpec((None, mem_len, D_MODEL), lambda i: (i, 0, 0)),
            _const_spec((1, D_MODEL)),
            _const_spec((D_MODEL, D_MODEL)),
            _const_spec((D_MODEL, D_MODEL)),
        ],
        out_specs=[
            pl.BlockSpec((None, D_MODEL, mem_len), lambda i: (i, 0, 0)),
            pl.BlockSpec((None, mem_len, D_MODEL), lambda i: (i, 0, 0)),
        ],
        out_shape=[
            jax.ShapeDtypeStruct((b, D_MODEL, mem_len), BF16),
            jax.ShapeDtypeStruct((b, mem_len, D_MODEL), BF16),
        ],
        compiler_params=pltpu.CompilerParams(
            dimension_semantics=("arbitrary",), vmem_limit_bytes=VMEM_LIMIT),
        name="mem_kv",
    )(mem, g, w_k, w_v)


def _post_kernel(x_ref, ypool_ref, ydn_ref, kt_ref, v_ref, wout_ref, gxa_ref, wq_ref, wo_ref, gmlp_ref,
                 wup_ref, wdn_ref, gfin_ref, out_ref):
    y = jnp.concatenate([ypool_ref[...], ydn_ref[...]], axis=-1)
    h1 = x_ref[...] + _dot(y, wout_ref[...])
    q = _dot(_rms(h1, gxa_ref[...]).astype(BF16), wq_ref[...]).astype(BF16)
    head_lanes = [slice(hd * XA_HEAD_DIM, (hd + 1) * XA_HEAD_DIM) for hd in range(XA_HEADS)]
    scores = [_dot(q[:, hl], kt_ref[hl, :]) * (XA_HEAD_DIM ** -0.5) for hl in head_lanes]
    expd = [jnp.exp(s - jnp.max(s, axis=-1, keepdims=True)) for s in scores]
    pv = [_dot(e.astype(BF16), v_ref[:, hl]) for e, hl in zip(expd, head_lanes)]
    att = jnp.concatenate([(o / jnp.sum(e, axis=-1, keepdims=True)).astype(BF16) for e, o in zip(expd, pv)],
                          axis=-1)
    h2 = h1 + _dot(att, wo_ref[...])
    hn2 = _rms(h2, gmlp_ref[...]).astype(BF16)
    acc = h2
    for j in range(D_FF // FF_BLOCK):
        a = jnp.maximum(_dot(hn2, wup_ref[:, j * FF_BLOCK:(j + 1) * FF_BLOCK]), 0.0)
        acc = acc + _dot((a * a).astype(BF16), wdn_ref[j * FF_BLOCK:(j + 1) * FF_BLOCK, :])
    out_ref[...] = _rms(acc, gfin_ref[...])


def _post(x2, y_pool, y_dn, kt3, v3, w_out, g_xa, w_q, w_o, g_mlp, w_up, w_dn, g_fin, seq):
    t = x2.shape[0]
    mem_len = v3.shape[1]
    tiles_per_batch = seq // TM_POST
    tile = pl.BlockSpec((TM_POST, D_MODEL), lambda i: (i, 0))
    half_tile = pl.BlockSpec((TM_POST, D_MODEL // 2), lambda i: (i, 0))
    return pl.pallas_call(
        _post_kernel,
        grid=(t // TM_POST,),
        in_specs=[
            tile, half_tile, half_tile,
            pl.BlockSpec((None, D_MODEL, mem_len), lambda i: (i // tiles_per_batch, 0, 0)),
            pl.BlockSpec((None, mem_len, D_MODEL), lambda i: (i // tiles_per_batch, 0, 0)),
            _const_spec((D_MODEL, D_MODEL)),
            _const_spec((1, D_MODEL)),
            _const_spec((D_MODEL, D_MODEL)),
            _const_spec((D_MODEL, D_MODEL)),
            _const_spec((1, D_MODEL)),
            _const_spec((D_MODEL, D_FF)),
            _const_spec((D_FF, D_MODEL)),
            _const_spec((1, D_MODEL)),
        ],
        out_specs=tile,
        out_shape=jax.ShapeDtypeStruct((t, D_MODEL), F32),
        compiler_params=pltpu.CompilerParams(
            dimension_semantics=("arbitrary",), vmem_limit_bytes=VMEM_LIMIT),
        name="post",
    )(x2, y_pool, y_dn, kt3, v3, w_out, g_xa, w_q, w_o, g_mlp, w_up, w_dn, g_fin)


def kernel(x, mem, norm_mix_g, w_in, w_pool, pool_scale, conv_w, a_log, dt_bias, dn_out_norm_g, w_out,
           norm_xattn_g, mem_norm_g, w_xq, w_xk, w_xv, w_xo, norm_mlp_g, w_up, w_down, final_norm_g):
    b, s, d = x.shape
    assert w_in.shape[0] == 1, "single-layer problem: the final RMSNorm is fused into the layer's post stage"
    row = lambda v: v.reshape(1, -1).astype(F32)

    w_qkv = w_in[0][:, QKV_OFF:QKV_OFF + 3 * DN_WIDTH].astype(BF16)
    w_uz = jnp.concatenate([w_in[0][:, :POOL_WIDTH], w_in[0][:, Z_OFF:Z_OFF + DN_WIDTH]], axis=1).astype(BF16)
    w_ba = jnp.pad(w_in[0][:, MAIN_COLS:], ((0, 0), (0, LANES - 2 * DN_HEADS))).astype(BF16)
    gate_params = jnp.zeros((8, LANES), F32)
    gate_params = gate_params.at[DN_HEADS:, 0].set(a_log[0]).at[DN_HEADS:, 1].set(dt_bias[0])

    x2 = x.reshape(b * s, d)
    later_weights = [w_xk[0], w_xv[0], w_out[0], w_xq[0], w_xo[0], w_up[0], w_down[0]]
    q, k, v, grow, gcol, y_pool, sz, later16 = _in_proj(
        x2, row(norm_mix_g[0]), w_qkv, w_ba, w_uz, conv_w[0].astype(F32), gate_params, w_pool[0].astype(BF16),
        row(pool_scale[0]), later_weights, s)
    w_xk16, w_xv16, w_out16, w_xq16, w_xo16, w_up16, w_down16 = later16
    per_batch = lambda a: a.reshape(b, s, a.shape[-1])
    y_dn = _gdn(per_batch(q), per_batch(k), per_batch(v), grow, per_batch(gcol), per_batch(sz),
                row(dn_out_norm_g[0]))
    kt3, v3 = _mem_kv(mem, row(mem_norm_g[0]), w_xk16, w_xv16)
    out = _post(x2, y_pool, y_dn.reshape(b * s, DN_WIDTH), kt3, v3, w_out16, row(norm_xattn_g[0]),
                w_xq16, w_xo16, row(norm_mlp_g[0]), w_up16, w_down16, row(final_norm_g), s)
    return out.reshape(b, s, d)
```

```python
import functools
import itertools

import jax
import jax.numpy as jnp
from jax import lax
from jax.experimental import pallas as pl
from jax.experimental.pallas import tpu as pltpu

D_MODEL = 1024
POOL_WIDTH = 512
POOL_WINDOWS = (2, 4, 8, 16)
POOL_GROUP_DIM = 128
DN_WIDTH = 512
DN_HEADS = 4
DN_HEAD_DIM = 128
CONV_K = 4
MAIN_COLS = POOL_WIDTH + 4 * DN_WIDTH
QKV_OFF = POOL_WIDTH
Z_OFF = POOL_WIDTH + 3 * DN_WIDTH
XA_HEADS = 4
XA_HEAD_DIM = 256
D_FF = 4096
EPS = 1e-6

LANES = 128
GDN_CHUNK = 128
POOL_HALO = 16
CONV_HALO = 8
TM = 1024
TM_GDN = 1024
TM_POST = 512
CHUNKS_PER_SWEEP = 2
FF_BLOCK = 1024
_ROW_BLOCKS = TM // GDN_CHUNK
PREP_SEGMENTS_AFTER_IN_PROJ_SEGMENT = (0, 0, 1 + _ROW_BLOCKS, _ROW_BLOCKS, _ROW_BLOCKS, len(POOL_WINDOWS))
VMEM_LIMIT = 56 * 1024 * 1024

F32 = jnp.float32
BF16 = jnp.bfloat16


def _dot(a, b):
    return jnp.dot(a, b, preferred_element_type=F32)


def _rms(x, g):
    return x * lax.rsqrt(jnp.mean(x * x, axis=-1, keepdims=True) + EPS) * g


def _silu(x):
    return x * jax.nn.sigmoid(x)


def _const_spec(shape):
    zeros = (0,) * len(shape)
    return pl.BlockSpec(shape, lambda *_: zeros, pipeline_mode=pl.Buffered(1))


def _round_robin(generators):
    live = list(generators)
    while live:
        for g in list(live):
            try:
                yield next(g)
            except StopIteration:
                live.remove(g)


def _interleave(first, second, second_per_first=1):
    for _ in first:
        for _ in range(second_per_first):
            next(second, None)
    for _ in second:
        pass


def _pool_mixer(t0, wpool_ref, pscale_ref, y_ref, extp_ref):
    tm = y_ref.shape[0]
    t_glob = lax.broadcasted_iota(jnp.int32, (tm, POOL_GROUP_DIM), 0) + t0
    for gi, win in enumerate(POOL_WINDOWS):
        lo = gi * POOL_GROUP_DIM
        assert win & (win - 1) == 0 and win <= POOL_HALO, "doubling needs power-of-two windows within the halo"
        acc = extp_ref[:, lo:lo + POOL_GROUP_DIM]
        shift = 1
        while shift < win:
            acc = acc + pltpu.roll(acc, shift, axis=0)
            shift *= 2
        acc = acc[POOL_HALO:, :]
        u = extp_ref[POOL_HALO:POOL_HALO + tm, lo:lo + POOL_GROUP_DIM]
        cnt = jnp.minimum(t_glob + 1, win).astype(F32)
        pooled = acc / cnt - u
        mixed = _dot(pooled.astype(BF16), wpool_ref[gi])
        y_ref[:, lo:lo + POOL_GROUP_DIM] = (mixed * pscale_ref[:, lo:lo + POOL_GROUP_DIM]).astype(y_ref.dtype)
        yield
    extp_ref[0:POOL_HALO, :] = extp_ref[tm:tm + POOL_HALO, :]


def _in_proj_matmuls(x_ref, g_ref, wqkv_ref, wba_ref, wuz_ref, hn_ref, qkv_ref, ba_ref, extp_ref, z_ref):
    hn_ref[...] = _rms(x_ref[...], g_ref[...]).astype(BF16)
    ba_ref[...] = _dot(hn_ref[...], wba_ref[...])
    yield
    for part in range(3):
        lanes = slice(part * DN_WIDTH, (part + 1) * DN_WIDTH)
        qkv_ref[CONV_HALO:CONV_HALO + TM, lanes] = _dot(hn_ref[...], wqkv_ref[:, lanes])
        yield
    extp_ref[POOL_HALO:, :] = _dot(hn_ref[...], wuz_ref[:, 0:POOL_WIDTH])
    yield
    z_ref[...] = _dot(hn_ref[...], wuz_ref[:, POOL_WIDTH:POOL_WIDTH + DN_WIDTH])
    yield


def _in_proj_prep(qkv_ref, ba_ref, z_ref, convw_ref, gp_ref, q_ref, k_ref, v_ref, grow_ref, gcol_ref, sz_ref, pool):
    c = GDN_CHUNK

    bat = ba_ref[...].T[0:8, :]
    gp = gp_ref[...]
    a_log = gp[:, 0:1]
    dt_bias = gp[:, 1:2]
    xa = bat + dt_bias
    softplus = jnp.maximum(xa, 0.0) + jnp.log1p(jnp.exp(-jnp.abs(xa)))
    cs = -jnp.exp(a_log) * softplus
    lane = lax.broadcasted_iota(jnp.int32, (8, TM), 1) & (c - 1)
    sh = 1
    while sh < c:
        cs = cs + jnp.where(lane >= sh, pltpu.roll(cs, sh, axis=1), 0.0)
        sh *= 2
    row8 = lax.broadcasted_iota(jnp.int32, (8, TM), 0)
    stack = jnp.where(row8 < DN_HEADS, jax.nn.sigmoid(bat), cs)
    grow_ref[...] = stack
    gcol_ref[...] = jnp.concatenate([stack, jnp.zeros((LANES - 8, TM), F32)], axis=0).T
    yield

    for part, dst in enumerate((q_ref, k_ref, v_ref)):
        for r0 in range(0, TM, c):
            for h in range(DN_HEADS):
                lo = h * DN_HEAD_DIM
                lanes = slice(part * DN_WIDTH + lo, part * DN_WIDTH + lo + DN_HEAD_DIM)
                xe = qkv_ref[r0:r0 + CONV_HALO + c, lanes]
                xc = convw_ref[CONV_K - 1:CONV_K, lanes] * xe
                for d in range(1, CONV_K):
                    xc = xc + convw_ref[CONV_K - 1 - d:CONV_K - d, lanes] * pltpu.roll(xe, d, axis=0)
                dst[r0:r0 + c, lo:lo + DN_HEAD_DIM] = xc[CONV_HALO:, :]
            yield
    yield from pool
    for r0 in range(0, TM, c):
        sz_ref[r0:r0 + c, :] = _silu(z_ref[r0:r0 + c, :])
        yield


def _in_proj_kernel(tiles_per_batch, n_cast, *refs):
    x_ref, g_ref, wqkv_ref, wba_ref, wuz_ref, convw_ref, gp_ref, wpool_ref, pscale_ref = refs[:9]
    cast_src = refs[9:9 + n_cast]
    q_ref, k_ref, v_ref, grow_ref, gcol_ref, ypool_ref, sz_ref = refs[9 + n_cast:16 + n_cast]
    cast_dst = refs[16 + n_cast:16 + 2 * n_cast]
    hn_ref, qkv_ref, ba_ref, extp_ref, z_ref = refs[16 + 2 * n_cast:]
    t_in_batch = lax.rem(pl.program_id(0), jnp.int32(tiles_per_batch))

    for src, dst in zip(cast_src, cast_dst):
        dst[...] = src[...].astype(dst.dtype)

    @pl.when(t_in_batch == 0)
    def _():
        qkv_ref[0:CONV_HALO, :] = jnp.zeros((CONV_HALO, 3 * DN_WIDTH), F32)
        extp_ref[0:POOL_HALO, :] = jnp.zeros((POOL_HALO, POOL_WIDTH), F32)

    @pl.when(t_in_batch != 0)
    def _():
        qkv_ref[0:CONV_HALO, :] = qkv_ref[TM:TM + CONV_HALO, :]

    matmuls = _in_proj_matmuls(x_ref, g_ref, wqkv_ref, wba_ref, wuz_ref, hn_ref, qkv_ref, ba_ref, extp_ref, z_ref)
    pool = _pool_mixer(t_in_batch * TM, wpool_ref, pscale_ref, ypool_ref, extp_ref)
    prep = _in_proj_prep(qkv_ref, ba_ref, z_ref, convw_ref, gp_ref, q_ref, k_ref, v_ref, grow_ref, gcol_ref,
                         sz_ref, pool)
    for n_prep_segments in PREP_SEGMENTS_AFTER_IN_PROJ_SEGMENT:
        next(matmuls)
        for _ in range(n_prep_segments):
            next(prep)
    done = object()
    assert next(matmuls, done) is done
    for _ in prep:
        pass


def _in_proj(x2, g, w_qkv, w_ba, w_uz, conv_w, gate_params, w_pool, pool_scale, later_weights, seq):
    t = x2.shape[0]
    n_tiles = t // TM
    tiles_per_gdn_tile = TM_GDN // TM
    tile = lambda width: pl.BlockSpec((TM, width), lambda i: (i, 0))
    wide = jax.ShapeDtypeStruct((t, DN_WIDTH), F32)
    assert all(w.shape[0] % (16 * n_tiles) == 0 for w in later_weights)
    cast_specs = [pl.BlockSpec((w.shape[0] // n_tiles, w.shape[1]), lambda i: (i, 0)) for w in later_weights]
    outs = pl.pallas_call(
        functools.partial(_in_proj_kernel, seq // TM, len(later_weights)),
        grid=(n_tiles,),
        in_specs=[
            tile(D_MODEL),
            _const_spec((1, D_MODEL)),
            _const_spec((D_MODEL, 3 * DN_WIDTH)),
            _const_spec((D_MODEL, LANES)),
            _const_spec((D_MODEL, POOL_WIDTH + DN_WIDTH)),
            _const_spec((CONV_K, 3 * DN_WIDTH)),
            _const_spec((8, LANES)),
            _const_spec((len(POOL_WINDOWS), POOL_GROUP_DIM, POOL_GROUP_DIM)),
            _const_spec((1, POOL_WIDTH)),
        ] + cast_specs,
        out_specs=[
            tile(DN_WIDTH), tile(DN_WIDTH), tile(DN_WIDTH),
            pl.BlockSpec((None, 8, TM), lambda i: (i // tiles_per_gdn_tile, 0, i % tiles_per_gdn_tile)),
            tile(LANES), tile(POOL_WIDTH), tile(DN_WIDTH),
        ] + cast_specs,
        out_shape=[
            wide, wide, wide,
            jax.ShapeDtypeStruct((t // TM_GDN, 8, TM_GDN), F32),
            jax.ShapeDtypeStruct((t, LANES), F32),
            jax.ShapeDtypeStruct((t, POOL_WIDTH), BF16),
            wide,
        ] + [jax.ShapeDtypeStruct(w.shape, BF16) for w in later_weights],
        scratch_shapes=[
            pltpu.VMEM((TM, D_MODEL), BF16),
            pltpu.VMEM((CONV_HALO + TM, 3 * DN_WIDTH), F32),
            pltpu.VMEM((TM, LANES), F32),
            pltpu.VMEM((POOL_HALO + TM, POOL_WIDTH), F32),
            pltpu.VMEM((TM, DN_WIDTH), F32),
        ],
        compiler_params=pltpu.CompilerParams(
            dimension_semantics=("arbitrary",), vmem_limit_bytes=VMEM_LIMIT),
        name="in_proj",
    )(x2, g, w_qkv, w_ba, w_uz, conv_w, gate_params, w_pool, pool_scale, *later_weights)
    return (*outs[:7], tuple(outs[7:]))


def _gdn_prep(chunks, q_ref, k_ref, v_ref, grow_ref, gcol_ref, ops):
    c = GDN_CHUNK
    ii = lax.broadcasted_iota(jnp.int32, (c, c), 0)
    jj = lax.broadcasted_iota(jnp.int32, (c, c), 1)
    tril = ii >= jj
    strict = ii > jj

    def col(r0, j):
        return jnp.broadcast_to(gcol_ref[r0:r0 + c, j:j + 1], (c, c))

    for ci in chunks:
        for h in range(DN_HEADS):
            r0 = ci * c
            lo = h * DN_HEAD_DIM
            qh = _silu(q_ref[r0:r0 + c, lo:lo + DN_HEAD_DIM])
            kh = _silu(k_ref[r0:r0 + c, lo:lo + DN_HEAD_DIM])
            vh = _silu(v_ref[r0:r0 + c, lo:lo + DN_HEAD_DIM])
            qn = qh * lax.rsqrt(jnp.sum(qh * qh, axis=-1, keepdims=True) + EPS) * (DN_HEAD_DIM ** -0.5)
            kn = kh * lax.rsqrt(jnp.sum(kh * kh, axis=-1, keepdims=True) + EPS)
            beta_b = col(r0, h)
            gc_b = col(r0, DN_HEADS + h)
            gc_r = grow_ref[DN_HEADS + h:DN_HEADS + h + 1, r0:r0 + c]
            g_last = gcol_ref[r0 + c - 1:r0 + c, DN_HEADS + h:DN_HEADS + h + 1]
            d = jnp.where(tril, jnp.exp(gc_b - gc_r), 0.0)
            eg = jnp.exp(gc_b)
            kb = kn * beta_b
            kt = kn.T
            ops["unit"].append((ci, h))
            ops["dec"].append(d)
            ops["neg_d"].append(jnp.where(strict, -d, 0.0))
            ops["kt16"].append(kt.astype(BF16))
            ops["lhs_kq"].append(jnp.concatenate([kb, qn], axis=0).astype(BF16))
            ops["vbkg"].append(jnp.concatenate([vh * beta_b, kb * eg], axis=1).astype(BF16))
            ops["qg16"].append((qn * eg).astype(BF16))
            ops["kdt16"].append((kt * jnp.exp(g_last - gc_r)).astype(BF16))
            if h % 2 == 1:
                yield


def _gdn_stages(ops, u_ref, wq_ref, ak_ref):
    c = GDN_CHUNK
    ii = lax.broadcasted_iota(jnp.int32, (c, c), 0)
    jj = lax.broadcasted_iota(jnp.int32, (c, c), 1)
    eye = jnp.where(ii == jj, 1.0, 0.0).astype(F32)
    n = len(ops["unit"])
    kq = [_dot(ops["lhs_kq"][i], ops["kt16"][i]) for i in range(n)]
    yield
    xs = [kq[i][:c] * ops["neg_d"][i] for i in range(n)]
    ps = [eye + xs[i] for i in range(n)]
    att16 = [(kq[i][c:] * ops["dec"][i]).astype(BF16) for i in range(n)]
    x16 = [x.astype(BF16) for x in xs]
    x16 = [_dot(x, x).astype(BF16) for x in x16]
    yield
    span = 2
    while 2 * span < c:
        both = [_dot(jnp.concatenate([ps[i].astype(BF16), x16[i]], axis=0), x16[i]) for i in range(n)]
        ps = [ps[i] + both[i][:c] for i in range(n)]
        x16 = [both[i][c:].astype(BF16) for i in range(n)]
        span *= 2
        yield
    ps = [ps[i] + _dot(ps[i].astype(BF16), x16[i]) for i in range(n)]
    yield
    uw = [_dot(ps[i].astype(BF16), ops["vbkg"][i]) for i in range(n)]
    for i, (ci, h) in enumerate(ops["unit"]):
        idx = ci * DN_HEADS + h
        u_ref[idx] = uw[i][:, :DN_HEAD_DIM]
        wq_ref[idx] = jnp.concatenate([uw[i][:, DN_HEAD_DIM:].astype(BF16), ops["qg16"][i]], axis=0)
        ak_ref[idx] = jnp.concatenate([att16[i], ops["kdt16"][i]], axis=0)
    yield


def _gdn_phase2(chunks, gcol_ref, sz_ref, og_ref, state_ref, u_ref, wq_ref, ak_ref, y_ref):
    c = GDN_CHUNK
    heads = range(DN_HEADS)
    for ci in chunks:
        r0 = ci * c
        st = [state_ref[h] for h in heads]
        st16 = [s.astype(BF16) for s in st]
        r1 = [_dot(wq_ref[ci * DN_HEADS + h], st16[h]) for h in heads]
        yield
        vn16 = [(u_ref[ci * DN_HEADS + h] - r1[h][:c]).astype(BF16) for h in heads]
        r2 = [_dot(ak_ref[ci * DN_HEADS + h], vn16[h]) for h in heads]
        yield
        for h in heads:
            lo = h * DN_HEAD_DIM
            g_last = gcol_ref[r0 + c - 1:r0 + c, DN_HEADS + h:DN_HEADS + h + 1]
            state_ref[h] = st[h] * jnp.exp(g_last) + r2[h][c:]
            o = r1[h][c:] + r2[h][:c]
            on = o * lax.rsqrt(jnp.mean(o * o, axis=-1, keepdims=True) + EPS) * og_ref[...]
            y_ref[r0:r0 + c, lo:lo + DN_HEAD_DIM] = (on * sz_ref[r0:r0 + c, lo:lo + DN_HEAD_DIM]).astype(y_ref.dtype)


def _gdn_kernel(q_ref, k_ref, v_ref, grow_ref, gcol_ref, sz_ref, og_ref, y_ref, state_ref, u_ref, wq_ref, ak_ref):
    @pl.when(pl.program_id(1) == 0)
    def _():
        state_ref[...] = jnp.zeros_like(state_ref)

    n_chunks = TM_GDN // GDN_CHUNK
    sweeps = [range(g0, g0 + CHUNKS_PER_SWEEP) for g0 in range(0, n_chunks, CHUNKS_PER_SWEEP)]
    new_ops = lambda: {name: [] for name in ("unit", "dec", "neg_d", "kt16", "lhs_kq", "vbkg", "qg16", "kdt16")}
    prep = lambda chunks, ops: _gdn_prep(chunks, q_ref, k_ref, v_ref, grow_ref, gcol_ref, ops)
    phase2 = lambda chunks: _gdn_phase2(chunks, gcol_ref, sz_ref, og_ref, state_ref, u_ref, wq_ref, ak_ref, y_ref)
    ops = new_ops()
    for _ in prep(sweeps[0], ops):
        pass
    for i, sweep in enumerate(sweeps):
        next_ops = new_ops()
        side = []
        if i + 1 < len(sweeps):
            side.append(prep(sweeps[i + 1], next_ops))
        if i > 0:
            side.append(phase2(sweeps[i - 1]))
        _interleave(_gdn_stages(ops, u_ref, wq_ref, ak_ref), _round_robin(side))
        ops = next_ops
    for _ in phase2(sweeps[-1]):
        pass


def _gdn(q2, k2, v2, grow3, gcol2, sz2, o_norm_g, seq):
    t = q2.shape[0]
    b = t // seq
    tiles_per_batch = seq // TM_GDN
    units = (TM_GDN // GDN_CHUNK) * DN_HEADS
    tile = lambda width: pl.BlockSpec((TM_GDN, width), lambda bi, si: (bi * tiles_per_batch + si, 0))
    return pl.pallas_call(
        _gdn_kernel,
        grid=(b, tiles_per_batch),
        in_specs=[tile(DN_WIDTH), tile(DN_WIDTH), tile(DN_WIDTH),
                  pl.BlockSpec((None, 8, TM_GDN), lambda bi, si: (bi * tiles_per_batch + si, 0, 0)),
                  tile(LANES), tile(DN_WIDTH),
                  _const_spec((1, DN_HEAD_DIM))],
        out_specs=tile(DN_WIDTH),
        out_shape=jax.ShapeDtypeStruct((t, DN_WIDTH), BF16),
        scratch_shapes=[
            pltpu.VMEM((DN_HEADS, DN_HEAD_DIM, DN_HEAD_DIM), F32),
            pltpu.VMEM((units, GDN_CHUNK, DN_HEAD_DIM), F32),
            pltpu.VMEM((units, 2 * GDN_CHUNK, DN_HEAD_DIM), BF16),
            pltpu.VMEM((units, 2 * GDN_CHUNK, DN_HEAD_DIM), BF16),
        ],
        compiler_params=pltpu.CompilerParams(
            dimension_semantics=("arbitrary", "arbitrary"), vmem_limit_bytes=VMEM_LIMIT),
        name="gdn",
    )(q2, k2, v2, grow3, gcol2, sz2, o_norm_g)


def _mem_kv_kernel(m_ref, g_ref, wk_ref, wv_ref, kt_ref, v_ref):
    mn = _rms(m_ref[...], g_ref[...]).astype(BF16)
    kt_ref[...] = _dot(mn, wk_ref[...]).T.astype(kt_ref.dtype)
    v_ref[...] = _dot(mn, wv_ref[...]).astype(v_ref.dtype)


def _mem_kv(mem, g, w_k, w_v):
    b, mem_len, _ = mem.shape
    return pl.pallas_call(
        _mem_kv_kernel,
        grid=(b,),
        in_specs=[
            pl.BlockSpec((None, mem_len, D_MODEL), lambda i: (i, 0, 0)),
            _const_spec((1, D_MODEL)),
            _const_spec((D_MODEL, D_MODEL)),
            _const_spec((D_MODEL, D_MODEL)),
        ],
        out_specs=[
            pl.BlockSpec((None, D_MODEL, mem_len), lambda i: (i, 0, 0)),
            pl.BlockSpec((None, mem_len, D_MODEL), lambda i: (i, 0, 0)),
        ],
        out_shape=[
            jax.ShapeDtypeStruct((b, D_MODEL, mem_len), BF16),
            jax.ShapeDtypeStruct((b, mem_len, D_MODEL), BF16),
        ],
        compiler_params=pltpu.CompilerParams(
            dimension_semantics=("arbitrary",), vmem_limit_bytes=VMEM_LIMIT),
        name="mem_kv",
    )(mem, g, w_k, w_v)


def _post_kernel(x_ref, ypool_ref, ydn_ref, kt_ref, v_ref, wout_ref, gxa_ref, wq_ref, wo_ref, gmlp_ref,
                 wup_ref, wdn_ref, gfin_ref, out_ref):
    y = jnp.concatenate([ypool_ref[...], ydn_ref[...]], axis=-1)
    h1 = x_ref[...] + _dot(y, wout_ref[...])
    q = _dot(_rms(h1, gxa_ref[...]).astype(BF16), wq_ref[...]).astype(BF16)
    head_lanes = [slice(hd * XA_HEAD_DIM, (hd + 1) * XA_HEAD_DIM) for hd in range(XA_HEADS)]
    scores = [_dot(q[:, hl], kt_ref[hl, :]) * (XA_HEAD_DIM ** -0.5) for hl in head_lanes]
    expd = [jnp.exp(s - jnp.max(s, axis=-1, keepdims=True)) for s in scores]
    pv = [_dot(e.astype(BF16), v_ref[:, hl]) for e, hl in zip(expd, head_lanes)]
    att = jnp.concatenate([(o / jnp.sum(e, axis=-1, keepdims=True)).astype(BF16) for e, o in zip(expd, pv)],
                          axis=-1)
    h2 = h1 + _dot(att, wo_ref[...])
    hn2 = _rms(h2, gmlp_ref[...]).astype(BF16)
    acc = h2
    for j in range(D_FF // FF_BLOCK):
        a = jnp.maximum(_dot(hn2, wup_ref[:, j * FF_BLOCK:(j + 1) * FF_BLOCK]), 0.0)
        acc = acc + _dot((a * a).astype(BF16), wdn_ref[j * FF_BLOCK:(j + 1) * FF_BLOCK, :])
    out_ref[...] = _rms(acc, gfin_ref[...])


def _post(x2, y_pool, y_dn, kt3, v3, w_out, g_xa, w_q, w_o, g_mlp, w_up, w_dn, g_fin, seq):
    t = x2.shape[0]
    mem_len = v3.shape[1]
    tiles_per_batch = seq // TM_POST
    tile = pl.BlockSpec((TM_POST, D_MODEL), lambda i: (i, 0))
    half_tile = pl.BlockSpec((TM_POST, D_MODEL // 2), lambda i: (i, 0))
    return pl.pallas_call(
        _post_kernel,
        grid=(t // TM_POST,),
        in_specs=[
            tile, half_tile, half_tile,
            pl.BlockSpec((None, D_MODEL, mem_len), lambda i: (i // tiles_per_batch, 0, 0)),
            pl.BlockSpec((None, mem_len, D_MODEL), lambda i: (i // tiles_per_batch, 0, 0)),
            _const_spec((D_MODEL, D_MODEL)),
            _const_spec((1, D_MODEL)),
            _const_spec((D_MODEL, D_MODEL)),
            _const_spec((D_MODEL, D_MODEL)),
            _const_spec((1, D_MODEL)),
            _const_spec((D_MODEL, D_FF)),
            _const_spec((D_FF, D_MODEL)),
            _const_spec((1, D_MODEL)),
        ],
        out_specs=tile,
        out_shape=jax.ShapeDtypeStruct((t, D_MODEL), F32),
        compiler_params=pltpu.CompilerParams(
            dimension_semantics=("arbitrary",), vmem_limit_bytes=VMEM_LIMIT),
        name="post",
    )(x2, y_pool, y_dn, kt3, v3, w_out, g_xa, w_q, w_o, g_mlp, w_up, w_dn, g_fin)


def kernel(x, mem, norm_mix_g, w_in, w_pool, pool_scale, conv_w, a_log, dt_bias, dn_out_norm_g, w_out,
           norm_xattn_g, mem_norm_g, w_xq, w_xk, w_xv, w_xo, norm_mlp_g, w_up, w_down, final_norm_g):
    b, s, d = x.shape
    assert w_in.shape[0] == 1, "single-layer problem: the final RMSNorm is fused into the layer's post stage"
    row = lambda v: v.reshape(1, -1).astype(F32)

    w_qkv = w_in[0][:, QKV_OFF:QKV_OFF + 3 * DN_WIDTH].astype(BF16)
    w_uz = jnp.concatenate([w_in[0][:, :POOL_WIDTH], w_in[0][:, Z_OFF:Z_OFF + DN_WIDTH]], axis=1).astype(BF16)
    w_ba = jnp.pad(w_in[0][:, MAIN_COLS:], ((0, 0), (0, LANES - 2 * DN_HEADS))).astype(BF16)
    gate_params = jnp.zeros((8, LANES), F32)
    gate_params = gate_params.at[DN_HEADS:, 0].set(a_log[0]).at[DN_HEADS:, 1].set(dt_bias[0])

    x2 = x.reshape(b * s, d)
    later_weights = [w_xk[0], w_xv[0], w_out[0], w_xq[0], w_xo[0], w_up[0], w_down[0]]
    q, k, v, grow, gcol, y_pool, sz, later16 = _in_proj(
        x2, row(norm_mix_g[0]), w_qkv, w_ba, w_uz, conv_w[0].astype(F32), gate_params, w_pool[0].astype(BF16),
        row(pool_scale[0]), later_weights, s)
    w_xk16, w_xv16, w_out16, w_xq16, w_xo16, w_up16, w_down16 = later16
    y_dn = _gdn(q, k, v, grow, gcol, sz, row(dn_out_norm_g[0]), s)
    kt3, v3 = _mem_kv(mem, row(mem_norm_g[0]), w_xk16, w_xv16)
    out = _post(x2, y_pool, y_dn, kt3, v3, w_out16, row(norm_xattn_g[0]),
                w_xq16, w_xo16, row(norm_mlp_g[0]), w_up16, w_down16, row(final_norm_g), s)
    return out.reshape(b, s, d)
```

```python
import functools
import itertools

import jax
import jax.numpy as jnp
from jax import lax
from jax.experimental import pallas as pl
from jax.experimental.pallas import tpu as pltpu

D_MODEL = 1024
POOL_WIDTH = 512
POOL_WINDOWS = (2, 4, 8, 16)
POOL_GROUP_DIM = 128
DN_WIDTH = 512
DN_HEADS = 4
DN_HEAD_DIM = 128
CONV_K = 4
MAIN_COLS = POOL_WIDTH + 4 * DN_WIDTH
QKV_OFF = POOL_WIDTH
Z_OFF = POOL_WIDTH + 3 * DN_WIDTH
XA_HEADS = 4
XA_HEAD_DIM = 256
D_FF = 4096
EPS = 1e-6

LANES = 128
GDN_CHUNK = 128
POOL_HALO = 16
CONV_HALO = 8
TM = 1024
TM_GDN = 2048
TM_POST = 512
CHUNKS_PER_SWEEP = 2
GDN_SLOTS = 2 * CHUNKS_PER_SWEEP * DN_HEADS
FF_BLOCK = 1024
_ROW_BLOCKS = TM // GDN_CHUNK
PREP_SEGMENTS_AFTER_IN_PROJ_SEGMENT = (0, 0, 1 + _ROW_BLOCKS, _ROW_BLOCKS, _ROW_BLOCKS, len(POOL_WINDOWS))
VMEM_LIMIT = 56 * 1024 * 1024

F32 = jnp.float32
BF16 = jnp.bfloat16


def _dot(a, b):
    return jnp.dot(a, b, preferred_element_type=F32)


def _rms(x, g):
    return x * lax.rsqrt(jnp.mean(x * x, axis=-1, keepdims=True) + EPS) * g


def _silu(x):
    return x * jax.nn.sigmoid(x)


def _const_spec(shape):
    zeros = (0,) * len(shape)
    return pl.BlockSpec(shape, lambda *_: zeros, pipeline_mode=pl.Buffered(1))


def _round_robin(generators):
    live = list(generators)
    while live:
        for g in list(live):
            try:
                yield next(g)
            except StopIteration:
                live.remove(g)


def _interleave(first, second, second_per_first=1):
    for _ in first:
        for _ in range(second_per_first):
            next(second, None)
    for _ in second:
        pass


def _pool_mixer(t0, wpool_ref, pscale_ref, y_ref, extp_ref):
    tm = y_ref.shape[0]
    t_glob = lax.broadcasted_iota(jnp.int32, (tm, POOL_GROUP_DIM), 0) + t0
    for gi, win in enumerate(POOL_WINDOWS):
        lo = gi * POOL_GROUP_DIM
        assert win & (win - 1) == 0 and win <= POOL_HALO, "doubling needs power-of-two windows within the halo"
        acc = extp_ref[:, lo:lo + POOL_GROUP_DIM]
        shift = 1
        while shift < win:
            acc = acc + pltpu.roll(acc, shift, axis=0)
            shift *= 2
        acc = acc[POOL_HALO:, :]
        u = extp_ref[POOL_HALO:POOL_HALO + tm, lo:lo + POOL_GROUP_DIM]
        cnt = jnp.minimum(t_glob + 1, win).astype(F32)
        pooled = acc / cnt - u
        mixed = _dot(pooled.astype(BF16), wpool_ref[gi])
        y_ref[:, lo:lo + POOL_GROUP_DIM] = (mixed * pscale_ref[:, lo:lo + POOL_GROUP_DIM]).astype(y_ref.dtype)
        yield
    extp_ref[0:POOL_HALO, :] = extp_ref[tm:tm + POOL_HALO, :]


def _in_proj_matmuls(x_ref, g_ref, wqkv_ref, wba_ref, wuz_ref, hn_ref, qkv_ref, ba_ref, extp_ref, z_ref):
    hn_ref[...] = _rms(x_ref[...], g_ref[...]).astype(BF16)
    ba_ref[...] = _dot(hn_ref[...], wba_ref[...])
    yield
    for part in range(3):
        lanes = slice(part * DN_WIDTH, (part + 1) * DN_WIDTH)
        qkv_ref[CONV_HALO:CONV_HALO + TM, lanes] = _dot(hn_ref[...], wqkv_ref[:, lanes])
        yield
    extp_ref[POOL_HALO:, :] = _dot(hn_ref[...], wuz_ref[:, 0:POOL_WIDTH])
    yield
    z_ref[...] = _dot(hn_ref[...], wuz_ref[:, POOL_WIDTH:POOL_WIDTH + DN_WIDTH])
    yield


def _in_proj_prep(qkv_ref, ba_ref, z_ref, convw_ref, gp_ref, q_ref, k_ref, v_ref, grow_ref, gcol_ref, sz_ref, pool):
    c = GDN_CHUNK

    bat = ba_ref[...].T[0:8, :]
    gp = gp_ref[...]
    a_log = gp[:, 0:1]
    dt_bias = gp[:, 1:2]
    xa = bat + dt_bias
    softplus = jnp.maximum(xa, 0.0) + jnp.log1p(jnp.exp(-jnp.abs(xa)))
    cs = -jnp.exp(a_log) * softplus
    lane = lax.broadcasted_iota(jnp.int32, (8, TM), 1) & (c - 1)
    sh = 1
    while sh < c:
        cs = cs + jnp.where(lane >= sh, pltpu.roll(cs, sh, axis=1), 0.0)
        sh *= 2
    row8 = lax.broadcasted_iota(jnp.int32, (8, TM), 0)
    stack = jnp.where(row8 < DN_HEADS, jax.nn.sigmoid(bat), cs)
    grow_ref[...] = stack
    gcol_ref[...] = jnp.concatenate([stack, jnp.zeros((LANES - 8, TM), F32)], axis=0).T
    yield

    for part, dst in enumerate((q_ref, k_ref, v_ref)):
        for r0 in range(0, TM, c):
            for h in range(DN_HEADS):
                lo = h * DN_HEAD_DIM
                lanes = slice(part * DN_WIDTH + lo, part * DN_WIDTH + lo + DN_HEAD_DIM)
                xe = qkv_ref[r0:r0 + CONV_HALO + c, lanes]
                xc = convw_ref[CONV_K - 1:CONV_K, lanes] * xe
                for d in range(1, CONV_K):
                    xc = xc + convw_ref[CONV_K - 1 - d:CONV_K - d, lanes] * pltpu.roll(xe, d, axis=0)
                dst[r0:r0 + c, lo:lo + DN_HEAD_DIM] = xc[CONV_HALO:, :]
            yield
    yield from pool
    for r0 in range(0, TM, c):
        sz_ref[r0:r0 + c, :] = _silu(z_ref[r0:r0 + c, :])
        yield


def _in_proj_kernel(tiles_per_batch, n_cast, *refs):
    x_ref, g_ref, wqkv_ref, wba_ref, wuz_ref, convw_ref, gp_ref, wpool_ref, pscale_ref = refs[:9]
    cast_src = refs[9:9 + n_cast]
    q_ref, k_ref, v_ref, grow_ref, gcol_ref, ypool_ref, sz_ref = refs[9 + n_cast:16 + n_cast]
    cast_dst = refs[16 + n_cast:16 + 2 * n_cast]
    hn_ref, qkv_ref, ba_ref, extp_ref, z_ref = refs[16 + 2 * n_cast:]
    t_in_batch = lax.rem(pl.program_id(0), jnp.int32(tiles_per_batch))

    for src, dst in zip(cast_src, cast_dst):
        dst[...] = src[...].astype(dst.dtype)

    @pl.when(t_in_batch == 0)
    def _():
        qkv_ref[0:CONV_HALO, :] = jnp.zeros((CONV_HALO, 3 * DN_WIDTH), F32)
        extp_ref[0:POOL_HALO, :] = jnp.zeros((POOL_HALO, POOL_WIDTH), F32)

    @pl.when(t_in_batch != 0)
    def _():
        qkv_ref[0:CONV_HALO, :] = qkv_ref[TM:TM + CONV_HALO, :]

    matmuls = _in_proj_matmuls(x_ref, g_ref, wqkv_ref, wba_ref, wuz_ref, hn_ref, qkv_ref, ba_ref, extp_ref, z_ref)
    pool = _pool_mixer(t_in_batch * TM, wpool_ref, pscale_ref, ypool_ref, extp_ref)
    prep = _in_proj_prep(qkv_ref, ba_ref, z_ref, convw_ref, gp_ref, q_ref, k_ref, v_ref, grow_ref, gcol_ref,
                         sz_ref, pool)
    for n_prep_segments in PREP_SEGMENTS_AFTER_IN_PROJ_SEGMENT:
        next(matmuls)
        for _ in range(n_prep_segments):
            next(prep)
    done = object()
    assert next(matmuls, done) is done
    for _ in prep:
        pass


def _in_proj(x2, g, w_qkv, w_ba, w_uz, conv_w, gate_params, w_pool, pool_scale, later_weights, seq):
    t = x2.shape[0]
    n_tiles = t // TM
    tiles_per_gdn_tile = TM_GDN // TM
    tile = lambda width: pl.BlockSpec((TM, width), lambda i: (i, 0))
    wide = jax.ShapeDtypeStruct((t, DN_WIDTH), F32)
    assert all(w.shape[0] % (16 * n_tiles) == 0 for w in later_weights)
    cast_specs = [pl.BlockSpec((w.shape[0] // n_tiles, w.shape[1]), lambda i: (i, 0)) for w in later_weights]
    outs = pl.pallas_call(
        functools.partial(_in_proj_kernel, seq // TM, len(later_weights)),
        grid=(n_tiles,),
        in_specs=[
            tile(D_MODEL),
            _const_spec((1, D_MODEL)),
            _const_spec((D_MODEL, 3 * DN_WIDTH)),
            _const_spec((D_MODEL, LANES)),
            _const_spec((D_MODEL, POOL_WIDTH + DN_WIDTH)),
            _const_spec((CONV_K, 3 * DN_WIDTH)),
            _const_spec((8, LANES)),
            _const_spec((len(POOL_WINDOWS), POOL_GROUP_DIM, POOL_GROUP_DIM)),
            _const_spec((1, POOL_WIDTH)),
        ] + cast_specs,
        out_specs=[
            tile(DN_WIDTH), tile(DN_WIDTH), tile(DN_WIDTH),
            pl.BlockSpec((None, 8, TM), lambda i: (i // tiles_per_gdn_tile, 0, i % tiles_per_gdn_tile)),
            tile(LANES), tile(POOL_WIDTH), tile(DN_WIDTH),
        ] + cast_specs,
        out_shape=[
            wide, wide, wide,
            jax.ShapeDtypeStruct((t // TM_GDN, 8, TM_GDN), F32),
            jax.ShapeDtypeStruct((t, LANES), F32),
            jax.ShapeDtypeStruct((t, POOL_WIDTH), BF16),
            wide,
        ] + [jax.ShapeDtypeStruct(w.shape, BF16) for w in later_weights],
        scratch_shapes=[
            pltpu.VMEM((TM, D_MODEL), BF16),
            pltpu.VMEM((CONV_HALO + TM, 3 * DN_WIDTH), F32),
            pltpu.VMEM((TM, LANES), F32),
            pltpu.VMEM((POOL_HALO + TM, POOL_WIDTH), F32),
            pltpu.VMEM((TM, DN_WIDTH), F32),
        ],
        compiler_params=pltpu.CompilerParams(
            dimension_semantics=("arbitrary",), vmem_limit_bytes=VMEM_LIMIT),
        name="in_proj",
    )(x2, g, w_qkv, w_ba, w_uz, conv_w, gate_params, w_pool, pool_scale, *later_weights)
    return (*outs[:7], tuple(outs[7:]))


def _slot(chunk, head):
    return (chunk * DN_HEADS + head) % GDN_SLOTS


def _gdn_prep(chunks, q_ref, k_ref, v_ref, grow_ref, gcol_ref, ops):
    c = GDN_CHUNK
    ii = lax.broadcasted_iota(jnp.int32, (c, c), 0)
    jj = lax.broadcasted_iota(jnp.int32, (c, c), 1)
    tril = ii >= jj
    strict = ii > jj

    def col(r0, j):
        return jnp.broadcast_to(gcol_ref[r0:r0 + c, j:j + 1], (c, c))

    for ci in chunks:
        for h in range(DN_HEADS):
            r0 = ci * c
            lo = h * DN_HEAD_DIM
            qh = _silu(q_ref[r0:r0 + c, lo:lo + DN_HEAD_DIM])
            kh = _silu(k_ref[r0:r0 + c, lo:lo + DN_HEAD_DIM])
            vh = _silu(v_ref[r0:r0 + c, lo:lo + DN_HEAD_DIM])
            qn = qh * lax.rsqrt(jnp.sum(qh * qh, axis=-1, keepdims=True) + EPS) * (DN_HEAD_DIM ** -0.5)
            kn = kh * lax.rsqrt(jnp.sum(kh * kh, axis=-1, keepdims=True) + EPS)
            beta_b = col(r0, h)
            gc_b = col(r0, DN_HEADS + h)
            gc_r = grow_ref[DN_HEADS + h:DN_HEADS + h + 1, r0:r0 + c]
            g_last = gcol_ref[r0 + c - 1:r0 + c, DN_HEADS + h:DN_HEADS + h + 1]
            d = jnp.where(tril, jnp.exp(gc_b - gc_r), 0.0)
            eg = jnp.exp(gc_b)
            kb = kn * beta_b
            kt = kn.T
            ops["unit"].append((ci, h))
            ops["dec"].append(d)
            ops["neg_d"].append(jnp.where(strict, -d, 0.0))
            ops["kt16"].append(kt.astype(BF16))
            ops["lhs_kq"].append(jnp.concatenate([kb, qn], axis=0).astype(BF16))
            ops["vbkg"].append(jnp.concatenate([vh * beta_b, kb * eg], axis=1).astype(BF16))
            ops["qg16"].append((qn * eg).astype(BF16))
            ops["kdt16"].append((kt * jnp.exp(g_last - gc_r)).astype(BF16))
            if h % 2 == 1:
                yield


def _gdn_stages(ops, u_ref, wq_ref, ak_ref):
    c = GDN_CHUNK
    ii = lax.broadcasted_iota(jnp.int32, (c, c), 0)
    jj = lax.broadcasted_iota(jnp.int32, (c, c), 1)
    eye = jnp.where(ii == jj, 1.0, 0.0).astype(F32)
    n = len(ops["unit"])
    kq = [_dot(ops["lhs_kq"][i], ops["kt16"][i]) for i in range(n)]
    yield
    xs = [kq[i][:c] * ops["neg_d"][i] for i in range(n)]
    ps = [eye + xs[i] for i in range(n)]
    att16 = [(kq[i][c:] * ops["dec"][i]).astype(BF16) for i in range(n)]
    x16 = [x.astype(BF16) for x in xs]
    x16 = [_dot(x, x).astype(BF16) for x in x16]
    yield
    span = 2
    while 2 * span < c:
        both = [_dot(jnp.concatenate([ps[i].astype(BF16), x16[i]], axis=0), x16[i]) for i in range(n)]
        ps = [ps[i] + both[i][:c] for i in range(n)]
        x16 = [both[i][c:].astype(BF16) for i in range(n)]
        span *= 2
        yield
    ps = [ps[i] + _dot(ps[i].astype(BF16), x16[i]) for i in range(n)]
    yield
    uw = [_dot(ps[i].astype(BF16), ops["vbkg"][i]) for i in range(n)]
    for i, (ci, h) in enumerate(ops["unit"]):
        idx = _slot(ci, h)
        u_ref[idx] = uw[i][:, :DN_HEAD_DIM]
        wq_ref[idx] = jnp.concatenate([uw[i][:, DN_HEAD_DIM:].astype(BF16), ops["qg16"][i]], axis=0)
        ak_ref[idx] = jnp.concatenate([att16[i], ops["kdt16"][i]], axis=0)
    yield


def _gdn_phase2(chunks, gcol_ref, sz_ref, og_ref, state_ref, u_ref, wq_ref, ak_ref, y_ref):
    c = GDN_CHUNK
    heads = range(DN_HEADS)
    for ci in chunks:
        r0 = ci * c
        st = [state_ref[h] for h in heads]
        st16 = [s.astype(BF16) for s in st]
        r1 = [_dot(wq_ref[_slot(ci, h)], st16[h]) for h in heads]
        yield
        vn16 = [(u_ref[_slot(ci, h)] - r1[h][:c]).astype(BF16) for h in heads]
        r2 = [_dot(ak_ref[_slot(ci, h)], vn16[h]) for h in heads]
        yield
        for h in heads:
            lo = h * DN_HEAD_DIM
            g_last = gcol_ref[r0 + c - 1:r0 + c, DN_HEADS + h:DN_HEADS + h + 1]
            state_ref[h] = st[h] * jnp.exp(g_last) + r2[h][c:]
            o = r1[h][c:] + r2[h][:c]
            on = o * lax.rsqrt(jnp.mean(o * o, axis=-1, keepdims=True) + EPS) * og_ref[...]
            y_ref[r0:r0 + c, lo:lo + DN_HEAD_DIM] = (on * sz_ref[r0:r0 + c, lo:lo + DN_HEAD_DIM]).astype(y_ref.dtype)


def _gdn_kernel(q_ref, k_ref, v_ref, grow_ref, gcol_ref, sz_ref, og_ref, y_ref, state_ref, u_ref, wq_ref, ak_ref):
    @pl.when(pl.program_id(1) == 0)
    def _():
        state_ref[...] = jnp.zeros_like(state_ref)

    n_chunks = TM_GDN // GDN_CHUNK
    sweeps = [range(g0, g0 + CHUNKS_PER_SWEEP) for g0 in range(0, n_chunks, CHUNKS_PER_SWEEP)]
    new_ops = lambda: {name: [] for name in ("unit", "dec", "neg_d", "kt16", "lhs_kq", "vbkg", "qg16", "kdt16")}
    prep = lambda chunks, ops: _gdn_prep(chunks, q_ref, k_ref, v_ref, grow_ref, gcol_ref, ops)
    phase2 = lambda chunks: _gdn_phase2(chunks, gcol_ref, sz_ref, og_ref, state_ref, u_ref, wq_ref, ak_ref, y_ref)
    ops = new_ops()
    for _ in prep(sweeps[0], ops):
        pass
    for i, sweep in enumerate(sweeps):
        next_ops = new_ops()
        side = []
        if i + 1 < len(sweeps):
            side.append(prep(sweeps[i + 1], next_ops))
        if i > 0:
            side.append(phase2(sweeps[i - 1]))
        _interleave(_gdn_stages(ops, u_ref, wq_ref, ak_ref), _round_robin(side))
        ops = next_ops
    for _ in phase2(sweeps[-1]):
        pass


def _gdn(q2, k2, v2, grow3, gcol2, sz2, o_norm_g, seq):
    t = q2.shape[0]
    b = t // seq
    tiles_per_batch = seq // TM_GDN
    tile = lambda width: pl.BlockSpec((TM_GDN, width), lambda bi, si: (bi * tiles_per_batch + si, 0))
    return pl.pallas_call(
        _gdn_kernel,
        grid=(b, tiles_per_batch),
        in_specs=[tile(DN_WIDTH), tile(DN_WIDTH), tile(DN_WIDTH),
                  pl.BlockSpec((None, 8, TM_GDN), lambda bi, si: (bi * tiles_per_batch + si, 0, 0)),
                  tile(LANES), tile(DN_WIDTH),
                  _const_spec((1, DN_HEAD_DIM))],
        out_specs=tile(DN_WIDTH),
        out_shape=jax.ShapeDtypeStruct((t, DN_WIDTH), BF16),
        scratch_shapes=[
            pltpu.VMEM((DN_HEADS, DN_HEAD_DIM, DN_HEAD_DIM), F32),
            pltpu.VMEM((GDN_SLOTS, GDN_CHUNK, DN_HEAD_DIM), F32),
            pltpu.VMEM((GDN_SLOTS, 2 * GDN_CHUNK, DN_HEAD_DIM), BF16),
            pltpu.VMEM((GDN_SLOTS, 2 * GDN_CHUNK, DN_HEAD_DIM), BF16),
        ],
        compiler_params=pltpu.CompilerParams(
            dimension_semantics=("arbitrary", "arbitrary"), vmem_limit_bytes=VMEM_LIMIT),
        name="gdn",
    )(q2, k2, v2, grow3, gcol2, sz2, o_norm_g)


def _mem_kv_kernel(m_ref, g_ref, wk_ref, wv_ref, kt_ref, v_ref):
    mn = _rms(m_ref[...], g_ref[...]).astype(BF16)
    kt_ref[...] = _dot(mn, wk_ref[...]).T.astype(kt_ref.dtype)
    v_ref[...] = _dot(mn, wv_ref[...]).astype(v_ref.dtype)


def _mem_kv(mem, g, w_k, w_v):
    b, mem_len, _ = mem.shape
    return pl.pallas_call(
        _mem_kv_kernel,
        grid=(b,),
        in_specs=[
            pl.BlockSpec((None, mem_len, D_MODEL), lambda i: (i, 0, 0)),
            _const_spec((1, D_MODEL)),
            _const_spec((D_MODEL, D_MODEL)),
            _const_spec((D_MODEL, D_MODEL)),
        ],
        out_specs=[
            pl.BlockSpec((None, D_MODEL, mem_len), lambda i: (i, 0, 0)),
            pl.BlockSpec((None, mem_len, D_MODEL), lambda i: (i, 0, 0)),
        ],
        out_shape=[
            jax.ShapeDtypeStruct((b, D_MODEL, mem_len), BF16),
            jax.ShapeDtypeStruct((b, mem_len, D_MODEL), BF16),
        ],
        compiler_params=pltpu.CompilerParams(
            dimension_semantics=("arbitrary",), vmem_limit_bytes=VMEM_LIMIT),
        name="mem_kv",
    )(mem, g, w_k, w_v)


def _post_kernel(x_ref, ypool_ref, ydn_ref, kt_ref, v_ref, wout_ref, gxa_ref, wq_ref, wo_ref, gmlp_ref,
                 wup_ref, wdn_ref, gfin_ref, out_ref):
    y = jnp.concatenate([ypool_ref[...], ydn_ref[...]], axis=-1)
    h1 = x_ref[...] + _dot(y, wout_ref[...])
    q = _dot(_rms(h1, gxa_ref[...]).astype(BF16), wq_ref[...]).astype(BF16)
    head_lanes = [slice(hd * XA_HEAD_DIM, (hd + 1) * XA_HEAD_DIM) for hd in range(XA_HEADS)]
    scores = [_dot(q[:, hl], kt_ref[hl, :]) * (XA_HEAD_DIM ** -0.5) for hl in head_lanes]
    expd = [jnp.exp(s - jnp.max(s, axis=-1, keepdims=True)) for s in scores]
    pv = [_dot(e.astype(BF16), v_ref[:, hl]) for e, hl in zip(expd, head_lanes)]
    att = jnp.concatenate([(o / jnp.sum(e, axis=-1, keepdims=True)).astype(BF16) for e, o in zip(expd, pv)],
                          axis=-1)
    h2 = h1 + _dot(att, wo_ref[...])
    hn2 = _rms(h2, gmlp_ref[...]).astype(BF16)
    acc = h2
    for j in range(D_FF // FF_BLOCK):
        a = jnp.maximum(_dot(hn2, wup_ref[:, j * FF_BLOCK:(j + 1) * FF_BLOCK]), 0.0)
        acc = acc + _dot((a * a).astype(BF16), wdn_ref[j * FF_BLOCK:(j + 1) * FF_BLOCK, :])
    out_ref[...] = _rms(acc, gfin_ref[...])


def _post(x2, y_pool, y_dn, kt3, v3, w_out, g_xa, w_q, w_o, g_mlp, w_up, w_dn, g_fin, seq):
    t = x2.shape[0]
    mem_len = v3.shape[1]
    tiles_per_batch = seq // TM_POST
    tile = pl.BlockSpec((TM_POST, D_MODEL), lambda i: (i, 0))
    half_tile = pl.BlockSpec((TM_POST, D_MODEL // 2), lambda i: (i, 0))
    return pl.pallas_call(
        _post_kernel,
        grid=(t // TM_POST,),
        in_specs=[
            tile, half_tile, half_tile,
            pl.BlockSpec((None, D_MODEL, mem_len), lambda i: (i // tiles_per_batch, 0, 0)),
            pl.BlockSpec((None, mem_len, D_MODEL), lambda i: (i // tiles_per_batch, 0, 0)),
            _const_spec((D_MODEL, D_MODEL)),
            _const_spec((1, D_MODEL)),
            _const_spec((D_MODEL, D_MODEL)),
            _const_spec((D_MODEL, D_MODEL)),
            _const_spec((1, D_MODEL)),
            _const_spec((D_MODEL, D_FF)),
            _const_spec((D_FF, D_MODEL)),
            _const_spec((1, D_MODEL)),
        ],
        out_specs=tile,
        out_shape=jax.ShapeDtypeStruct((t, D_MODEL), F32),
        compiler_params=pltpu.CompilerParams(
            dimension_semantics=("arbitrary",), vmem_limit_bytes=VMEM_LIMIT),
        name="post",
    )(x2, y_pool, y_dn, kt3, v3, w_out, g_xa, w_q, w_o, g_mlp, w_up, w_dn, g_fin)


def kernel(x, mem, norm_mix_g, w_in, w_pool, pool_scale, conv_w, a_log, dt_bias, dn_out_norm_g, w_out,
           norm_xattn_g, mem_norm_g, w_xq, w_xk, w_xv, w_xo, norm_mlp_g, w_up, w_down, final_norm_g):
    b, s, d = x.shape
    assert w_in.shape[0] == 1, "single-layer problem: the final RMSNorm is fused into the layer's post stage"
    row = lambda v: v.reshape(1, -1).astype(F32)

    w_qkv = w_in[0][:, QKV_OFF:QKV_OFF + 3 * DN_WIDTH].astype(BF16)
    w_uz = jnp.concatenate([w_in[0][:, :POOL_WIDTH], w_in[0][:, Z_OFF:Z_OFF + DN_WIDTH]], axis=1).astype(BF16)
    w_ba = jnp.pad(w_in[0][:, MAIN_COLS:], ((0, 0), (0, LANES - 2 * DN_HEADS))).astype(BF16)
    gate_params = jnp.zeros((8, LANES), F32)
    gate_params = gate_params.at[DN_HEADS:, 0].set(a_log[0]).at[DN_HEADS:, 1].set(dt_bias[0])

    x2 = x.reshape(b * s, d)
    later_weights = [w_xk[0], w_xv[0], w_out[0], w_xq[0], w_xo[0], w_up[0], w_down[0]]
    q, k, v, grow, gcol, y_pool, sz, later16 = _in_proj(
        x2, row(norm_mix_g[0]), w_qkv, w_ba, w_uz, conv_w[0].astype(F32), gate_params, w_pool[0].astype(BF16),
        row(pool_scale[0]), later_weights, s)
    w_xk16, w_xv16, w_out16, w_xq16, w_xo16, w_up16, w_down16 = later16
    y_dn = _gdn(q, k, v, grow, gcol, sz, row(dn_out_norm_g[0]), s)
    kt3, v3 = _mem_kv(mem, row(mem_norm_g[0]), w_xk16, w_xv16)
    out = _post(x2, y_pool, y_dn, kt3, v3, w_out16, row(norm_xattn_g[0]),
                w_xq16, w_xo16, row(norm_mlp_g[0]), w_up16, w_down16, row(final_norm_g), s)
    return out.reshape(b, s, d)
```

```python
import functools
import itertools

import jax
import jax.numpy as jnp
from jax import lax
from jax.experimental import pallas as pl
from jax.experimental.pallas import tpu as pltpu

D_MODEL = 1024
POOL_WIDTH = 512
POOL_WINDOWS = (2, 4, 8, 16)
POOL_GROUP_DIM = 128
DN_WIDTH = 512
DN_HEADS = 4
DN_HEAD_DIM = 128
CONV_K = 4
MAIN_COLS = POOL_WIDTH + 4 * DN_WIDTH
QKV_OFF = POOL_WIDTH
Z_OFF = POOL_WIDTH + 3 * DN_WIDTH
XA_HEADS = 4
XA_HEAD_DIM = 256
D_FF = 4096
EPS = 1e-6

LANES = 128
GDN_CHUNK = 128
POOL_HALO = 16
CONV_HALO = 8
TM = 1024
TM_GDN = 2048
TM_POST = 1024
POST_ROW_BLOCK = 512
CHUNKS_PER_SWEEP = 2
GDN_SLOTS = 2 * CHUNKS_PER_SWEEP * DN_HEADS
FF_BLOCK = 1024
_ROW_BLOCKS = TM // GDN_CHUNK
PREP_SEGMENTS_AFTER_IN_PROJ_SEGMENT = (0, 0, 1 + _ROW_BLOCKS, _ROW_BLOCKS, _ROW_BLOCKS, len(POOL_WINDOWS))
VMEM_LIMIT = 60 * 1024 * 1024

F32 = jnp.float32
BF16 = jnp.bfloat16


def _dot(a, b):
    return jnp.dot(a, b, preferred_element_type=F32)


def _rms(x, g):
    return x * lax.rsqrt(jnp.mean(x * x, axis=-1, keepdims=True) + EPS) * g


def _silu(x):
    return x * jax.nn.sigmoid(x)


def _const_spec(shape):
    zeros = (0,) * len(shape)
    return pl.BlockSpec(shape, lambda *_: zeros, pipeline_mode=pl.Buffered(1))


def _round_robin(generators):
    live = list(generators)
    while live:
        for g in list(live):
            try:
                yield next(g)
            except StopIteration:
                live.remove(g)


def _interleave(first, second, second_per_first=1):
    for _ in first:
        for _ in range(second_per_first):
            next(second, None)
    for _ in second:
        pass


def _pool_mixer(t0, wpool_ref, pscale_ref, y_ref, extp_ref):
    tm = y_ref.shape[0]
    t_glob = lax.broadcasted_iota(jnp.int32, (tm, POOL_GROUP_DIM), 0) + t0
    for gi, win in enumerate(POOL_WINDOWS):
        lo = gi * POOL_GROUP_DIM
        assert win & (win - 1) == 0 and win <= POOL_HALO, "doubling needs power-of-two windows within the halo"
        acc = extp_ref[:, lo:lo + POOL_GROUP_DIM]
        shift = 1
        while shift < win:
            acc = acc + pltpu.roll(acc, shift, axis=0)
            shift *= 2
        acc = acc[POOL_HALO:, :]
        u = extp_ref[POOL_HALO:POOL_HALO + tm, lo:lo + POOL_GROUP_DIM]
        cnt = jnp.minimum(t_glob + 1, win).astype(F32)
        pooled = acc / cnt - u
        mixed = _dot(pooled.astype(BF16), wpool_ref[gi])
        y_ref[:, lo:lo + POOL_GROUP_DIM] = (mixed * pscale_ref[:, lo:lo + POOL_GROUP_DIM]).astype(y_ref.dtype)
        yield
    extp_ref[0:POOL_HALO, :] = extp_ref[tm:tm + POOL_HALO, :]


def _in_proj_matmuls(x_ref, g_ref, wqkv_ref, wba_ref, wuz_ref, hn_ref, qkv_ref, ba_ref, extp_ref, z_ref):
    hn_ref[...] = _rms(x_ref[...], g_ref[...]).astype(BF16)
    ba_ref[...] = _dot(hn_ref[...], wba_ref[...])
    yield
    for part in range(3):
        lanes = slice(part * DN_WIDTH, (part + 1) * DN_WIDTH)
        qkv_ref[CONV_HALO:CONV_HALO + TM, lanes] = _dot(hn_ref[...], wqkv_ref[:, lanes])
        yield
    extp_ref[POOL_HALO:, :] = _dot(hn_ref[...], wuz_ref[:, 0:POOL_WIDTH])
    yield
    z_ref[...] = _dot(hn_ref[...], wuz_ref[:, POOL_WIDTH:POOL_WIDTH + DN_WIDTH])
    yield


def _in_proj_prep(qkv_ref, ba_ref, z_ref, convw_ref, gp_ref, q_ref, k_ref, v_ref, grow_ref, gcol_ref, sz_ref, pool):
    c = GDN_CHUNK

    bat = ba_ref[...].T[0:8, :]
    gp = gp_ref[...]
    a_log = gp[:, 0:1]
    dt_bias = gp[:, 1:2]
    xa = bat + dt_bias
    softplus = jnp.maximum(xa, 0.0) + jnp.log1p(jnp.exp(-jnp.abs(xa)))
    cs = -jnp.exp(a_log) * softplus
    lane = lax.broadcasted_iota(jnp.int32, (8, TM), 1) & (c - 1)
    sh = 1
    while sh < c:
        cs = cs + jnp.where(lane >= sh, pltpu.roll(cs, sh, axis=1), 0.0)
        sh *= 2
    row8 = lax.broadcasted_iota(jnp.int32, (8, TM), 0)
    stack = jnp.where(row8 < DN_HEADS, jax.nn.sigmoid(bat), cs)
    grow_ref[...] = stack
    gcol_ref[...] = jnp.concatenate([stack, jnp.zeros((LANES - 8, TM), F32)], axis=0).T
    yield

    for part, dst in enumerate((q_ref, k_ref, v_ref)):
        for r0 in range(0, TM, c):
            for h in range(DN_HEADS):
                lo = h * DN_HEAD_DIM
                lanes = slice(part * DN_WIDTH + lo, part * DN_WIDTH + lo + DN_HEAD_DIM)
                xe = qkv_ref[r0:r0 + CONV_HALO + c, lanes]
                xc = convw_ref[CONV_K - 1:CONV_K, lanes] * xe
                for d in range(1, CONV_K):
                    xc = xc + convw_ref[CONV_K - 1 - d:CONV_K - d, lanes] * pltpu.roll(xe, d, axis=0)
                dst[r0:r0 + c, lo:lo + DN_HEAD_DIM] = xc[CONV_HALO:, :]
            yield
    yield from pool
    for r0 in range(0, TM, c):
        sz_ref[r0:r0 + c, :] = _silu(z_ref[r0:r0 + c, :])
        yield


def _in_proj_kernel(tiles_per_batch, n_cast, *refs):
    x_ref, g_ref, wqkv_ref, wba_ref, wuz_ref, convw_ref, gp_ref, wpool_ref, pscale_ref = refs[:9]
    cast_src = refs[9:9 + n_cast]
    q_ref, k_ref, v_ref, grow_ref, gcol_ref, ypool_ref, sz_ref = refs[9 + n_cast:16 + n_cast]
    cast_dst = refs[16 + n_cast:16 + 2 * n_cast]
    hn_ref, qkv_ref, ba_ref, extp_ref, z_ref = refs[16 + 2 * n_cast:]
    t_in_batch = lax.rem(pl.program_id(0), jnp.int32(tiles_per_batch))

    for src, dst in zip(cast_src, cast_dst):
        dst[...] = src[...].astype(dst.dtype)

    @pl.when(t_in_batch == 0)
    def _():
        qkv_ref[0:CONV_HALO, :] = jnp.zeros((CONV_HALO, 3 * DN_WIDTH), F32)
        extp_ref[0:POOL_HALO, :] = jnp.zeros((POOL_HALO, POOL_WIDTH), F32)

    @pl.when(t_in_batch != 0)
    def _():
        qkv_ref[0:CONV_HALO, :] = qkv_ref[TM:TM + CONV_HALO, :]

    matmuls = _in_proj_matmuls(x_ref, g_ref, wqkv_ref, wba_ref, wuz_ref, hn_ref, qkv_ref, ba_ref, extp_ref, z_ref)
    pool = _pool_mixer(t_in_batch * TM, wpool_ref, pscale_ref, ypool_ref, extp_ref)
    prep = _in_proj_prep(qkv_ref, ba_ref, z_ref, convw_ref, gp_ref, q_ref, k_ref, v_ref, grow_ref, gcol_ref,
                         sz_ref, pool)
    for n_prep_segments in PREP_SEGMENTS_AFTER_IN_PROJ_SEGMENT:
        next(matmuls)
        for _ in range(n_prep_segments):
            next(prep)
    done = object()
    assert next(matmuls, done) is done
    for _ in prep:
        pass


def _in_proj(x2, g, w_qkv, w_ba, w_uz, conv_w, gate_params, w_pool, pool_scale, later_weights, seq):
    t = x2.shape[0]
    n_tiles = t // TM
    tiles_per_gdn_tile = TM_GDN // TM
    tile = lambda width: pl.BlockSpec((TM, width), lambda i: (i, 0))
    wide = jax.ShapeDtypeStruct((t, DN_WIDTH), F32)
    assert all(w.shape[0] % (16 * n_tiles) == 0 for w in later_weights)
    cast_specs = [pl.BlockSpec((w.shape[0] // n_tiles, w.shape[1]), lambda i: (i, 0)) for w in later_weights]
    outs = pl.pallas_call(
        functools.partial(_in_proj_kernel, seq // TM, len(later_weights)),
        grid=(n_tiles,),
        in_specs=[
            tile(D_MODEL),
            _const_spec((1, D_MODEL)),
            _const_spec((D_MODEL, 3 * DN_WIDTH)),
            _const_spec((D_MODEL, LANES)),
            _const_spec((D_MODEL, POOL_WIDTH + DN_WIDTH)),
            _const_spec((CONV_K, 3 * DN_WIDTH)),
            _const_spec((8, LANES)),
            _const_spec((len(POOL_WINDOWS), POOL_GROUP_DIM, POOL_GROUP_DIM)),
            _const_spec((1, POOL_WIDTH)),
        ] + cast_specs,
        out_specs=[
            tile(DN_WIDTH), tile(DN_WIDTH), tile(DN_WIDTH),
            pl.BlockSpec((None, 8, TM), lambda i: (i // tiles_per_gdn_tile, 0, i % tiles_per_gdn_tile)),
            tile(LANES), tile(POOL_WIDTH), tile(DN_WIDTH),
        ] + cast_specs,
        out_shape=[
            wide, wide, wide,
            jax.ShapeDtypeStruct((t // TM_GDN, 8, TM_GDN), F32),
            jax.ShapeDtypeStruct((t, LANES), F32),
            jax.ShapeDtypeStruct((t, POOL_WIDTH), BF16),
            wide,
        ] + [jax.ShapeDtypeStruct(w.shape, BF16) for w in later_weights],
        scratch_shapes=[
            pltpu.VMEM((TM, D_MODEL), BF16),
            pltpu.VMEM((CONV_HALO + TM, 3 * DN_WIDTH), F32),
            pltpu.VMEM((TM, LANES), F32),
            pltpu.VMEM((POOL_HALO + TM, POOL_WIDTH), F32),
            pltpu.VMEM((TM, DN_WIDTH), F32),
        ],
        compiler_params=pltpu.CompilerParams(
            dimension_semantics=("arbitrary",), vmem_limit_bytes=VMEM_LIMIT),
        name="in_proj",
    )(x2, g, w_qkv, w_ba, w_uz, conv_w, gate_params, w_pool, pool_scale, *later_weights)
    return (*outs[:7], tuple(outs[7:]))


def _slot(chunk, head):
    return (chunk * DN_HEADS + head) % GDN_SLOTS


def _gdn_prep(chunks, q_ref, k_ref, v_ref, grow_ref, gcol_ref, ops):
    c = GDN_CHUNK
    ii = lax.broadcasted_iota(jnp.int32, (c, c), 0)
    jj = lax.broadcasted_iota(jnp.int32, (c, c), 1)
    tril = ii >= jj
    strict = ii > jj

    def col(r0, j):
        return jnp.broadcast_to(gcol_ref[r0:r0 + c, j:j + 1], (c, c))

    for ci in chunks:
        for h in range(DN_HEADS):
            r0 = ci * c
            lo = h * DN_HEAD_DIM
            qh = _silu(q_ref[r0:r0 + c, lo:lo + DN_HEAD_DIM])
            kh = _silu(k_ref[r0:r0 + c, lo:lo + DN_HEAD_DIM])
            vh = _silu(v_ref[r0:r0 + c, lo:lo + DN_HEAD_DIM])
            qn = qh * lax.rsqrt(jnp.sum(qh * qh, axis=-1, keepdims=True) + EPS) * (DN_HEAD_DIM ** -0.5)
            kn = kh * lax.rsqrt(jnp.sum(kh * kh, axis=-1, keepdims=True) + EPS)
            beta_b = col(r0, h)
            gc_b = col(r0, DN_HEADS + h)
            gc_r = grow_ref[DN_HEADS + h:DN_HEADS + h + 1, r0:r0 + c]
            g_last = gcol_ref[r0 + c - 1:r0 + c, DN_HEADS + h:DN_HEADS + h + 1]
            d = jnp.where(tril, jnp.exp(gc_b - gc_r), 0.0)
            eg = jnp.exp(gc_b)
            kb = kn * beta_b
            kt = kn.T
            ops["unit"].append((ci, h))
            ops["dec"].append(d)
            ops["neg_d"].append(jnp.where(strict, -d, 0.0))
            ops["kt16"].append(kt.astype(BF16))
            ops["lhs_kq"].append(jnp.concatenate([kb, qn], axis=0).astype(BF16))
            ops["vbkg"].append(jnp.concatenate([vh * beta_b, kb * eg], axis=1).astype(BF16))
            ops["qg16"].append((qn * eg).astype(BF16))
            ops["kdt16"].append((kt * jnp.exp(g_last - gc_r)).astype(BF16))
            if h % 2 == 1:
                yield


def _gdn_stages(ops, u_ref, wq_ref, ak_ref):
    c = GDN_CHUNK
    ii = lax.broadcasted_iota(jnp.int32, (c, c), 0)
    jj = lax.broadcasted_iota(jnp.int32, (c, c), 1)
    eye = jnp.where(ii == jj, 1.0, 0.0).astype(F32)
    n = len(ops["unit"])
    kq = [_dot(ops["lhs_kq"][i], ops["kt16"][i]) for i in range(n)]
    yield
    xs = [kq[i][:c] * ops["neg_d"][i] for i in range(n)]
    ps = [eye + xs[i] for i in range(n)]
    att16 = [(kq[i][c:] * ops["dec"][i]).astype(BF16) for i in range(n)]
    x16 = [x.astype(BF16) for x in xs]
    x16 = [_dot(x, x).astype(BF16) for x in x16]
    yield
    span = 2
    while 2 * span < c:
        both = [_dot(jnp.concatenate([ps[i].astype(BF16), x16[i]], axis=0), x16[i]) for i in range(n)]
        ps = [ps[i] + both[i][:c] for i in range(n)]
        x16 = [both[i][c:].astype(BF16) for i in range(n)]
        span *= 2
        yield
    ps = [ps[i] + _dot(ps[i].astype(BF16), x16[i]) for i in range(n)]
    yield
    uw = [_dot(ps[i].astype(BF16), ops["vbkg"][i]) for i in range(n)]
    for i, (ci, h) in enumerate(ops["unit"]):
        idx = _slot(ci, h)
        u_ref[idx] = uw[i][:, :DN_HEAD_DIM]
        wq_ref[idx] = jnp.concatenate([uw[i][:, DN_HEAD_DIM:].astype(BF16), ops["qg16"][i]], axis=0)
        ak_ref[idx] = jnp.concatenate([att16[i], ops["kdt16"][i]], axis=0)
    yield


def _gdn_phase2(chunks, gcol_ref, sz_ref, og_ref, state_ref, u_ref, wq_ref, ak_ref, y_ref):
    c = GDN_CHUNK
    heads = range(DN_HEADS)
    for ci in chunks:
        r0 = ci * c
        st = [state_ref[h] for h in heads]
        st16 = [s.astype(BF16) for s in st]
        r1 = [_dot(wq_ref[_slot(ci, h)], st16[h]) for h in heads]
        yield
        vn16 = [(u_ref[_slot(ci, h)] - r1[h][:c]).astype(BF16) for h in heads]
        r2 = [_dot(ak_ref[_slot(ci, h)], vn16[h]) for h in heads]
        yield
        for h in heads:
            lo = h * DN_HEAD_DIM
            g_last = gcol_ref[r0 + c - 1:r0 + c, DN_HEADS + h:DN_HEADS + h + 1]
            state_ref[h] = st[h] * jnp.exp(g_last) + r2[h][c:]
            o = r1[h][c:] + r2[h][:c]
            on = o * lax.rsqrt(jnp.mean(o * o, axis=-1, keepdims=True) + EPS) * og_ref[...]
            y_ref[r0:r0 + c, lo:lo + DN_HEAD_DIM] = (on * sz_ref[r0:r0 + c, lo:lo + DN_HEAD_DIM]).astype(y_ref.dtype)


def _gdn_kernel(q_ref, k_ref, v_ref, grow_ref, gcol_ref, sz_ref, og_ref, y_ref, state_ref, u_ref, wq_ref, ak_ref):
    @pl.when(pl.program_id(1) == 0)
    def _():
        state_ref[...] = jnp.zeros_like(state_ref)

    n_chunks = TM_GDN // GDN_CHUNK
    sweeps = [range(g0, g0 + CHUNKS_PER_SWEEP) for g0 in range(0, n_chunks, CHUNKS_PER_SWEEP)]
    new_ops = lambda: {name: [] for name in ("unit", "dec", "neg_d", "kt16", "lhs_kq", "vbkg", "qg16", "kdt16")}
    prep = lambda chunks, ops: _gdn_prep(chunks, q_ref, k_ref, v_ref, grow_ref, gcol_ref, ops)
    phase2 = lambda chunks: _gdn_phase2(chunks, gcol_ref, sz_ref, og_ref, state_ref, u_ref, wq_ref, ak_ref, y_ref)
    ops = new_ops()
    for _ in prep(sweeps[0], ops):
        pass
    for i, sweep in enumerate(sweeps):
        next_ops = new_ops()
        side = []
        if i + 1 < len(sweeps):
            side.append(prep(sweeps[i + 1], next_ops))
        if i > 0:
            side.append(phase2(sweeps[i - 1]))
        _interleave(_gdn_stages(ops, u_ref, wq_ref, ak_ref), _round_robin(side))
        ops = next_ops
    for _ in phase2(sweeps[-1]):
        pass


def _gdn(q2, k2, v2, grow3, gcol2, sz2, o_norm_g, seq):
    t = q2.shape[0]
    b = t // seq
    tiles_per_batch = seq // TM_GDN
    tile = lambda width: pl.BlockSpec((TM_GDN, width), lambda bi, si: (bi * tiles_per_batch + si, 0))
    return pl.pallas_call(
        _gdn_kernel,
        grid=(b, tiles_per_batch),
        in_specs=[tile(DN_WIDTH), tile(DN_WIDTH), tile(DN_WIDTH),
                  pl.BlockSpec((None, 8, TM_GDN), lambda bi, si: (bi * tiles_per_batch + si, 0, 0)),
                  tile(LANES), tile(DN_WIDTH),
                  _const_spec((1, DN_HEAD_DIM))],
        out_specs=tile(DN_WIDTH),
        out_shape=jax.ShapeDtypeStruct((t, DN_WIDTH), BF16),
        scratch_shapes=[
            pltpu.VMEM((DN_HEADS, DN_HEAD_DIM, DN_HEAD_DIM), F32),
            pltpu.VMEM((GDN_SLOTS, GDN_CHUNK, DN_HEAD_DIM), F32),
            pltpu.VMEM((GDN_SLOTS, 2 * GDN_CHUNK, DN_HEAD_DIM), BF16),
            pltpu.VMEM((GDN_SLOTS, 2 * GDN_CHUNK, DN_HEAD_DIM), BF16),
        ],
        compiler_params=pltpu.CompilerParams(
            dimension_semantics=("arbitrary", "arbitrary"), vmem_limit_bytes=VMEM_LIMIT),
        name="gdn",
    )(q2, k2, v2, grow3, gcol2, sz2, o_norm_g)


def _mem_kv_kernel(m_ref, g_ref, wk_ref, wv_ref, kt_ref, v_ref):
    mn = _rms(m_ref[...], g_ref[...]).astype(BF16)
    kt_ref[...] = _dot(mn, wk_ref[...]).T.astype(kt_ref.dtype)
    v_ref[...] = _dot(mn, wv_ref[...]).astype(v_ref.dtype)


def _mem_kv(mem, g, w_k, w_v):
    b, mem_len, _ = mem.shape
    return pl.pallas_call(
        _mem_kv_kernel,
        grid=(b,),
        in_specs=[
            pl.BlockSpec((None, mem_len, D_MODEL), lambda i: (i, 0, 0)),
            _const_spec((1, D_MODEL)),
            _const_spec((D_MODEL, D_MODEL)),
            _const_spec((D_MODEL, D_MODEL)),
        ],
        out_specs=[
            pl.BlockSpec((None, D_MODEL, mem_len), lambda i: (i, 0, 0)),
            pl.BlockSpec((None, mem_len, D_MODEL), lambda i: (i, 0, 0)),
        ],
        out_shape=[
            jax.ShapeDtypeStruct((b, D_MODEL, mem_len), BF16),
            jax.ShapeDtypeStruct((b, mem_len, D_MODEL), BF16),
        ],
        compiler_params=pltpu.CompilerParams(
            dimension_semantics=("arbitrary",), vmem_limit_bytes=VMEM_LIMIT),
        name="mem_kv",
    )(mem, g, w_k, w_v)


def _post_rows(rows, x_ref, ypool_ref, ydn_ref, kt_ref, v_ref, wout_ref, gxa_ref, wq_ref, wo_ref, gmlp_ref,
               wup_ref, wdn_ref, gfin_ref, out_ref):
    y = jnp.concatenate([ypool_ref[rows, :], ydn_ref[rows, :]], axis=-1)
    h1 = x_ref[rows, :] + _dot(y, wout_ref[...])
    yield
    q = _dot(_rms(h1, gxa_ref[...]).astype(BF16), wq_ref[...]).astype(BF16)
    yield
    head_lanes = [slice(hd * XA_HEAD_DIM, (hd + 1) * XA_HEAD_DIM) for hd in range(XA_HEADS)]
    scores = [_dot(q[:, hl], kt_ref[hl, :]) * (XA_HEAD_DIM ** -0.5) for hl in head_lanes]
    yield
    expd = [jnp.exp(s - jnp.max(s, axis=-1, keepdims=True)) for s in scores]
    pv = [_dot(e.astype(BF16), v_ref[:, hl]) for e, hl in zip(expd, head_lanes)]
    yield
    att = jnp.concatenate([(o / jnp.sum(e, axis=-1, keepdims=True)).astype(BF16) for e, o in zip(expd, pv)],
                          axis=-1)
    h2 = h1 + _dot(att, wo_ref[...])
    yield
    hn2 = _rms(h2, gmlp_ref[...]).astype(BF16)
    acc = h2
    for j in range(D_FF // FF_BLOCK):
        a = jnp.maximum(_dot(hn2, wup_ref[:, j * FF_BLOCK:(j + 1) * FF_BLOCK]), 0.0)
        yield
        acc = acc + _dot((a * a).astype(BF16), wdn_ref[j * FF_BLOCK:(j + 1) * FF_BLOCK, :])
        yield
    out_ref[rows, :] = _rms(acc, gfin_ref[...])


def _post_kernel(*refs):
    blocks = [slice(r0, r0 + POST_ROW_BLOCK) for r0 in range(0, TM_POST, POST_ROW_BLOCK)]
    for _ in _round_robin([_post_rows(rows, *refs) for rows in blocks]):
        pass


def _post(x2, y_pool, y_dn, kt3, v3, w_out, g_xa, w_q, w_o, g_mlp, w_up, w_dn, g_fin, seq):
    t = x2.shape[0]
    mem_len = v3.shape[1]
    tiles_per_batch = seq // TM_POST
    tile = pl.BlockSpec((TM_POST, D_MODEL), lambda i: (i, 0))
    half_tile = pl.BlockSpec((TM_POST, D_MODEL // 2), lambda i: (i, 0))
    return pl.pallas_call(
        _post_kernel,
        grid=(t // TM_POST,),
        in_specs=[
            tile, half_tile, half_tile,
            pl.BlockSpec((None, D_MODEL, mem_len), lambda i: (i // tiles_per_batch, 0, 0)),
            pl.BlockSpec((None, mem_len, D_MODEL), lambda i: (i // tiles_per_batch, 0, 0)),
            _const_spec((D_MODEL, D_MODEL)),
            _const_spec((1, D_MODEL)),
            _const_spec((D_MODEL, D_MODEL)),
            _const_spec((D_MODEL, D_MODEL)),
            _const_spec((1, D_MODEL)),
            _const_spec((D_MODEL, D_FF)),
            _const_spec((D_FF, D_MODEL)),
            _const_spec((1, D_MODEL)),
        ],
        out_specs=tile,
        out_shape=jax.ShapeDtypeStruct((t, D_MODEL), F32),
        compiler_params=pltpu.CompilerParams(
            dimension_semantics=("arbitrary",), vmem_limit_bytes=VMEM_LIMIT),
        name="post",
    )(x2, y_pool, y_dn, kt3, v3, w_out, g_xa, w_q, w_o, g_mlp, w_up, w_dn, g_fin)


def kernel(x, mem, norm_mix_g, w_in, w_pool, pool_scale, conv_w, a_log, dt_bias, dn_out_norm_g, w_out,
           norm_xattn_g, mem_norm_g, w_xq, w_xk, w_xv, w_xo, norm_mlp_g, w_up, w_down, final_norm_g):
    b, s, d = x.shape
    assert w_in.shape[0] == 1, "single-layer problem: the final RMSNorm is fused into the layer's post stage"
    row = lambda v: v.reshape(1, -1).astype(F32)

    w_qkv = w_in[0][:, QKV_OFF:QKV_OFF + 3 * DN_WIDTH].astype(BF16)
    w_uz = jnp.concatenate([w_in[0][:, :POOL_WIDTH], w_in[0][:, Z_OFF:Z_OFF + DN_WIDTH]], axis=1).astype(BF16)
    w_ba = jnp.pad(w_in[0][:, MAIN_COLS:], ((0, 0), (0, LANES - 2 * DN_HEADS))).astype(BF16)
    gate_params = jnp.zeros((8, LANES), F32)
    gate_params = gate_params.at[DN_HEADS:, 0].set(a_log[0]).at[DN_HEADS:, 1].set(dt_bias[0])

    x2 = x.reshape(b * s, d)
    later_weights = [w_xk[0], w_xv[0], w_out[0], w_xq[0], w_xo[0], w_up[0], w_down[0]]
    q, k, v, grow, gcol, y_pool, sz, later16 = _in_proj(
        x2, row(norm_mix_g[0]), w_qkv, w_ba, w_uz, conv_w[0].astype(F32), gate_params, w_pool[0].astype(BF16),
        row(pool_scale[0]), later_weights, s)
    w_xk16, w_xv16, w_out16, w_xq16, w_xo16, w_up16, w_down16 = later16
    y_dn = _gdn(q, k, v, grow, gcol, sz, row(dn_out_norm_g[0]), s)
    kt3, v3 = _mem_kv(mem, row(mem_norm_g[0]), w_xk16, w_xv16)
    out = _post(x2, y_pool, y_dn, kt3, v3, w_out16, row(norm_xattn_g[0]),
                w_xq16, w_xo16, row(norm_mlp_g[0]), w_up16, w_down16, row(final_norm_g), s)
    return out.reshape(b, s, d)
```

```python
import functools

import jax
import jax.numpy as jnp
from jax import lax
from jax.experimental import pallas as pl
from jax.experimental.pallas import tpu as pltpu

D_MODEL = 1024
POOL_WIDTH = 512
POOL_WINDOWS = (2, 4, 8, 16)
POOL_GROUP_DIM = 128
DN_WIDTH = 512
DN_HEADS = 4
DN_HEAD_DIM = 128
CONV_K = 4
MAIN_COLS = POOL_WIDTH + 4 * DN_WIDTH
QKV_OFF = POOL_WIDTH
Z_OFF = POOL_WIDTH + 3 * DN_WIDTH
XA_HEADS = 4
XA_HEAD_DIM = 256
D_FF = 4096
EPS = 1e-6

LANES = 128
GATE_ROWS = 2 * DN_HEADS
GDN_CHUNK = 128
POOL_HALO = 16
CONV_HALO = 8
TM = 1024
TM_GDN = 2048
TM_POST = 1024
POST_ROW_BLOCK = 512
CHUNKS_PER_SWEEP = 2
GDN_SLOTS = 2 * CHUNKS_PER_SWEEP * DN_HEADS
FF_BLOCK = 1024
_ROW_BLOCKS = TM // GDN_CHUNK
PREP_SEGMENTS_AFTER_IN_PROJ_SEGMENT = (0, 0, 1 + _ROW_BLOCKS, _ROW_BLOCKS, _ROW_BLOCKS, len(POOL_WINDOWS))
VMEM_LIMIT = 60 * 1024 * 1024

F32 = jnp.float32
BF16 = jnp.bfloat16


def _dot(a, b):
    return jnp.dot(a, b, preferred_element_type=F32)


def _rms(x, g):
    return x * lax.rsqrt(jnp.mean(x * x, axis=-1, keepdims=True) + EPS) * g


def _silu(x):
    return x * jax.nn.sigmoid(x)


def _const_spec(shape):
    zeros = (0,) * len(shape)
    return pl.BlockSpec(shape, lambda *_: zeros, pipeline_mode=pl.Buffered(1))


def _round_robin(generators):
    live = list(generators)
    while live:
        for g in list(live):
            try:
                yield next(g)
            except StopIteration:
                live.remove(g)


def _interleave(first, second):
    for _ in first:
        next(second, None)
    for _ in second:
        pass


def _pool_mixer(t0, wpool_ref, pscale_ref, y_ref, extp_ref):
    tm = y_ref.shape[0]
    t_glob = lax.broadcasted_iota(jnp.int32, (tm, POOL_GROUP_DIM), 0) + t0
    for gi, win in enumerate(POOL_WINDOWS):
        lo = gi * POOL_GROUP_DIM
        assert win & (win - 1) == 0 and win <= POOL_HALO, "doubling needs power-of-two windows within the halo"
        acc = extp_ref[:, lo:lo + POOL_GROUP_DIM]
        shift = 1
        while shift < win:
            acc = acc + pltpu.roll(acc, shift, axis=0)
            shift *= 2
        acc = acc[POOL_HALO:, :]
        u = extp_ref[POOL_HALO:POOL_HALO + tm, lo:lo + POOL_GROUP_DIM]
        cnt = jnp.minimum(t_glob + 1, win).astype(F32)
        pooled = acc / cnt - u
        mixed = _dot(pooled.astype(BF16), wpool_ref[gi])
        y_ref[:, lo:lo + POOL_GROUP_DIM] = (mixed * pscale_ref[:, lo:lo + POOL_GROUP_DIM]).astype(y_ref.dtype)
        yield
    extp_ref[0:POOL_HALO, :] = extp_ref[tm:tm + POOL_HALO, :]


def _in_proj_matmuls(x_ref, g_ref, wqkv_ref, wba_ref, wuz_ref, hn_ref, qkv_ref, ba_ref, extp_ref, z_ref):
    hn_ref[...] = _rms(x_ref[...], g_ref[...]).astype(BF16)
    ba_ref[...] = _dot(hn_ref[...], wba_ref[...])
    yield
    for part in range(3):
        lanes = slice(part * DN_WIDTH, (part + 1) * DN_WIDTH)
        qkv_ref[CONV_HALO:CONV_HALO + TM, lanes] = _dot(hn_ref[...], wqkv_ref[:, lanes])
        yield
    extp_ref[POOL_HALO:, :] = _dot(hn_ref[...], wuz_ref[:, 0:POOL_WIDTH])
    yield
    z_ref[...] = _dot(hn_ref[...], wuz_ref[:, POOL_WIDTH:POOL_WIDTH + DN_WIDTH])
    yield


def _in_proj_prep(qkv_ref, ba_ref, z_ref, convw_ref, gp_ref, q_ref, k_ref, v_ref, grow_ref, gcol_ref, sz_ref, pool):
    c = GDN_CHUNK

    bat = ba_ref[...].T[0:GATE_ROWS, :]
    gp = gp_ref[...]
    a_log = gp[:, 0:1]
    dt_bias = gp[:, 1:2]
    xa = bat + dt_bias
    softplus = jnp.maximum(xa, 0.0) + jnp.log1p(jnp.exp(-jnp.abs(xa)))
    cs = -jnp.exp(a_log) * softplus
    lane = lax.broadcasted_iota(jnp.int32, (GATE_ROWS, TM), 1) & (c - 1)
    sh = 1
    while sh < c:
        cs = cs + jnp.where(lane >= sh, pltpu.roll(cs, sh, axis=1), 0.0)
        sh *= 2
    row = lax.broadcasted_iota(jnp.int32, (GATE_ROWS, TM), 0)
    stack = jnp.where(row < DN_HEADS, jax.nn.sigmoid(bat), cs)
    grow_ref[...] = stack
    gcol_ref[...] = jnp.concatenate([stack, jnp.zeros((LANES - GATE_ROWS, TM), F32)], axis=0).T
    yield

    for part, dst in enumerate((q_ref, k_ref, v_ref)):
        for r0 in range(0, TM, c):
            for h in range(DN_HEADS):
                lo = h * DN_HEAD_DIM
                lanes = slice(part * DN_WIDTH + lo, part * DN_WIDTH + lo + DN_HEAD_DIM)
                xe = qkv_ref[r0:r0 + CONV_HALO + c, lanes]
                xc = convw_ref[CONV_K - 1:CONV_K, lanes] * xe
                for d in range(1, CONV_K):
                    xc = xc + convw_ref[CONV_K - 1 - d:CONV_K - d, lanes] * pltpu.roll(xe, d, axis=0)
                dst[r0:r0 + c, lo:lo + DN_HEAD_DIM] = xc[CONV_HALO:, :]
            yield
    yield from pool
    for r0 in range(0, TM, c):
        sz_ref[r0:r0 + c, :] = _silu(z_ref[r0:r0 + c, :])
        yield


def _in_proj_kernel(tiles_per_batch, n_cast, *refs):
    x_ref, g_ref, wqkv_ref, wba_ref, wuz_ref, convw_ref, gp_ref, wpool_ref, pscale_ref = refs[:9]
    cast_src = refs[9:9 + n_cast]
    q_ref, k_ref, v_ref, grow_ref, gcol_ref, ypool_ref, sz_ref = refs[9 + n_cast:16 + n_cast]
    cast_dst = refs[16 + n_cast:16 + 2 * n_cast]
    hn_ref, qkv_ref, ba_ref, extp_ref, z_ref = refs[16 + 2 * n_cast:]
    t_in_batch = lax.rem(pl.program_id(0), jnp.int32(tiles_per_batch))

    for src, dst in zip(cast_src, cast_dst):
        dst[...] = src[...].astype(dst.dtype)

    @pl.when(t_in_batch == 0)
    def _():
        qkv_ref[0:CONV_HALO, :] = jnp.zeros((CONV_HALO, 3 * DN_WIDTH), F32)
        extp_ref[0:POOL_HALO, :] = jnp.zeros((POOL_HALO, POOL_WIDTH), F32)

    @pl.when(t_in_batch != 0)
    def _():
        qkv_ref[0:CONV_HALO, :] = qkv_ref[TM:TM + CONV_HALO, :]

    matmuls = _in_proj_matmuls(x_ref, g_ref, wqkv_ref, wba_ref, wuz_ref, hn_ref, qkv_ref, ba_ref, extp_ref, z_ref)
    pool = _pool_mixer(t_in_batch * TM, wpool_ref, pscale_ref, ypool_ref, extp_ref)
    prep = _in_proj_prep(qkv_ref, ba_ref, z_ref, convw_ref, gp_ref, q_ref, k_ref, v_ref, grow_ref, gcol_ref,
                         sz_ref, pool)
    for n_prep_segments in PREP_SEGMENTS_AFTER_IN_PROJ_SEGMENT:
        next(matmuls)
        for _ in range(n_prep_segments):
            next(prep)
    done = object()
    assert next(matmuls, done) is done
    for _ in prep:
        pass


def _in_proj(x2, g, w_qkv, w_ba, w_uz, conv_w, gate_params, w_pool, pool_scale, later_weights, seq):
    t = x2.shape[0]
    n_tiles = t // TM
    tiles_per_gdn_tile = TM_GDN // TM
    tile = lambda width: pl.BlockSpec((TM, width), lambda i: (i, 0))
    wide = jax.ShapeDtypeStruct((t, DN_WIDTH), F32)
    assert all(w.shape[0] % (16 * n_tiles) == 0 for w in later_weights)
    cast_specs = [pl.BlockSpec((w.shape[0] // n_tiles, w.shape[1]), lambda i: (i, 0)) for w in later_weights]
    outs = pl.pallas_call(
        functools.partial(_in_proj_kernel, seq // TM, len(later_weights)),
        grid=(n_tiles,),
        in_specs=[
            tile(D_MODEL),
            _const_spec((1, D_MODEL)),
            _const_spec((D_MODEL, 3 * DN_WIDTH)),
            _const_spec((D_MODEL, LANES)),
            _const_spec((D_MODEL, POOL_WIDTH + DN_WIDTH)),
            _const_spec((CONV_K, 3 * DN_WIDTH)),
            _const_spec((GATE_ROWS, LANES)),
            _const_spec((len(POOL_WINDOWS), POOL_GROUP_DIM, POOL_GROUP_DIM)),
            _const_spec((1, POOL_WIDTH)),
        ] + cast_specs,
        out_specs=[
            tile(DN_WIDTH), tile(DN_WIDTH), tile(DN_WIDTH),
            pl.BlockSpec((None, GATE_ROWS, TM), lambda i: (i // tiles_per_gdn_tile, 0, i % tiles_per_gdn_tile)),
            tile(LANES), tile(POOL_WIDTH), tile(DN_WIDTH),
        ] + cast_specs,
        out_shape=[
            wide, wide, wide,
            jax.ShapeDtypeStruct((t // TM_GDN, GATE_ROWS, TM_GDN), F32),
            jax.ShapeDtypeStruct((t, LANES), F32),
            jax.ShapeDtypeStruct((t, POOL_WIDTH), BF16),
            wide,
        ] + [jax.ShapeDtypeStruct(w.shape, BF16) for w in later_weights],
        scratch_shapes=[
            pltpu.VMEM((TM, D_MODEL), BF16),
            pltpu.VMEM((CONV_HALO + TM, 3 * DN_WIDTH), F32),
            pltpu.VMEM((TM, LANES), F32),
            pltpu.VMEM((POOL_HALO + TM, POOL_WIDTH), F32),
            pltpu.VMEM((TM, DN_WIDTH), F32),
        ],
        compiler_params=pltpu.CompilerParams(
            dimension_semantics=("arbitrary",), vmem_limit_bytes=VMEM_LIMIT),
        name="in_proj",
    )(x2, g, w_qkv, w_ba, w_uz, conv_w, gate_params, w_pool, pool_scale, *later_weights)
    return (*outs[:7], tuple(outs[7:]))


def _slot(chunk, head):
    return (chunk * DN_HEADS + head) % GDN_SLOTS


def _gdn_prep(chunks, q_ref, k_ref, v_ref, grow_ref, gcol_ref, ops):
    c = GDN_CHUNK
    ii = lax.broadcasted_iota(jnp.int32, (c, c), 0)
    jj = lax.broadcasted_iota(jnp.int32, (c, c), 1)
    tril = ii >= jj
    strict = ii > jj

    def col(r0, j):
        return jnp.broadcast_to(gcol_ref[r0:r0 + c, j:j + 1], (c, c))

    for ci in chunks:
        for h in range(DN_HEADS):
            r0 = ci * c
            lo = h * DN_HEAD_DIM
            qh = _silu(q_ref[r0:r0 + c, lo:lo + DN_HEAD_DIM])
            kh = _silu(k_ref[r0:r0 + c, lo:lo + DN_HEAD_DIM])
            vh = _silu(v_ref[r0:r0 + c, lo:lo + DN_HEAD_DIM])
            qn = qh * (lax.rsqrt(jnp.sum(qh * qh, axis=-1, keepdims=True) + EPS) * (DN_HEAD_DIM ** -0.5))
            kn = kh * lax.rsqrt(jnp.sum(kh * kh, axis=-1, keepdims=True) + EPS)
            beta_b = col(r0, h)
            gc_b = col(r0, DN_HEADS + h)
            gc_r = grow_ref[DN_HEADS + h:DN_HEADS + h + 1, r0:r0 + c]
            g_last = gcol_ref[r0 + c - 1:r0 + c, DN_HEADS + h:DN_HEADS + h + 1]
            d = jnp.where(tril, jnp.exp(gc_b - gc_r), 0.0)
            eg = jnp.exp(gc_b)
            kb = kn * beta_b
            kt = kn.T
            ops["unit"].append((ci, h))
            ops["dec"].append(d)
            ops["neg_d"].append(jnp.where(strict, -d, 0.0))
            ops["kt16"].append(kt.astype(BF16))
            ops["lhs_kq"].append(jnp.concatenate([kb, qn], axis=0).astype(BF16))
            ops["vbkg"].append(jnp.concatenate([vh * beta_b, kb * eg], axis=1).astype(BF16))
            ops["qg16"].append((qn * eg).astype(BF16))
            ops["kdt16"].append((kt * jnp.exp(g_last - gc_r)).astype(BF16))
            if h % 2 == 1:
                yield


def _gdn_stages(ops, u_ref, wq_ref, ak_ref):
    c = GDN_CHUNK
    ii = lax.broadcasted_iota(jnp.int32, (c, c), 0)
    jj = lax.broadcasted_iota(jnp.int32, (c, c), 1)
    eye = jnp.where(ii == jj, 1.0, 0.0).astype(F32)
    n = len(ops["unit"])
    kq = [_dot(ops["lhs_kq"][i], ops["kt16"][i]) for i in range(n)]
    yield
    xs = [kq[i][:c] * ops["neg_d"][i] for i in range(n)]
    ps = [eye + xs[i] for i in range(n)]
    att16 = [(kq[i][c:] * ops["dec"][i]).astype(BF16) for i in range(n)]
    x16 = [x.astype(BF16) for x in xs]
    x16 = [_dot(x, x).astype(BF16) for x in x16]
    yield
    span = 2
    while 2 * span < c:
        both = [_dot(jnp.concatenate([ps[i].astype(BF16), x16[i]], axis=0), x16[i]) for i in range(n)]
        ps = [ps[i] + both[i][:c] for i in range(n)]
        x16 = [both[i][c:].astype(BF16) for i in range(n)]
        span *= 2
        yield
    ps = [ps[i] + _dot(ps[i].astype(BF16), x16[i]) for i in range(n)]
    yield
    uw = [_dot(ps[i].astype(BF16), ops["vbkg"][i]) for i in range(n)]
    for i, (ci, h) in enumerate(ops["unit"]):
        idx = _slot(ci, h)
        u_ref[idx] = uw[i][:, :DN_HEAD_DIM]
        wq_ref[idx] = jnp.concatenate([uw[i][:, DN_HEAD_DIM:].astype(BF16), ops["qg16"][i]], axis=0)
        ak_ref[idx] = jnp.concatenate([att16[i], ops["kdt16"][i]], axis=0)
    yield


def _gdn_phase2(chunks, gcol_ref, sz_ref, og_ref, state_ref, u_ref, wq_ref, ak_ref, y_ref):
    c = GDN_CHUNK
    heads = range(DN_HEADS)
    for ci in chunks:
        r0 = ci * c
        st = [state_ref[h] for h in heads]
        st16 = [s.astype(BF16) for s in st]
        r1 = [_dot(wq_ref[_slot(ci, h)], st16[h]) for h in heads]
        yield
        vn16 = [(u_ref[_slot(ci, h)] - r1[h][:c]).astype(BF16) for h in heads]
        r2 = [_dot(ak_ref[_slot(ci, h)], vn16[h]) for h in heads]
        yield
        for h in heads:
            lo = h * DN_HEAD_DIM
            g_last = gcol_ref[r0 + c - 1:r0 + c, DN_HEADS + h:DN_HEADS + h + 1]
            state_ref[h] = st[h] * jnp.exp(g_last) + r2[h][c:]
            o = r1[h][c:] + r2[h][:c]
            on = o * lax.rsqrt(jnp.mean(o * o, axis=-1, keepdims=True) + EPS) * og_ref[...]
            y_ref[r0:r0 + c, lo:lo + DN_HEAD_DIM] = (on * sz_ref[r0:r0 + c, lo:lo + DN_HEAD_DIM]).astype(y_ref.dtype)


def _gdn_kernel(q_ref, k_ref, v_ref, grow_ref, gcol_ref, sz_ref, og_ref, y_ref, state_ref, u_ref, wq_ref, ak_ref):
    @pl.when(pl.program_id(1) == 0)
    def _():
        state_ref[...] = jnp.zeros_like(state_ref)

    n_chunks = TM_GDN // GDN_CHUNK
    sweeps = [range(g0, g0 + CHUNKS_PER_SWEEP) for g0 in range(0, n_chunks, CHUNKS_PER_SWEEP)]
    new_ops = lambda: {name: [] for name in ("unit", "dec", "neg_d", "kt16", "lhs_kq", "vbkg", "qg16", "kdt16")}
    prep = lambda chunks, ops: _gdn_prep(chunks, q_ref, k_ref, v_ref, grow_ref, gcol_ref, ops)
    phase2 = lambda chunks: _gdn_phase2(chunks, gcol_ref, sz_ref, og_ref, state_ref, u_ref, wq_ref, ak_ref, y_ref)
    ops = new_ops()
    for _ in prep(sweeps[0], ops):
        pass
    for i, sweep in enumerate(sweeps):
        next_ops = new_ops()
        side = []
        if i + 1 < len(sweeps):
            side.append(prep(sweeps[i + 1], next_ops))
        if i > 0:
            side.append(phase2(sweeps[i - 1]))
        _interleave(_gdn_stages(ops, u_ref, wq_ref, ak_ref), _round_robin(side))
        ops = next_ops
    for _ in phase2(sweeps[-1]):
        pass


def _gdn(q2, k2, v2, grow3, gcol2, sz2, o_norm_g, seq):
    t = q2.shape[0]
    b = t // seq
    tiles_per_batch = seq // TM_GDN
    tile = lambda width: pl.BlockSpec((TM_GDN, width), lambda bi, si: (bi * tiles_per_batch + si, 0))
    return pl.pallas_call(
        _gdn_kernel,
        grid=(b, tiles_per_batch),
        in_specs=[tile(DN_WIDTH), tile(DN_WIDTH), tile(DN_WIDTH),
                  pl.BlockSpec((None, GATE_ROWS, TM_GDN), lambda bi, si: (bi * tiles_per_batch + si, 0, 0)),
                  tile(LANES), tile(DN_WIDTH),
                  _const_spec((1, DN_HEAD_DIM))],
        out_specs=tile(DN_WIDTH),
        out_shape=jax.ShapeDtypeStruct((t, DN_WIDTH), BF16),
        scratch_shapes=[
            pltpu.VMEM((DN_HEADS, DN_HEAD_DIM, DN_HEAD_DIM), F32),
            pltpu.VMEM((GDN_SLOTS, GDN_CHUNK, DN_HEAD_DIM), F32),
            pltpu.VMEM((GDN_SLOTS, 2 * GDN_CHUNK, DN_HEAD_DIM), BF16),
            pltpu.VMEM((GDN_SLOTS, 2 * GDN_CHUNK, DN_HEAD_DIM), BF16),
        ],
        compiler_params=pltpu.CompilerParams(
            dimension_semantics=("arbitrary", "arbitrary"), vmem_limit_bytes=VMEM_LIMIT),
        name="gdn",
    )(q2, k2, v2, grow3, gcol2, sz2, o_norm_g)


def _mem_kv_kernel(m_ref, g_ref, wk_ref, wv_ref, kt_ref, v_ref):
    mn = _rms(m_ref[...], g_ref[...]).astype(BF16)
    kt_ref[...] = _dot(mn, wk_ref[...]).T.astype(kt_ref.dtype)
    v_ref[...] = _dot(mn, wv_ref[...]).astype(v_ref.dtype)


def _mem_kv(mem, g, w_k, w_v):
    b, mem_len, _ = mem.shape
    return pl.pallas_call(
        _mem_kv_kernel,
        grid=(b,),
        in_specs=[
            pl.BlockSpec((None, mem_len, D_MODEL), lambda i: (i, 0, 0)),
            _const_spec((1, D_MODEL)),
            _const_spec((D_MODEL, D_MODEL)),
            _const_spec((D_MODEL, D_MODEL)),
        ],
        out_specs=[
            pl.BlockSpec((None, D_MODEL, mem_len), lambda i: (i, 0, 0)),
            pl.BlockSpec((None, mem_len, D_MODEL), lambda i: (i, 0, 0)),
        ],
        out_shape=[
            jax.ShapeDtypeStruct((b, D_MODEL, mem_len), BF16),
            jax.ShapeDtypeStruct((b, mem_len, D_MODEL), BF16),
        ],
        compiler_params=pltpu.CompilerParams(
            dimension_semantics=("arbitrary",), vmem_limit_bytes=VMEM_LIMIT),
        name="mem_kv",
    )(mem, g, w_k, w_v)


def _post_rows(rows, x_ref, ypool_ref, ydn_ref, kt_ref, v_ref, wout_ref, gxa_ref, wq_ref, wo_ref, gmlp_ref,
               wup_ref, wdn_ref, gfin_ref, out_ref):
    y = jnp.concatenate([ypool_ref[rows, :], ydn_ref[rows, :]], axis=-1)
    h1 = x_ref[rows, :] + _dot(y, wout_ref[...])
    yield
    q = _dot(_rms(h1, gxa_ref[...]).astype(BF16), wq_ref[...]).astype(BF16)
    yield
    head_lanes = [slice(hd * XA_HEAD_DIM, (hd + 1) * XA_HEAD_DIM) for hd in range(XA_HEADS)]
    scores = [_dot(q[:, hl], kt_ref[hl, :]) * (XA_HEAD_DIM ** -0.5) for hl in head_lanes]
    yield
    expd = [jnp.exp(s - jnp.max(s, axis=-1, keepdims=True)) for s in scores]
    pv = [_dot(e.astype(BF16), v_ref[:, hl]) for e, hl in zip(expd, head_lanes)]
    yield
    att = jnp.concatenate([(o / jnp.sum(e, axis=-1, keepdims=True)).astype(BF16) for e, o in zip(expd, pv)],
                          axis=-1)
    h2 = h1 + _dot(att, wo_ref[...])
    yield
    hn2 = _rms(h2, gmlp_ref[...]).astype(BF16)
    acc = h2
    for j in range(D_FF // FF_BLOCK):
        a = jnp.maximum(_dot(hn2, wup_ref[:, j * FF_BLOCK:(j + 1) * FF_BLOCK]), 0.0)
        yield
        acc = acc + _dot((a * a).astype(BF16), wdn_ref[j * FF_BLOCK:(j + 1) * FF_BLOCK, :])
        yield
    out_ref[rows, :] = _rms(acc, gfin_ref[...])


def _post_kernel(*refs):
    blocks = [slice(r0, r0 + POST_ROW_BLOCK) for r0 in range(0, TM_POST, POST_ROW_BLOCK)]
    for _ in _round_robin([_post_rows(rows, *refs) for rows in blocks]):
        pass


def _post(x2, y_pool, y_dn, kt3, v3, w_out, g_xa, w_q, w_o, g_mlp, w_up, w_dn, g_fin, seq):
    t = x2.shape[0]
    mem_len = v3.shape[1]
    tiles_per_batch = seq // TM_POST
    tile = pl.BlockSpec((TM_POST, D_MODEL), lambda i: (i, 0))
    half_tile = pl.BlockSpec((TM_POST, D_MODEL // 2), lambda i: (i, 0))
    return pl.pallas_call(
        _post_kernel,
        grid=(t // TM_POST,),
        in_specs=[
            tile, half_tile, half_tile,
            pl.BlockSpec((None, D_MODEL, mem_len), lambda i: (i // tiles_per_batch, 0, 0)),
            pl.BlockSpec((None, mem_len, D_MODEL), lambda i: (i // tiles_per_batch, 0, 0)),
            _const_spec((D_MODEL, D_MODEL)),
            _const_spec((1, D_MODEL)),
            _const_spec((D_MODEL, D_MODEL)),
            _const_spec((D_MODEL, D_MODEL)),
            _const_spec((1, D_MODEL)),
            _const_spec((D_MODEL, D_FF)),
            _const_spec((D_FF, D_MODEL)),
            _const_spec((1, D_MODEL)),
        ],
        out_specs=tile,
        out_shape=jax.ShapeDtypeStruct((t, D_MODEL), F32),
        compiler_params=pltpu.CompilerParams(
            dimension_semantics=("arbitrary",), vmem_limit_bytes=VMEM_LIMIT),
        name="post",
    )(x2, y_pool, y_dn, kt3, v3, w_out, g_xa, w_q, w_o, g_mlp, w_up, w_dn, g_fin)


def kernel(x, mem, norm_mix_g, w_in, w_pool, pool_scale, conv_w, a_log, dt_bias, dn_out_norm_g, w_out,
           norm_xattn_g, mem_norm_g, w_xq, w_xk, w_xv, w_xo, norm_mlp_g, w_up, w_down, final_norm_g):
    b, s, d = x.shape
    assert w_in.shape[0] == 1, "single-layer problem: the final RMSNorm is fused into the layer's post stage"
    row = lambda v: v.reshape(1, -1).astype(F32)

    w_qkv = w_in[0][:, QKV_OFF:QKV_OFF + 3 * DN_WIDTH].astype(BF16)
    w_uz = jnp.concatenate([w_in[0][:, :POOL_WIDTH], w_in[0][:, Z_OFF:Z_OFF + DN_WIDTH]], axis=1).astype(BF16)
    w_ba = jnp.pad(w_in[0][:, MAIN_COLS:], ((0, 0), (0, LANES - 2 * DN_HEADS))).astype(BF16)
    gate_params = jnp.zeros((GATE_ROWS, LANES), F32)
    gate_params = gate_params.at[DN_HEADS:, 0].set(a_log[0]).at[DN_HEADS:, 1].set(dt_bias[0])

    x2 = x.reshape(b * s, d)
    later_weights = [w_xk[0], w_xv[0], w_out[0], w_xq[0], w_xo[0], w_up[0], w_down[0]]
    q, k, v, grow, gcol, y_pool, sz, later16 = _in_proj(
        x2, row(norm_mix_g[0]), w_qkv, w_ba, w_uz, conv_w[0].astype(F32), gate_params, w_pool[0].astype(BF16),
        row(pool_scale[0]), later_weights, s)
    w_xk16, w_xv16, w_out16, w_xq16, w_xo16, w_up16, w_down16 = later16
    y_dn = _gdn(q, k, v, grow, gcol, sz, row(dn_out_norm_g[0]), s)
    kt3, v3 = _mem_kv(mem, row(mem_norm_g[0]), w_xk16, w_xv16)
    out = _post(x2, y_pool, y_dn, kt3, v3, w_out16, row(norm_xattn_g[0]),
                w_xq16, w_xo16, row(norm_mlp_g[0]), w_up16, w_down16, row(final_norm_g), s)
    return out.reshape(b, s, d)
```

```python
import functools

import jax
import jax.numpy as jnp
from jax import lax
from jax.experimental import pallas as pl
from jax.experimental.pallas import tpu as pltpu

D_MODEL = 1024
POOL_WIDTH = 512
POOL_WINDOWS = (2, 4, 8, 16)
POOL_GROUP_DIM = 128
DN_WIDTH = 512
DN_HEADS = 4
DN_HEAD_DIM = 128
CONV_K = 4
MAIN_COLS = POOL_WIDTH + 4 * DN_WIDTH
QKV_OFF = POOL_WIDTH
Z_OFF = POOL_WIDTH + 3 * DN_WIDTH
XA_HEADS = 4
XA_HEAD_DIM = 256
D_FF = 4096
EPS = 1e-6

LANES = 128
GATE_ROWS = 2 * DN_HEADS
GDN_CHUNK = 128
POOL_HALO = 16
CONV_HALO = 8
TM = 1024
TM_GDN = 2048
TM_POST = 1024
POST_ROW_BLOCK = 512
CHUNKS_PER_SWEEP = 2
GDN_SLOTS = 2 * CHUNKS_PER_SWEEP * DN_HEADS
FF_BLOCK = 1024
MEM_BATCHES_PER_STEP = 4
_ROW_BLOCKS = TM // GDN_CHUNK
PREP_SEGMENTS_AFTER_IN_PROJ_SEGMENT = (0, 0, 1 + _ROW_BLOCKS, _ROW_BLOCKS, _ROW_BLOCKS, len(POOL_WINDOWS))
VMEM_LIMIT = 60 * 1024 * 1024

F32 = jnp.float32
BF16 = jnp.bfloat16


def _dot(a, b):
    return jnp.dot(a, b, preferred_element_type=F32)


def _rms(x, g):
    return x * lax.rsqrt(jnp.mean(x * x, axis=-1, keepdims=True) + EPS) * g


def _silu(x):
    return x * jax.nn.sigmoid(x)


def _const_spec(shape):
    zeros = (0,) * len(shape)
    return pl.BlockSpec(shape, lambda *_: zeros, pipeline_mode=pl.Buffered(1))


def _round_robin(generators):
    live = list(generators)
    while live:
        for g in list(live):
            try:
                yield next(g)
            except StopIteration:
                live.remove(g)


def _interleave(first, second):
    for _ in first:
        next(second, None)
    for _ in second:
        pass


def _pool_mixer(t0, wpool_ref, pscale_ref, y_ref, extp_ref):
    tm = y_ref.shape[0]
    t_glob = lax.broadcasted_iota(jnp.int32, (tm, POOL_GROUP_DIM), 0) + t0
    for gi, win in enumerate(POOL_WINDOWS):
        lo = gi * POOL_GROUP_DIM
        assert win & (win - 1) == 0 and win <= POOL_HALO, "doubling needs power-of-two windows within the halo"
        acc = extp_ref[:, lo:lo + POOL_GROUP_DIM]
        shift = 1
        while shift < win:
            acc = acc + pltpu.roll(acc, shift, axis=0)
            shift *= 2
        acc = acc[POOL_HALO:, :]
        u = extp_ref[POOL_HALO:POOL_HALO + tm, lo:lo + POOL_GROUP_DIM]
        cnt = jnp.minimum(t_glob + 1, win).astype(F32)
        pooled = acc / cnt - u
        mixed = _dot(pooled.astype(BF16), wpool_ref[gi])
        y_ref[:, lo:lo + POOL_GROUP_DIM] = (mixed * pscale_ref[:, lo:lo + POOL_GROUP_DIM]).astype(y_ref.dtype)
        yield
    extp_ref[0:POOL_HALO, :] = extp_ref[tm:tm + POOL_HALO, :]


def _in_proj_matmuls(x_ref, g_ref, wqkv_ref, wba_ref, wuz_ref, hn_ref, qkv_ref, ba_ref, extp_ref, z_ref):
    hn_ref[...] = _rms(x_ref[...], g_ref[...]).astype(BF16)
    ba_ref[...] = _dot(hn_ref[...], wba_ref[...])
    yield
    for part in range(3):
        lanes = slice(part * DN_WIDTH, (part + 1) * DN_WIDTH)
        qkv_ref[CONV_HALO:CONV_HALO + TM, lanes] = _dot(hn_ref[...], wqkv_ref[:, lanes])
        yield
    extp_ref[POOL_HALO:, :] = _dot(hn_ref[...], wuz_ref[:, 0:POOL_WIDTH])
    yield
    z_ref[...] = _dot(hn_ref[...], wuz_ref[:, POOL_WIDTH:POOL_WIDTH + DN_WIDTH])
    yield


def _in_proj_prep(qkv_ref, ba_ref, z_ref, convw_ref, gp_ref, q_ref, k_ref, v_ref, grow_ref, gcol_ref, sz_ref, pool):
    c = GDN_CHUNK

    bat = ba_ref[...].T[0:GATE_ROWS, :]
    gp = gp_ref[...]
    a_log = gp[:, 0:1]
    dt_bias = gp[:, 1:2]
    xa = bat + dt_bias
    softplus = jnp.maximum(xa, 0.0) + jnp.log1p(jnp.exp(-jnp.abs(xa)))
    cs = -jnp.exp(a_log) * softplus
    lane = lax.broadcasted_iota(jnp.int32, (GATE_ROWS, TM), 1) & (c - 1)
    sh = 1
    while sh < c:
        cs = cs + jnp.where(lane >= sh, pltpu.roll(cs, sh, axis=1), 0.0)
        sh *= 2
    row = lax.broadcasted_iota(jnp.int32, (GATE_ROWS, TM), 0)
    stack = jnp.where(row < DN_HEADS, jax.nn.sigmoid(bat), cs)
    grow_ref[...] = stack
    gcol_ref[...] = jnp.concatenate([stack, jnp.zeros((LANES - GATE_ROWS, TM), F32)], axis=0).T
    yield

    for part, dst in enumerate((q_ref, k_ref, v_ref)):
        for r0 in range(0, TM, c):
            for h in range(DN_HEADS):
                lo = h * DN_HEAD_DIM
                lanes = slice(part * DN_WIDTH + lo, part * DN_WIDTH + lo + DN_HEAD_DIM)
                xe = qkv_ref[r0:r0 + CONV_HALO + c, lanes]
                xc = convw_ref[CONV_K - 1:CONV_K, lanes] * xe
                for d in range(1, CONV_K):
                    xc = xc + convw_ref[CONV_K - 1 - d:CONV_K - d, lanes] * pltpu.roll(xe, d, axis=0)
                dst[r0:r0 + c, lo:lo + DN_HEAD_DIM] = xc[CONV_HALO:, :]
            yield
    yield from pool
    for r0 in range(0, TM, c):
        sz_ref[r0:r0 + c, :] = _silu(z_ref[r0:r0 + c, :])
        yield


def _in_proj_kernel(tiles_per_batch, n_cast, *refs):
    x_ref, g_ref, wqkv_ref, wba_ref, wuz_ref, convw_ref, gp_ref, wpool_ref, pscale_ref = refs[:9]
    cast_src = refs[9:9 + n_cast]
    q_ref, k_ref, v_ref, grow_ref, gcol_ref, ypool_ref, sz_ref = refs[9 + n_cast:16 + n_cast]
    cast_dst = refs[16 + n_cast:16 + 2 * n_cast]
    hn_ref, qkv_ref, ba_ref, extp_ref, z_ref = refs[16 + 2 * n_cast:]
    t_in_batch = lax.rem(pl.program_id(0), jnp.int32(tiles_per_batch))

    for src, dst in zip(cast_src, cast_dst):
        dst[...] = src[...].astype(dst.dtype)

    @pl.when(t_in_batch == 0)
    def _():
        qkv_ref[0:CONV_HALO, :] = jnp.zeros((CONV_HALO, 3 * DN_WIDTH), F32)
        extp_ref[0:POOL_HALO, :] = jnp.zeros((POOL_HALO, POOL_WIDTH), F32)

    @pl.when(t_in_batch != 0)
    def _():
        qkv_ref[0:CONV_HALO, :] = qkv_ref[TM:TM + CONV_HALO, :]

    matmuls = _in_proj_matmuls(x_ref, g_ref, wqkv_ref, wba_ref, wuz_ref, hn_ref, qkv_ref, ba_ref, extp_ref, z_ref)
    pool = _pool_mixer(t_in_batch * TM, wpool_ref, pscale_ref, ypool_ref, extp_ref)
    prep = _in_proj_prep(qkv_ref, ba_ref, z_ref, convw_ref, gp_ref, q_ref, k_ref, v_ref, grow_ref, gcol_ref,
                         sz_ref, pool)
    for n_prep_segments in PREP_SEGMENTS_AFTER_IN_PROJ_SEGMENT:
        next(matmuls)
        for _ in range(n_prep_segments):
            next(prep)
    done = object()
    assert next(matmuls, done) is done
    for _ in prep:
        pass


def _in_proj(x2, g, w_qkv, w_ba, w_uz, conv_w, gate_params, w_pool, pool_scale, later_weights, seq):
    t = x2.shape[0]
    n_tiles = t // TM
    tiles_per_gdn_tile = TM_GDN // TM
    tile = lambda width: pl.BlockSpec((TM, width), lambda i: (i, 0))
    wide = jax.ShapeDtypeStruct((t, DN_WIDTH), F32)
    assert all(w.shape[0] % (16 * n_tiles) == 0 for w in later_weights)
    cast_specs = [pl.BlockSpec((w.shape[0] // n_tiles, w.shape[1]), lambda i: (i, 0)) for w in later_weights]
    outs = pl.pallas_call(
        functools.partial(_in_proj_kernel, seq // TM, len(later_weights)),
        grid=(n_tiles,),
        in_specs=[
            tile(D_MODEL),
            _const_spec((1, D_MODEL)),
            _const_spec((D_MODEL, 3 * DN_WIDTH)),
            _const_spec((D_MODEL, LANES)),
            _const_spec((D_MODEL, POOL_WIDTH + DN_WIDTH)),
            _const_spec((CONV_K, 3 * DN_WIDTH)),
            _const_spec((GATE_ROWS, LANES)),
            _const_spec((len(POOL_WINDOWS), POOL_GROUP_DIM, POOL_GROUP_DIM)),
            _const_spec((1, POOL_WIDTH)),
        ] + cast_specs,
        out_specs=[
            tile(DN_WIDTH), tile(DN_WIDTH), tile(DN_WIDTH),
            pl.BlockSpec((None, GATE_ROWS, TM), lambda i: (i // tiles_per_gdn_tile, 0, i % tiles_per_gdn_tile)),
            tile(LANES), tile(POOL_WIDTH), tile(DN_WIDTH),
        ] + cast_specs,
        out_shape=[
            wide, wide, wide,
            jax.ShapeDtypeStruct((t // TM_GDN, GATE_ROWS, TM_GDN), F32),
            jax.ShapeDtypeStruct((t, LANES), F32),
            jax.ShapeDtypeStruct((t, POOL_WIDTH), BF16),
            wide,
        ] + [jax.ShapeDtypeStruct(w.shape, BF16) for w in later_weights],
        scratch_shapes=[
            pltpu.VMEM((TM, D_MODEL), BF16),
            pltpu.VMEM((CONV_HALO + TM, 3 * DN_WIDTH), F32),
            pltpu.VMEM((TM, LANES), F32),
            pltpu.VMEM((POOL_HALO + TM, POOL_WIDTH), F32),
            pltpu.VMEM((TM, DN_WIDTH), F32),
        ],
        compiler_params=pltpu.CompilerParams(
            dimension_semantics=("arbitrary",), vmem_limit_bytes=VMEM_LIMIT),
        name="in_proj",
    )(x2, g, w_qkv, w_ba, w_uz, conv_w, gate_params, w_pool, pool_scale, *later_weights)
    return (*outs[:7], tuple(outs[7:]))


def _slot(chunk, head):
    return (chunk * DN_HEADS + head) % GDN_SLOTS


def _gdn_prep(chunks, q_ref, k_ref, v_ref, grow_ref, gcol_ref, ops):
    c = GDN_CHUNK
    ii = lax.broadcasted_iota(jnp.int32, (c, c), 0)
    jj = lax.broadcasted_iota(jnp.int32, (c, c), 1)
    tril = ii >= jj
    strict = ii > jj

    def col(r0, j):
        return jnp.broadcast_to(gcol_ref[r0:r0 + c, j:j + 1], (c, c))

    for ci in chunks:
        for h in range(DN_HEADS):
            r0 = ci * c
            lo = h * DN_HEAD_DIM
            qh = _silu(q_ref[r0:r0 + c, lo:lo + DN_HEAD_DIM])
            kh = _silu(k_ref[r0:r0 + c, lo:lo + DN_HEAD_DIM])
            vh = _silu(v_ref[r0:r0 + c, lo:lo + DN_HEAD_DIM])
            qn = qh * (lax.rsqrt(jnp.sum(qh * qh, axis=-1, keepdims=True) + EPS) * (DN_HEAD_DIM ** -0.5))
            kn = kh * lax.rsqrt(jnp.sum(kh * kh, axis=-1, keepdims=True) + EPS)
            beta_b = col(r0, h)
            gc_b = col(r0, DN_HEADS + h)
            gc_r = grow_ref[DN_HEADS + h:DN_HEADS + h + 1, r0:r0 + c]
            g_last = gcol_ref[r0 + c - 1:r0 + c, DN_HEADS + h:DN_HEADS + h + 1]
            d = jnp.where(tril, jnp.exp(gc_b - gc_r), 0.0)
            eg = jnp.exp(gc_b)
            kb = kn * beta_b
            kt = kn.T
            ops["unit"].append((ci, h))
            ops["dec"].append(d)
            ops["neg_d"].append(jnp.where(strict, -d, 0.0))
            ops["kt16"].append(kt.astype(BF16))
            ops["lhs_kq"].append(jnp.concatenate([kb, qn], axis=0).astype(BF16))
            ops["vbkg"].append(jnp.concatenate([vh * beta_b, kb * eg], axis=1).astype(BF16))
            ops["qg16"].append((qn * eg).astype(BF16))
            ops["kdt16"].append((kt * jnp.exp(g_last - gc_r)).astype(BF16))
            if h % 2 == 1:
                yield


def _gdn_stages(ops, u_ref, wq_ref, ak_ref):
    c = GDN_CHUNK
    ii = lax.broadcasted_iota(jnp.int32, (c, c), 0)
    jj = lax.broadcasted_iota(jnp.int32, (c, c), 1)
    eye = jnp.where(ii == jj, 1.0, 0.0).astype(F32)
    n = len(ops["unit"])
    kq = [_dot(ops["lhs_kq"][i], ops["kt16"][i]) for i in range(n)]
    yield
    xs = [kq[i][:c] * ops["neg_d"][i] for i in range(n)]
    ps = [eye + xs[i] for i in range(n)]
    att16 = [(kq[i][c:] * ops["dec"][i]).astype(BF16) for i in range(n)]
    x16 = [x.astype(BF16) for x in xs]
    x16 = [_dot(x, x).astype(BF16) for x in x16]
    yield
    span = 2
    while 2 * span < c:
        both = [_dot(jnp.concatenate([ps[i].astype(BF16), x16[i]], axis=0), x16[i]) for i in range(n)]
        ps = [ps[i] + both[i][:c] for i in range(n)]
        x16 = [both[i][c:].astype(BF16) for i in range(n)]
        span *= 2
        yield
    ps = [ps[i] + _dot(ps[i].astype(BF16), x16[i]) for i in range(n)]
    yield
    uw = [_dot(ps[i].astype(BF16), ops["vbkg"][i]) for i in range(n)]
    for i, (ci, h) in enumerate(ops["unit"]):
        idx = _slot(ci, h)
        u_ref[idx] = uw[i][:, :DN_HEAD_DIM]
        wq_ref[idx] = jnp.concatenate([uw[i][:, DN_HEAD_DIM:].astype(BF16), ops["qg16"][i]], axis=0)
        ak_ref[idx] = jnp.concatenate([att16[i], ops["kdt16"][i]], axis=0)
    yield


def _gdn_phase2(chunks, gcol_ref, sz_ref, og_ref, state_ref, u_ref, wq_ref, ak_ref, y_ref):
    c = GDN_CHUNK
    heads = range(DN_HEADS)
    for ci in chunks:
        r0 = ci * c
        st = [state_ref[h] for h in heads]
        st16 = [s.astype(BF16) for s in st]
        r1 = [_dot(wq_ref[_slot(ci, h)], st16[h]) for h in heads]
        yield
        vn16 = [(u_ref[_slot(ci, h)] - r1[h][:c]).astype(BF16) for h in heads]
        r2 = [_dot(ak_ref[_slot(ci, h)], vn16[h]) for h in heads]
        yield
        for h in heads:
            lo = h * DN_HEAD_DIM
            g_last = gcol_ref[r0 + c - 1:r0 + c, DN_HEADS + h:DN_HEADS + h + 1]
            state_ref[h] = st[h] * jnp.exp(g_last) + r2[h][c:]
            o = r1[h][c:] + r2[h][:c]
            on = o * lax.rsqrt(jnp.mean(o * o, axis=-1, keepdims=True) + EPS) * og_ref[...]
            y_ref[r0:r0 + c, lo:lo + DN_HEAD_DIM] = (on * sz_ref[r0:r0 + c, lo:lo + DN_HEAD_DIM]).astype(y_ref.dtype)


def _gdn_kernel(q_ref, k_ref, v_ref, grow_ref, gcol_ref, sz_ref, og_ref, y_ref, state_ref, u_ref, wq_ref, ak_ref):
    @pl.when(pl.program_id(1) == 0)
    def _():
        state_ref[...] = jnp.zeros_like(state_ref)

    n_chunks = TM_GDN // GDN_CHUNK
    sweeps = [range(g0, g0 + CHUNKS_PER_SWEEP) for g0 in range(0, n_chunks, CHUNKS_PER_SWEEP)]
    new_ops = lambda: {name: [] for name in ("unit", "dec", "neg_d", "kt16", "lhs_kq", "vbkg", "qg16", "kdt16")}
    prep = lambda chunks, ops: _gdn_prep(chunks, q_ref, k_ref, v_ref, grow_ref, gcol_ref, ops)
    phase2 = lambda chunks: _gdn_phase2(chunks, gcol_ref, sz_ref, og_ref, state_ref, u_ref, wq_ref, ak_ref, y_ref)
    ops = new_ops()
    for _ in prep(sweeps[0], ops):
        pass
    for i, sweep in enumerate(sweeps):
        next_ops = new_ops()
        side = []
        if i + 1 < len(sweeps):
            side.append(prep(sweeps[i + 1], next_ops))
        if i > 0:
            side.append(phase2(sweeps[i - 1]))
        _interleave(_gdn_stages(ops, u_ref, wq_ref, ak_ref), _round_robin(side))
        ops = next_ops
    for _ in phase2(sweeps[-1]):
        pass


def _gdn(q2, k2, v2, grow3, gcol2, sz2, o_norm_g, seq):
    t = q2.shape[0]
    b = t // seq
    tiles_per_batch = seq // TM_GDN
    tile = lambda width: pl.BlockSpec((TM_GDN, width), lambda bi, si: (bi * tiles_per_batch + si, 0))
    return pl.pallas_call(
        _gdn_kernel,
        grid=(b, tiles_per_batch),
        in_specs=[tile(DN_WIDTH), tile(DN_WIDTH), tile(DN_WIDTH),
                  pl.BlockSpec((None, GATE_ROWS, TM_GDN), lambda bi, si: (bi * tiles_per_batch + si, 0, 0)),
                  tile(LANES), tile(DN_WIDTH),
                  _const_spec((1, DN_HEAD_DIM))],
        out_specs=tile(DN_WIDTH),
        out_shape=jax.ShapeDtypeStruct((t, DN_WIDTH), BF16),
        scratch_shapes=[
            pltpu.VMEM((DN_HEADS, DN_HEAD_DIM, DN_HEAD_DIM), F32),
            pltpu.VMEM((GDN_SLOTS, GDN_CHUNK, DN_HEAD_DIM), F32),
            pltpu.VMEM((GDN_SLOTS, 2 * GDN_CHUNK, DN_HEAD_DIM), BF16),
            pltpu.VMEM((GDN_SLOTS, 2 * GDN_CHUNK, DN_HEAD_DIM), BF16),
        ],
        compiler_params=pltpu.CompilerParams(
            dimension_semantics=("arbitrary", "arbitrary"), vmem_limit_bytes=VMEM_LIMIT),
        name="gdn",
    )(q2, k2, v2, grow3, gcol2, sz2, o_norm_g)


def _mem_kv_kernel(m_ref, g_ref, wk_ref, wv_ref, kt_ref, v_ref):
    n_batches, mem_len, _ = m_ref.shape
    mn = _rms(m_ref[...].reshape(n_batches * mem_len, D_MODEL), g_ref[...]).astype(BF16)
    k = _dot(mn, wk_ref[...])
    v = _dot(mn, wv_ref[...])
    for bi in range(n_batches):
        rows = slice(bi * mem_len, (bi + 1) * mem_len)
        kt_ref[bi] = k[rows, :].T.astype(kt_ref.dtype)
        v_ref[bi] = v[rows, :].astype(v_ref.dtype)


def _mem_kv(mem, g, w_k, w_v):
    b, mem_len, _ = mem.shape
    nb = MEM_BATCHES_PER_STEP
    return pl.pallas_call(
        _mem_kv_kernel,
        grid=(b // nb,),
        in_specs=[
            pl.BlockSpec((nb, mem_len, D_MODEL), lambda i: (i, 0, 0)),
            _const_spec((1, D_MODEL)),
            _const_spec((D_MODEL, D_MODEL)),
            _const_spec((D_MODEL, D_MODEL)),
        ],
        out_specs=[
            pl.BlockSpec((nb, D_MODEL, mem_len), lambda i: (i, 0, 0)),
            pl.BlockSpec((nb, mem_len, D_MODEL), lambda i: (i, 0, 0)),
        ],
        out_shape=[
            jax.ShapeDtypeStruct((b, D_MODEL, mem_len), BF16),
            jax.ShapeDtypeStruct((b, mem_len, D_MODEL), BF16),
        ],
        compiler_params=pltpu.CompilerParams(
            dimension_semantics=("arbitrary",), vmem_limit_bytes=VMEM_LIMIT),
        name="mem_kv",
    )(mem, g, w_k, w_v)


def _post_rows(rows, x_ref, ypool_ref, ydn_ref, kt_ref, v_ref, wout_ref, gxa_ref, wq_ref, wo_ref, gmlp_ref,
               wup_ref, wdn_ref, gfin_ref, out_ref):
    y = jnp.concatenate([ypool_ref[rows, :], ydn_ref[rows, :]], axis=-1)
    h1 = x_ref[rows, :] + _dot(y, wout_ref[...])
    yield
    q = _dot(_rms(h1, gxa_ref[...]).astype(BF16), wq_ref[...]).astype(BF16)
    yield
    head_lanes = [slice(hd * XA_HEAD_DIM, (hd + 1) * XA_HEAD_DIM) for hd in range(XA_HEADS)]
    scores = [_dot(q[:, hl], kt_ref[hl, :]) * (XA_HEAD_DIM ** -0.5) for hl in head_lanes]
    yield
    expd = [jnp.exp(s - jnp.max(s, axis=-1, keepdims=True)) for s in scores]
    pv = [_dot(e.astype(BF16), v_ref[:, hl]) for e, hl in zip(expd, head_lanes)]
    yield
    att = jnp.concatenate([(o / jnp.sum(e, axis=-1, keepdims=True)).astype(BF16) for e, o in zip(expd, pv)],
                          axis=-1)
    h2 = h1 + _dot(att, wo_ref[...])
    yield
    hn2 = _rms(h2, gmlp_ref[...]).astype(BF16)
    acc = h2
    for j in range(D_FF // FF_BLOCK):
        a = jnp.maximum(_dot(hn2, wup_ref[:, j * FF_BLOCK:(j + 1) * FF_BLOCK]), 0.0)
        yield
        acc = acc + _dot((a * a).astype(BF16), wdn_ref[j * FF_BLOCK:(j + 1) * FF_BLOCK, :])
        yield
    out_ref[rows, :] = _rms(acc, gfin_ref[...])


def _post_kernel(*refs):
    blocks = [slice(r0, r0 + POST_ROW_BLOCK) for r0 in range(0, TM_POST, POST_ROW_BLOCK)]
    for _ in _round_robin([_post_rows(rows, *refs) for rows in blocks]):
        pass


def _post(x2, y_pool, y_dn, kt3, v3, w_out, g_xa, w_q, w_o, g_mlp, w_up, w_dn, g_fin, seq):
    t = x2.shape[0]
    mem_len = v3.shape[1]
    tiles_per_batch = seq // TM_POST
    tile = pl.BlockSpec((TM_POST, D_MODEL), lambda i: (i, 0))
    half_tile = pl.BlockSpec((TM_POST, D_MODEL // 2), lambda i: (i, 0))
    return pl.pallas_call(
        _post_kernel,
        grid=(t // TM_POST,),
        in_specs=[
            tile, half_tile, half_tile,
            pl.BlockSpec((None, D_MODEL, mem_len), lambda i: (i // tiles_per_batch, 0, 0)),
            pl.BlockSpec((None, mem_len, D_MODEL), lambda i: (i // tiles_per_batch, 0, 0)),
            _const_spec((D_MODEL, D_MODEL)),
            _const_spec((1, D_MODEL)),
            _const_spec((D_MODEL, D_MODEL)),
            _const_spec((D_MODEL, D_MODEL)),
            _const_spec((1, D_MODEL)),
            _const_spec((D_MODEL, D_FF)),
            _const_spec((D_FF, D_MODEL)),
            _const_spec((1, D_MODEL)),
        ],
        out_specs=tile,
        out_shape=jax.ShapeDtypeStruct((t, D_MODEL), F32),
        compiler_params=pltpu.CompilerParams(
            dimension_semantics=("arbitrary",), vmem_limit_bytes=VMEM_LIMIT),
        name="post",
    )(x2, y_pool, y_dn, kt3, v3, w_out, g_xa, w_q, w_o, g_mlp, w_up, w_dn, g_fin)


def kernel(x, mem, norm_mix_g, w_in, w_pool, pool_scale, conv_w, a_log, dt_bias, dn_out_norm_g, w_out,
           norm_xattn_g, mem_norm_g, w_xq, w_xk, w_xv, w_xo, norm_mlp_g, w_up, w_down, final_norm_g):
    b, s, d = x.shape
    assert w_in.shape[0] == 1, "single-layer problem: the final RMSNorm is fused into the layer's post stage"
    row = lambda v: v.reshape(1, -1).astype(F32)

    w_qkv = w_in[0, :, QKV_OFF:QKV_OFF + 3 * DN_WIDTH].astype(BF16)
    w_uz = jnp.concatenate([w_in[0, :, :POOL_WIDTH].astype(BF16), w_in[0, :, Z_OFF:Z_OFF + DN_WIDTH].astype(BF16)],
                           axis=1)
    w_ba = jnp.pad(w_in[0, :, MAIN_COLS:].astype(BF16), ((0, 0), (0, LANES - 2 * DN_HEADS)))
    gate_params = jnp.zeros((GATE_ROWS, LANES), F32)
    gate_params = gate_params.at[DN_HEADS:, 0].set(a_log[0]).at[DN_HEADS:, 1].set(dt_bias[0])

    x2 = x.reshape(b * s, d)
    later_weights = [w_xk[0], w_xv[0], w_out[0], w_xq[0], w_xo[0], w_up[0], w_down[0]]
    q, k, v, grow, gcol, y_pool, sz, later16 = _in_proj(
        x2, row(norm_mix_g[0]), w_qkv, w_ba, w_uz, conv_w[0].astype(F32), gate_params, w_pool[0].astype(BF16),
        row(pool_scale[0]), later_weights, s)
    w_xk16, w_xv16, w_out16, w_xq16, w_xo16, w_up16, w_down16 = later16
    y_dn = _gdn(q, k, v, grow, gcol, sz, row(dn_out_norm_g[0]), s)
    kt3, v3 = _mem_kv(mem, row(mem_norm_g[0]), w_xk16, w_xv16)
    out = _post(x2, y_pool, y_dn, kt3, v3, w_out16, row(norm_xattn_g[0]),
                w_xq16, w_xo16, row(norm_mlp_g[0]), w_up16, w_down16, row(final_norm_g), s)
    return out.reshape(b, s, d)
```

```python
import functools

import jax
import jax.numpy as jnp
from jax import lax
from jax.experimental import pallas as pl
from jax.experimental.pallas import tpu as pltpu

D_MODEL = 1024
POOL_WIDTH = 512
POOL_WINDOWS = (2, 4, 8, 16)
POOL_GROUP_DIM = 128
DN_WIDTH = 512
DN_HEADS = 4
DN_HEAD_DIM = 128
CONV_K = 4
MAIN_COLS = POOL_WIDTH + 4 * DN_WIDTH
QKV_OFF = POOL_WIDTH
Z_OFF = POOL_WIDTH + 3 * DN_WIDTH
XA_HEADS = 4
XA_HEAD_DIM = 256
D_FF = 4096
EPS = 1e-6

LANES = 128
GATE_ROWS = 2 * DN_HEADS
GDN_CHUNK = 128
POOL_HALO = 16
CONV_HALO = 8
TM = 1024
TM_GDN = 2048
TM_POST = 1024
POST_ROW_BLOCK = 512
CHUNKS_PER_SWEEP = 2
GDN_SLOTS = 2 * CHUNKS_PER_SWEEP * DN_HEADS
FF_BLOCK = 1024
MEM_BATCHES_PER_STEP = 4
_ROW_BLOCKS = TM // GDN_CHUNK
PREP_SEGMENTS_AFTER_IN_PROJ_SEGMENT = (0, 0, 1 + _ROW_BLOCKS, _ROW_BLOCKS, _ROW_BLOCKS, len(POOL_WINDOWS))
VMEM_LIMIT = 60 * 1024 * 1024

F32 = jnp.float32
BF16 = jnp.bfloat16


def _dot(a, b):
    return jnp.dot(a, b, preferred_element_type=F32)


def _rms(x, g):
    return x * lax.rsqrt(jnp.mean(x * x, axis=-1, keepdims=True) + EPS) * g


def _silu(x):
    return x * jax.nn.sigmoid(x)


def _const_spec(shape):
    zeros = (0,) * len(shape)
    return pl.BlockSpec(shape, lambda *_: zeros, pipeline_mode=pl.Buffered(1))


def _round_robin(generators):
    live = list(generators)
    while live:
        for g in list(live):
            try:
                yield next(g)
            except StopIteration:
                live.remove(g)


def _interleave(first, second):
    for _ in first:
        next(second, None)
    for _ in second:
        pass


def _pool_mixer(t0, wpool_ref, pscale_ref, y_ref, extp_ref):
    tm = y_ref.shape[0]
    t_glob = lax.broadcasted_iota(jnp.int32, (tm, POOL_GROUP_DIM), 0) + t0
    for gi, win in enumerate(POOL_WINDOWS):
        lo = gi * POOL_GROUP_DIM
        assert win & (win - 1) == 0 and win <= POOL_HALO, "doubling needs power-of-two windows within the halo"
        acc = extp_ref[:, lo:lo + POOL_GROUP_DIM]
        shift = 1
        while shift < win:
            acc = acc + pltpu.roll(acc, shift, axis=0)
            shift *= 2
        acc = acc[POOL_HALO:, :]
        u = extp_ref[POOL_HALO:POOL_HALO + tm, lo:lo + POOL_GROUP_DIM]
        cnt = jnp.minimum(t_glob + 1, win).astype(F32)
        pooled = acc / cnt - u
        mixed = _dot(pooled.astype(BF16), wpool_ref[gi])
        y_ref[:, lo:lo + POOL_GROUP_DIM] = (mixed * pscale_ref[:, lo:lo + POOL_GROUP_DIM]).astype(y_ref.dtype)
        yield
    extp_ref[0:POOL_HALO, :] = extp_ref[tm:tm + POOL_HALO, :]


def _in_proj_matmuls(x_ref, g_ref, wqkv_ref, wba_ref, wuz_ref, hn_ref, qkv_ref, ba_ref, extp_ref, z_ref):
    hn_ref[...] = _rms(x_ref[...], g_ref[...]).astype(BF16)
    ba_ref[...] = _dot(hn_ref[...], wba_ref[...])
    yield
    for part in range(3):
        lanes = slice(part * DN_WIDTH, (part + 1) * DN_WIDTH)
        qkv_ref[CONV_HALO:CONV_HALO + TM, lanes] = _dot(hn_ref[...], wqkv_ref[:, lanes])
        yield
    extp_ref[POOL_HALO:, :] = _dot(hn_ref[...], wuz_ref[:, 0:POOL_WIDTH])
    yield
    z_ref[...] = _dot(hn_ref[...], wuz_ref[:, POOL_WIDTH:POOL_WIDTH + DN_WIDTH])
    yield


def _in_proj_prep(qkv_ref, ba_ref, z_ref, convw_ref, gp_ref, q_ref, k_ref, v_ref, grow_ref, gcol_ref, sz_ref, pool):
    c = GDN_CHUNK

    bat = ba_ref[...].T[0:GATE_ROWS, :]
    gp = gp_ref[...]
    a_log = gp[:, 0:1]
    dt_bias = gp[:, 1:2]
    xa = bat + dt_bias
    softplus = jnp.maximum(xa, 0.0) + jnp.log1p(jnp.exp(-jnp.abs(xa)))
    cs = -jnp.exp(a_log) * softplus
    lane = lax.broadcasted_iota(jnp.int32, (GATE_ROWS, TM), 1) & (c - 1)
    sh = 1
    while sh < c:
        cs = cs + jnp.where(lane >= sh, pltpu.roll(cs, sh, axis=1), 0.0)
        sh *= 2
    row = lax.broadcasted_iota(jnp.int32, (GATE_ROWS, TM), 0)
    stack = jnp.where(row < DN_HEADS, jax.nn.sigmoid(bat), cs)
    grow_ref[...] = stack
    gcol_ref[...] = jnp.concatenate([stack, jnp.zeros((LANES - GATE_ROWS, TM), F32)], axis=0).T
    yield

    for part, dst in enumerate((q_ref, k_ref, v_ref)):
        for r0 in range(0, TM, c):
            for h in range(DN_HEADS):
                lo = h * DN_HEAD_DIM
                lanes = slice(part * DN_WIDTH + lo, part * DN_WIDTH + lo + DN_HEAD_DIM)
                xe = qkv_ref[r0:r0 + CONV_HALO + c, lanes]
                xc = convw_ref[CONV_K - 1:CONV_K, lanes] * xe
                for d in range(1, CONV_K):
                    xc = xc + convw_ref[CONV_K - 1 - d:CONV_K - d, lanes] * pltpu.roll(xe, d, axis=0)
                dst[r0:r0 + c, lo:lo + DN_HEAD_DIM] = xc[CONV_HALO:, :]
            yield
    yield from pool
    for r0 in range(0, TM, c):
        sz_ref[r0:r0 + c, :] = _silu(z_ref[r0:r0 + c, :])
        yield


def _in_proj_kernel(tiles_per_batch, n_cast, *refs):
    x_ref, g_ref, wqkv_ref, wba_ref, wuz_ref, convw_ref, gp_ref, wpool_ref, pscale_ref = refs[:9]
    cast_src = refs[9:9 + n_cast]
    q_ref, k_ref, v_ref, grow_ref, gcol_ref, ypool_ref, sz_ref = refs[9 + n_cast:16 + n_cast]
    cast_dst = refs[16 + n_cast:16 + 2 * n_cast]
    hn_ref, qkv_ref, ba_ref, extp_ref, z_ref = refs[16 + 2 * n_cast:]
    t_in_batch = lax.rem(pl.program_id(0), jnp.int32(tiles_per_batch))

    for src, dst in zip(cast_src, cast_dst):
        dst[...] = src[...].astype(dst.dtype)

    @pl.when(t_in_batch == 0)
    def _():
        qkv_ref[0:CONV_HALO, :] = jnp.zeros((CONV_HALO, 3 * DN_WIDTH), F32)
        extp_ref[0:POOL_HALO, :] = jnp.zeros((POOL_HALO, POOL_WIDTH), F32)

    @pl.when(t_in_batch != 0)
    def _():
        qkv_ref[0:CONV_HALO, :] = qkv_ref[TM:TM + CONV_HALO, :]

    matmuls = _in_proj_matmuls(x_ref, g_ref, wqkv_ref, wba_ref, wuz_ref, hn_ref, qkv_ref, ba_ref, extp_ref, z_ref)
    pool = _pool_mixer(t_in_batch * TM, wpool_ref, pscale_ref, ypool_ref, extp_ref)
    prep = _in_proj_prep(qkv_ref, ba_ref, z_ref, convw_ref, gp_ref, q_ref, k_ref, v_ref, grow_ref, gcol_ref,
                         sz_ref, pool)
    for n_prep_segments in PREP_SEGMENTS_AFTER_IN_PROJ_SEGMENT:
        next(matmuls)
        for _ in range(n_prep_segments):
            next(prep)
    done = object()
    assert next(matmuls, done) is done
    for _ in prep:
        pass


def _in_proj(x2, g, w_qkv, w_ba, w_uz, conv_w, gate_params, w_pool, pool_scale, later_weights, seq):
    t = x2.shape[0]
    n_tiles = t // TM
    tiles_per_gdn_tile = TM_GDN // TM
    tile = lambda width: pl.BlockSpec((TM, width), lambda i: (i, 0))
    wide = jax.ShapeDtypeStruct((t, DN_WIDTH), F32)
    assert all(w.shape[0] % (16 * n_tiles) == 0 for w in later_weights)
    cast_specs = [pl.BlockSpec((w.shape[0] // n_tiles, w.shape[1]), lambda i: (i, 0)) for w in later_weights]
    outs = pl.pallas_call(
        functools.partial(_in_proj_kernel, seq // TM, len(later_weights)),
        grid=(n_tiles,),
        in_specs=[
            tile(D_MODEL),
            _const_spec((1, D_MODEL)),
            _const_spec((D_MODEL, 3 * DN_WIDTH)),
            _const_spec((D_MODEL, LANES)),
            _const_spec((D_MODEL, POOL_WIDTH + DN_WIDTH)),
            _const_spec((CONV_K, 3 * DN_WIDTH)),
            _const_spec((GATE_ROWS, LANES)),
            _const_spec((len(POOL_WINDOWS), POOL_GROUP_DIM, POOL_GROUP_DIM)),
            _const_spec((1, POOL_WIDTH)),
        ] + cast_specs,
        out_specs=[
            tile(DN_WIDTH), tile(DN_WIDTH), tile(DN_WIDTH),
            pl.BlockSpec((None, GATE_ROWS, TM), lambda i: (i // tiles_per_gdn_tile, 0, i % tiles_per_gdn_tile)),
            tile(LANES), tile(POOL_WIDTH), tile(DN_WIDTH),
        ] + cast_specs,
        out_shape=[
            wide, wide, wide,
            jax.ShapeDtypeStruct((t // TM_GDN, GATE_ROWS, TM_GDN), F32),
            jax.ShapeDtypeStruct((t, LANES), F32),
            jax.ShapeDtypeStruct((t, POOL_WIDTH), BF16),
            wide,
        ] + [jax.ShapeDtypeStruct(w.shape, BF16) for w in later_weights],
        scratch_shapes=[
            pltpu.VMEM((TM, D_MODEL), BF16),
            pltpu.VMEM((CONV_HALO + TM, 3 * DN_WIDTH), F32),
            pltpu.VMEM((TM, LANES), F32),
            pltpu.VMEM((POOL_HALO + TM, POOL_WIDTH), F32),
            pltpu.VMEM((TM, DN_WIDTH), F32),
        ],
        compiler_params=pltpu.CompilerParams(
            dimension_semantics=("arbitrary",), vmem_limit_bytes=VMEM_LIMIT),
        name="in_proj",
    )(x2, g, w_qkv, w_ba, w_uz, conv_w, gate_params, w_pool, pool_scale, *later_weights)
    return (*outs[:7], tuple(outs[7:]))


def _slot(chunk, head):
    return (chunk * DN_HEADS + head) % GDN_SLOTS


def _gdn_prep(chunks, q_ref, k_ref, v_ref, grow_ref, gcol_ref, ops):
    c = GDN_CHUNK
    ii = lax.broadcasted_iota(jnp.int32, (c, c), 0)
    jj = lax.broadcasted_iota(jnp.int32, (c, c), 1)
    tril = ii >= jj
    strict = ii > jj

    def col(r0, j):
        return jnp.broadcast_to(gcol_ref[r0:r0 + c, j:j + 1], (c, c))

    for ci in chunks:
        for h in range(DN_HEADS):
            r0 = ci * c
            lo = h * DN_HEAD_DIM
            qh = _silu(q_ref[r0:r0 + c, lo:lo + DN_HEAD_DIM])
            kh = _silu(k_ref[r0:r0 + c, lo:lo + DN_HEAD_DIM])
            vh = _silu(v_ref[r0:r0 + c, lo:lo + DN_HEAD_DIM])
            qn = qh * (lax.rsqrt(jnp.sum(qh * qh, axis=-1, keepdims=True) + EPS) * (DN_HEAD_DIM ** -0.5))
            kn = kh * lax.rsqrt(jnp.sum(kh * kh, axis=-1, keepdims=True) + EPS)
            beta_b = col(r0, h)
            gc_b = col(r0, DN_HEADS + h)
            gc_r = grow_ref[DN_HEADS + h:DN_HEADS + h + 1, r0:r0 + c]
            g_last = gcol_ref[r0 + c - 1:r0 + c, DN_HEADS + h:DN_HEADS + h + 1]
            d = jnp.where(tril, jnp.exp(gc_b - gc_r), 0.0)
            eg = jnp.exp(gc_b)
            kb = kn * beta_b
            kt = kn.T
            ops["unit"].append((ci, h))
            ops["dec"].append(d)
            ops["neg_d"].append(jnp.where(strict, -d, 0.0))
            ops["kt16"].append(kt.astype(BF16))
            ops["lhs_kq"].append(jnp.concatenate([kb, qn], axis=0).astype(BF16))
            ops["vbkg"].append(jnp.concatenate([vh * beta_b, kb * eg], axis=1).astype(BF16))
            ops["qg16"].append((qn * eg).astype(BF16))
            ops["kdt16"].append((kt * jnp.exp(g_last - gc_r)).astype(BF16))
            if h % 2 == 1:
                yield


def _gdn_stages(ops, u_ref, wq_ref, ak_ref):
    c = GDN_CHUNK
    ii = lax.broadcasted_iota(jnp.int32, (c, c), 0)
    jj = lax.broadcasted_iota(jnp.int32, (c, c), 1)
    eye = jnp.where(ii == jj, 1.0, 0.0).astype(F32)
    n = len(ops["unit"])
    kq = [_dot(ops["lhs_kq"][i], ops["kt16"][i]) for i in range(n)]
    yield
    xs = [kq[i][:c] * ops["neg_d"][i] for i in range(n)]
    ps = [eye + xs[i] for i in range(n)]
    att16 = [(kq[i][c:] * ops["dec"][i]).astype(BF16) for i in range(n)]
    x16 = [x.astype(BF16) for x in xs]
    x16 = [_dot(x, x).astype(BF16) for x in x16]
    yield
    span = 2
    while 2 * span < c:
        both = [_dot(jnp.concatenate([ps[i].astype(BF16), x16[i]], axis=0), x16[i]) for i in range(n)]
        ps = [ps[i] + both[i][:c] for i in range(n)]
        x16 = [both[i][c:].astype(BF16) for i in range(n)]
        span *= 2
        yield
    ps = [ps[i] + _dot(ps[i].astype(BF16), x16[i]) for i in range(n)]
    yield
    uw = [_dot(ps[i].astype(BF16), ops["vbkg"][i]) for i in range(n)]
    for i, (ci, h) in enumerate(ops["unit"]):
        idx = _slot(ci, h)
        u_ref[idx] = uw[i][:, :DN_HEAD_DIM]
        wq_ref[idx] = jnp.concatenate([uw[i][:, DN_HEAD_DIM:].astype(BF16), ops["qg16"][i]], axis=0)
        ak_ref[idx] = jnp.concatenate([att16[i], ops["kdt16"][i]], axis=0)
    yield


def _gdn_phase2(chunks, gcol_ref, sz_ref, og_ref, state_ref, u_ref, wq_ref, ak_ref, y_ref):
    c = GDN_CHUNK
    heads = range(DN_HEADS)
    for ci in chunks:
        r0 = ci * c
        st = [state_ref[h] for h in heads]
        st16 = [s.astype(BF16) for s in st]
        r1 = [_dot(wq_ref[_slot(ci, h)], st16[h]) for h in heads]
        yield
        vn16 = [(u_ref[_slot(ci, h)] - r1[h][:c]).astype(BF16) for h in heads]
        r2 = [_dot(ak_ref[_slot(ci, h)], vn16[h]) for h in heads]
        yield
        for h in heads:
            lo = h * DN_HEAD_DIM
            g_last = gcol_ref[r0 + c - 1:r0 + c, DN_HEADS + h:DN_HEADS + h + 1]
            state_ref[h] = st[h] * jnp.exp(g_last) + r2[h][c:]
            o = r1[h][c:] + r2[h][:c]
            on = o * lax.rsqrt(jnp.mean(o * o, axis=-1, keepdims=True) + EPS) * og_ref[...]
            y_ref[r0:r0 + c, lo:lo + DN_HEAD_DIM] = (on * sz_ref[r0:r0 + c, lo:lo + DN_HEAD_DIM]).astype(y_ref.dtype)


def _gdn_kernel(q_ref, k_ref, v_ref, grow_ref, gcol_ref, sz_ref, og_ref, y_ref, state_ref, u_ref, wq_ref, ak_ref):
    @pl.when(pl.program_id(1) == 0)
    def _():
        state_ref[...] = jnp.zeros_like(state_ref)

    n_chunks = TM_GDN // GDN_CHUNK
    sweeps = [range(g0, g0 + CHUNKS_PER_SWEEP) for g0 in range(0, n_chunks, CHUNKS_PER_SWEEP)]
    new_ops = lambda: {name: [] for name in ("unit", "dec", "neg_d", "kt16", "lhs_kq", "vbkg", "qg16", "kdt16")}
    prep = lambda chunks, ops: _gdn_prep(chunks, q_ref, k_ref, v_ref, grow_ref, gcol_ref, ops)
    phase2 = lambda chunks: _gdn_phase2(chunks, gcol_ref, sz_ref, og_ref, state_ref, u_ref, wq_ref, ak_ref, y_ref)
    ops = new_ops()
    for _ in prep(sweeps[0], ops):
        pass
    for i, sweep in enumerate(sweeps):
        next_ops = new_ops()
        side = []
        if i + 1 < len(sweeps):
            side.append(prep(sweeps[i + 1], next_ops))
        if i > 0:
            side.append(phase2(sweeps[i - 1]))
        _interleave(_gdn_stages(ops, u_ref, wq_ref, ak_ref), _round_robin(side))
        ops = next_ops
    for _ in phase2(sweeps[-1]):
        pass


def _gdn(q2, k2, v2, grow3, gcol2, sz2, o_norm_g, seq):
    t = q2.shape[0]
    b = t // seq
    tiles_per_batch = seq // TM_GDN
    tile = lambda width: pl.BlockSpec((TM_GDN, width), lambda bi, si: (bi * tiles_per_batch + si, 0))
    return pl.pallas_call(
        _gdn_kernel,
        grid=(b, tiles_per_batch),
        in_specs=[tile(DN_WIDTH), tile(DN_WIDTH), tile(DN_WIDTH),
                  pl.BlockSpec((None, GATE_ROWS, TM_GDN), lambda bi, si: (bi * tiles_per_batch + si, 0, 0)),
                  tile(LANES), tile(DN_WIDTH),
                  _const_spec((1, DN_HEAD_DIM))],
        out_specs=tile(DN_WIDTH),
        out_shape=jax.ShapeDtypeStruct((t, DN_WIDTH), BF16),
        scratch_shapes=[
            pltpu.VMEM((DN_HEADS, DN_HEAD_DIM, DN_HEAD_DIM), F32),
            pltpu.VMEM((GDN_SLOTS, GDN_CHUNK, DN_HEAD_DIM), F32),
            pltpu.VMEM((GDN_SLOTS, 2 * GDN_CHUNK, DN_HEAD_DIM), BF16),
            pltpu.VMEM((GDN_SLOTS, 2 * GDN_CHUNK, DN_HEAD_DIM), BF16),
        ],
        compiler_params=pltpu.CompilerParams(
            dimension_semantics=("arbitrary", "arbitrary"), vmem_limit_bytes=VMEM_LIMIT),
        name="gdn",
    )(q2, k2, v2, grow3, gcol2, sz2, o_norm_g)


def _mem_kv_kernel(m_ref, g_ref, wk_ref, wv_ref, kt_ref, v_ref):
    n_batches, mem_len, _ = m_ref.shape
    mn = _rms(m_ref[...].reshape(n_batches * mem_len, D_MODEL), g_ref[...]).astype(BF16)
    k = _dot(mn, wk_ref[...])
    v = _dot(mn, wv_ref[...])
    for bi in range(n_batches):
        rows = slice(bi * mem_len, (bi + 1) * mem_len)
        kt_ref[bi] = k[rows, :].T.astype(kt_ref.dtype)
        v_ref[bi] = v[rows, :].astype(v_ref.dtype)


def _mem_kv(mem, g, w_k, w_v):
    b, mem_len, _ = mem.shape
    nb = MEM_BATCHES_PER_STEP
    assert b % nb == 0
    return pl.pallas_call(
        _mem_kv_kernel,
        grid=(b // nb,),
        in_specs=[
            pl.BlockSpec((nb, mem_len, D_MODEL), lambda i: (i, 0, 0)),
            _const_spec((1, D_MODEL)),
            _const_spec((D_MODEL, D_MODEL)),
            _const_spec((D_MODEL, D_MODEL)),
        ],
        out_specs=[
            pl.BlockSpec((nb, D_MODEL, mem_len), lambda i: (i, 0, 0)),
            pl.BlockSpec((nb, mem_len, D_MODEL), lambda i: (i, 0, 0)),
        ],
        out_shape=[
            jax.ShapeDtypeStruct((b, D_MODEL, mem_len), BF16),
            jax.ShapeDtypeStruct((b, mem_len, D_MODEL), BF16),
        ],
        compiler_params=pltpu.CompilerParams(
            dimension_semantics=("arbitrary",), vmem_limit_bytes=VMEM_LIMIT),
        name="mem_kv",
    )(mem, g, w_k, w_v)


def _post_rows(rows, x_ref, ypool_ref, ydn_ref, kt_ref, v_ref, wout_ref, gxa_ref, wq_ref, wo_ref, gmlp_ref,
               wup_ref, wdn_ref, gfin_ref, out_ref):
    y = jnp.concatenate([ypool_ref[rows, :], ydn_ref[rows, :]], axis=-1)
    h1 = x_ref[rows, :] + _dot(y, wout_ref[...])
    yield
    q = _dot(_rms(h1, gxa_ref[...]).astype(BF16), wq_ref[...]).astype(BF16)
    yield
    head_lanes = [slice(hd * XA_HEAD_DIM, (hd + 1) * XA_HEAD_DIM) for hd in range(XA_HEADS)]
    scores = [_dot(q[:, hl], kt_ref[hl, :]) * (XA_HEAD_DIM ** -0.5) for hl in head_lanes]
    yield
    expd = [jnp.exp(s - jnp.max(s, axis=-1, keepdims=True)) for s in scores]
    pv = [_dot(e.astype(BF16), v_ref[:, hl]) for e, hl in zip(expd, head_lanes)]
    yield
    att = jnp.concatenate([(o / jnp.sum(e, axis=-1, keepdims=True)).astype(BF16) for e, o in zip(expd, pv)],
                          axis=-1)
    h2 = h1 + _dot(att, wo_ref[...])
    yield
    hn2 = _rms(h2, gmlp_ref[...]).astype(BF16)
    acc = h2
    for j in range(D_FF // FF_BLOCK):
        a = jnp.maximum(_dot(hn2, wup_ref[:, j * FF_BLOCK:(j + 1) * FF_BLOCK]), 0.0)
        yield
        acc = acc + _dot((a * a).astype(BF16), wdn_ref[j * FF_BLOCK:(j + 1) * FF_BLOCK, :])
        yield
    out_ref[rows, :] = _rms(acc, gfin_ref[...])


def _post_kernel(*refs):
    blocks = [slice(r0, r0 + POST_ROW_BLOCK) for r0 in range(0, TM_POST, POST_ROW_BLOCK)]
    for _ in _round_robin([_post_rows(rows, *refs) for rows in blocks]):
        pass


def _post(x2, y_pool, y_dn, kt3, v3, w_out, g_xa, w_q, w_o, g_mlp, w_up, w_dn, g_fin, seq):
    t = x2.shape[0]
    mem_len = v3.shape[1]
    tiles_per_batch = seq // TM_POST
    tile = pl.BlockSpec((TM_POST, D_MODEL), lambda i: (i, 0))
    half_tile = pl.BlockSpec((TM_POST, D_MODEL // 2), lambda i: (i, 0))
    return pl.pallas_call(
        _post_kernel,
        grid=(t // TM_POST,),
        in_specs=[
            tile, half_tile, half_tile,
            pl.BlockSpec((None, D_MODEL, mem_len), lambda i: (i // tiles_per_batch, 0, 0)),
            pl.BlockSpec((None, mem_len, D_MODEL), lambda i: (i // tiles_per_batch, 0, 0)),
            _const_spec((D_MODEL, D_MODEL)),
            _const_spec((1, D_MODEL)),
            _const_spec((D_MODEL, D_MODEL)),
            _const_spec((D_MODEL, D_MODEL)),
            _const_spec((1, D_MODEL)),
            _const_spec((D_MODEL, D_FF)),
            _const_spec((D_FF, D_MODEL)),
            _const_spec((1, D_MODEL)),
        ],
        out_specs=tile,
        out_shape=jax.ShapeDtypeStruct((t, D_MODEL), F32),
        compiler_params=pltpu.CompilerParams(
            dimension_semantics=("arbitrary",), vmem_limit_bytes=VMEM_LIMIT),
        name="post",
    )(x2, y_pool, y_dn, kt3, v3, w_out, g_xa, w_q, w_o, g_mlp, w_up, w_dn, g_fin)


def kernel(x, mem, norm_mix_g, w_in, w_pool, pool_scale, conv_w, a_log, dt_bias, dn_out_norm_g, w_out,
           norm_xattn_g, mem_norm_g, w_xq, w_xk, w_xv, w_xo, norm_mlp_g, w_up, w_down, final_norm_g):
    b, s, d = x.shape
    assert w_in.shape[0] == 1, "single-layer problem: the final RMSNorm is fused into the layer's post stage"
    row = lambda v: v.reshape(1, -1).astype(F32)

    w_qkv = w_in[0, :, QKV_OFF:QKV_OFF + 3 * DN_WIDTH].astype(BF16)
    w_uz = jnp.concatenate([w_in[0, :, :POOL_WIDTH].astype(BF16), w_in[0, :, Z_OFF:Z_OFF + DN_WIDTH].astype(BF16)],
                           axis=1)
    w_ba = jnp.pad(w_in[0, :, MAIN_COLS:].astype(BF16), ((0, 0), (0, LANES - 2 * DN_HEADS)))
    gate_params = jnp.zeros((GATE_ROWS, LANES), F32)
    gate_params = gate_params.at[DN_HEADS:, 0].set(a_log[0]).at[DN_HEADS:, 1].set(dt_bias[0])

    x2 = x.reshape(b * s, d)
    later_weights = [w_xk[0], w_xv[0], w_out[0], w_xq[0], w_xo[0], w_up[0], w_down[0]]
    q, k, v, grow, gcol, y_pool, sz, later16 = _in_proj(
        x2, row(norm_mix_g[0]), w_qkv, w_ba, w_uz, conv_w[0].astype(F32), gate_params, w_pool[0].astype(BF16),
        row(pool_scale[0]), later_weights, s)
    w_xk16, w_xv16, w_out16, w_xq16, w_xo16, w_up16, w_down16 = later16
    y_dn = _gdn(q, k, v, grow, gcol, sz, row(dn_out_norm_g[0]), s)
    kt3, v3 = _mem_kv(mem, row(mem_norm_g[0]), w_xk16, w_xv16)
    out = _post(x2, y_pool, y_dn, kt3, v3, w_out16, row(norm_xattn_g[0]),
                w_xq16, w_xo16, row(norm_mlp_g[0]), w_up16, w_down16, row(final_norm_g), s)
    return out.reshape(b, s, d)
```

```python
import functools

import jax
import jax.numpy as jnp
from jax import lax
from jax.experimental import pallas as pl
from jax.experimental.pallas import tpu as pltpu

D_MODEL = 1024
POOL_WIDTH = 512
POOL_WINDOWS = (2, 4, 8, 16)
POOL_GROUP_DIM = 128
DN_WIDTH = 512
DN_HEADS = 4
DN_HEAD_DIM = 128
CONV_K = 4
MAIN_COLS = POOL_WIDTH + 4 * DN_WIDTH
QKV_OFF = POOL_WIDTH
Z_OFF = POOL_WIDTH + 3 * DN_WIDTH
XA_HEADS = 4
XA_HEAD_DIM = 256
D_FF = 4096
EPS = 1e-6

LANES = 128
GATE_ROWS = 2 * DN_HEADS
GDN_CHUNK = 128
POOL_HALO = 16
CONV_HALO = 8
TM = 1024
TM_GDN = 2048
TM_POST = 1024
POST_ROW_BLOCK = 512
CHUNKS_PER_SWEEP = 2
GDN_SLOTS = 2 * CHUNKS_PER_SWEEP * DN_HEADS
FF_BLOCK = 1024
MEM_BATCHES_PER_STEP = 4
_ROW_BLOCKS = TM // GDN_CHUNK
PREP_SEGMENTS_AFTER_IN_PROJ_SEGMENT = (0, 0, 1 + _ROW_BLOCKS, _ROW_BLOCKS, _ROW_BLOCKS, len(POOL_WINDOWS))
VMEM_LIMIT = 60 * 1024 * 1024

F32 = jnp.float32
BF16 = jnp.bfloat16


def _dot(a, b):
    return jnp.dot(a, b, preferred_element_type=F32)


def _rms(x, g):
    return x * lax.rsqrt(jnp.mean(x * x, axis=-1, keepdims=True) + EPS) * g


def _silu(x):
    return x * jax.nn.sigmoid(x)


def _const_spec(shape):
    zeros = (0,) * len(shape)
    return pl.BlockSpec(shape, lambda *_: zeros, pipeline_mode=pl.Buffered(1))


def _round_robin(generators):
    live = list(generators)
    while live:
        for g in list(live):
            try:
                yield next(g)
            except StopIteration:
                live.remove(g)


def _interleave(first, second):
    for _ in first:
        next(second, None)
    for _ in second:
        pass


def _pool_mixer(t0, wpool_ref, pscale_ref, y_ref, extp_ref):
    tm = y_ref.shape[0]
    t_glob = lax.broadcasted_iota(jnp.int32, (tm, POOL_GROUP_DIM), 0) + t0
    for gi, win in enumerate(POOL_WINDOWS):
        lo = gi * POOL_GROUP_DIM
        assert win & (win - 1) == 0 and win <= POOL_HALO, "doubling needs power-of-two windows within the halo"
        acc = extp_ref[:, lo:lo + POOL_GROUP_DIM]
        shift = 1
        while shift < win:
            acc = acc + pltpu.roll(acc, shift, axis=0)
            shift *= 2
        acc = acc[POOL_HALO:, :]
        u = extp_ref[POOL_HALO:POOL_HALO + tm, lo:lo + POOL_GROUP_DIM]
        cnt = jnp.minimum(t_glob + 1, win).astype(F32)
        pooled = acc / cnt - u
        mixed = _dot(pooled.astype(BF16), wpool_ref[gi])
        y_ref[:, lo:lo + POOL_GROUP_DIM] = (mixed * pscale_ref[:, lo:lo + POOL_GROUP_DIM]).astype(y_ref.dtype)
        yield
    extp_ref[0:POOL_HALO, :] = extp_ref[tm:tm + POOL_HALO, :]


def _in_proj_matmuls(x_ref, g_ref, wqkv_ref, wba_ref, wuz_ref, hn_ref, qkv_ref, ba_ref, extp_ref, z_ref):
    hn_ref[...] = _rms(x_ref[...], g_ref[...]).astype(BF16)
    ba_ref[...] = _dot(hn_ref[...], wba_ref[...])
    yield
    for part in range(3):
        lanes = slice(part * DN_WIDTH, (part + 1) * DN_WIDTH)
        qkv_ref[CONV_HALO:CONV_HALO + TM, lanes] = _dot(hn_ref[...], wqkv_ref[:, lanes])
        yield
    extp_ref[POOL_HALO:, :] = _dot(hn_ref[...], wuz_ref[:, 0:POOL_WIDTH])
    yield
    z_ref[...] = _dot(hn_ref[...], wuz_ref[:, POOL_WIDTH:POOL_WIDTH + DN_WIDTH])
    yield


def _in_proj_prep(qkv_ref, ba_ref, z_ref, convw_ref, gp_ref, q_ref, k_ref, v_ref, grow_ref, gcol_ref, sz_ref, pool):
    c = GDN_CHUNK

    bat = ba_ref[...].T[0:GATE_ROWS, :]
    gp = gp_ref[...]
    a_log = gp[:, 0:1]
    dt_bias = gp[:, 1:2]
    xa = bat + dt_bias
    softplus = jnp.maximum(xa, 0.0) + jnp.log1p(jnp.exp(-jnp.abs(xa)))
    cs = -jnp.exp(a_log) * softplus
    lane = lax.broadcasted_iota(jnp.int32, (GATE_ROWS, TM), 1) & (c - 1)
    sh = 1
    while sh < c:
        cs = cs + jnp.where(lane >= sh, pltpu.roll(cs, sh, axis=1), 0.0)
        sh *= 2
    row = lax.broadcasted_iota(jnp.int32, (GATE_ROWS, TM), 0)
    stack = jnp.where(row < DN_HEADS, jax.nn.sigmoid(bat), cs)
    grow_ref[...] = stack
    gcol_ref[...] = jnp.concatenate([stack, jnp.zeros((LANES - GATE_ROWS, TM), F32)], axis=0).T
    yield

    for part, dst in enumerate((q_ref, k_ref, v_ref)):
        for r0 in range(0, TM, c):
            for h in range(DN_HEADS):
                lo = h * DN_HEAD_DIM
                lanes = slice(part * DN_WIDTH + lo, part * DN_WIDTH + lo + DN_HEAD_DIM)
                xe = qkv_ref[r0:r0 + CONV_HALO + c, lanes]
                xc = convw_ref[CONV_K - 1:CONV_K, lanes] * xe
                for d in range(1, CONV_K):
                    xc = xc + convw_ref[CONV_K - 1 - d:CONV_K - d, lanes] * pltpu.roll(xe, d, axis=0)
                dst[r0:r0 + c, lo:lo + DN_HEAD_DIM] = xc[CONV_HALO:, :]
            yield
    yield from pool
    for r0 in range(0, TM, c):
        sz_ref[r0:r0 + c, :] = _silu(z_ref[r0:r0 + c, :])
        yield


def _in_proj_kernel(tiles_per_batch, n_cast, *refs):
    x_ref, g_ref, wqkv_ref, wba_ref, wuz_ref, convw_ref, gp_ref, wpool_ref, pscale_ref = refs[:9]
    cast_src = refs[9:9 + n_cast]
    q_ref, k_ref, v_ref, grow_ref, gcol_ref, ypool_ref, sz_ref = refs[9 + n_cast:16 + n_cast]
    cast_dst = refs[16 + n_cast:16 + 2 * n_cast]
    hn_ref, qkv_ref, ba_ref, extp_ref, z_ref = refs[16 + 2 * n_cast:]
    t_in_batch = lax.rem(pl.program_id(0), jnp.int32(tiles_per_batch))

    for src, dst in zip(cast_src, cast_dst):
        dst[...] = src[...].astype(dst.dtype)

    @pl.when(t_in_batch == 0)
    def _():
        qkv_ref[0:CONV_HALO, :] = jnp.zeros((CONV_HALO, 3 * DN_WIDTH), F32)
        extp_ref[0:POOL_HALO, :] = jnp.zeros((POOL_HALO, POOL_WIDTH), F32)

    @pl.when(t_in_batch != 0)
    def _():
        qkv_ref[0:CONV_HALO, :] = qkv_ref[TM:TM + CONV_HALO, :]

    matmuls = _in_proj_matmuls(x_ref, g_ref, wqkv_ref, wba_ref, wuz_ref, hn_ref, qkv_ref, ba_ref, extp_ref, z_ref)
    pool = _pool_mixer(t_in_batch * TM, wpool_ref, pscale_ref, ypool_ref, extp_ref)
    prep = _in_proj_prep(qkv_ref, ba_ref, z_ref, convw_ref, gp_ref, q_ref, k_ref, v_ref, grow_ref, gcol_ref,
                         sz_ref, pool)
    for n_prep_segments in PREP_SEGMENTS_AFTER_IN_PROJ_SEGMENT:
        next(matmuls)
        for _ in range(n_prep_segments):
            next(prep)
    done = object()
    assert next(matmuls, done) is done
    for _ in prep:
        pass


def _in_proj(x2, g, w_qkv, w_ba, w_uz, conv_w, gate_params, w_pool, pool_scale, later_weights, seq):
    t = x2.shape[0]
    n_tiles = t // TM
    tiles_per_gdn_tile = TM_GDN // TM
    tile = lambda width: pl.BlockSpec((TM, width), lambda i: (i, 0))
    wide = jax.ShapeDtypeStruct((t, DN_WIDTH), F32)
    assert all(w.shape[0] % (16 * n_tiles) == 0 for w in later_weights)
    cast_specs = [pl.BlockSpec((w.shape[0] // n_tiles, w.shape[1]), lambda i: (i, 0)) for w in later_weights]
    outs = pl.pallas_call(
        functools.partial(_in_proj_kernel, seq // TM, len(later_weights)),
        grid=(n_tiles,),
        in_specs=[
            tile(D_MODEL),
            _const_spec((1, D_MODEL)),
            _const_spec((D_MODEL, 3 * DN_WIDTH)),
            _const_spec((D_MODEL, LANES)),
            _const_spec((D_MODEL, POOL_WIDTH + DN_WIDTH)),
            _const_spec((CONV_K, 3 * DN_WIDTH)),
            _const_spec((GATE_ROWS, LANES)),
            _const_spec((len(POOL_WINDOWS), POOL_GROUP_DIM, POOL_GROUP_DIM)),
            _const_spec((1, POOL_WIDTH)),
        ] + cast_specs,
        out_specs=[
            tile(DN_WIDTH), tile(DN_WIDTH), tile(DN_WIDTH),
            pl.BlockSpec((None, GATE_ROWS, TM), lambda i: (i // tiles_per_gdn_tile, 0, i % tiles_per_gdn_tile)),
            tile(LANES), tile(POOL_WIDTH), tile(DN_WIDTH),
        ] + cast_specs,
        out_shape=[
            wide, wide, wide,
            jax.ShapeDtypeStruct((t // TM_GDN, GATE_ROWS, TM_GDN), F32),
            jax.ShapeDtypeStruct((t, LANES), F32),
            jax.ShapeDtypeStruct((t, POOL_WIDTH), BF16),
            wide,
        ] + [jax.ShapeDtypeStruct(w.shape, BF16) for w in later_weights],
        scratch_shapes=[
            pltpu.VMEM((TM, D_MODEL), BF16),
            pltpu.VMEM((CONV_HALO + TM, 3 * DN_WIDTH), F32),
            pltpu.VMEM((TM, LANES), F32),
            pltpu.VMEM((POOL_HALO + TM, POOL_WIDTH), F32),
            pltpu.VMEM((TM, DN_WIDTH), F32),
        ],
        compiler_params=pltpu.CompilerParams(
            dimension_semantics=("arbitrary",), vmem_limit_bytes=VMEM_LIMIT),
        name="in_proj",
    )(x2, g, w_qkv, w_ba, w_uz, conv_w, gate_params, w_pool, pool_scale, *later_weights)
    return (*outs[:7], tuple(outs[7:]))


def _slot(chunk, head):
    return (chunk * DN_HEADS + head) % GDN_SLOTS


def _gdn_prep(chunks, q_ref, k_ref, v_ref, grow_ref, gcol_ref, ops):
    c = GDN_CHUNK
    ii = lax.broadcasted_iota(jnp.int32, (c, c), 0)
    jj = lax.broadcasted_iota(jnp.int32, (c, c), 1)
    tril = ii >= jj
    strict = ii > jj

    def col(r0, j):
        return jnp.broadcast_to(gcol_ref[r0:r0 + c, j:j + 1], (c, c))

    for ci in chunks:
        for h in range(DN_HEADS):
            r0 = ci * c
            lo = h * DN_HEAD_DIM
            qh = _silu(q_ref[r0:r0 + c, lo:lo + DN_HEAD_DIM])
            kh = _silu(k_ref[r0:r0 + c, lo:lo + DN_HEAD_DIM])
            vh = _silu(v_ref[r0:r0 + c, lo:lo + DN_HEAD_DIM])
            qn = qh * (lax.rsqrt(jnp.sum(qh * qh, axis=-1, keepdims=True) + EPS) * (DN_HEAD_DIM ** -0.5))
            kn = kh * lax.rsqrt(jnp.sum(kh * kh, axis=-1, keepdims=True) + EPS)
            beta_b = col(r0, h)
            gc_b = col(r0, DN_HEADS + h)
            gc_r = grow_ref[DN_HEADS + h:DN_HEADS + h + 1, r0:r0 + c]
            g_last = gcol_ref[r0 + c - 1:r0 + c, DN_HEADS + h:DN_HEADS + h + 1]
            d = jnp.where(tril, jnp.exp(gc_b - gc_r), 0.0)
            eg = jnp.exp(gc_b)
            kb = kn * beta_b
            kt = kn.T
            ops["unit"].append((ci, h))
            ops["dec"].append(d)
            ops["neg_d"].append(jnp.where(strict, -d, 0.0))
            ops["kt16"].append(kt.astype(BF16))
            ops["lhs_kq"].append(jnp.concatenate([kb, qn], axis=0).astype(BF16))
            ops["vbkg"].append(jnp.concatenate([vh * beta_b, kb * eg], axis=1).astype(BF16))
            ops["qg16"].append((qn * eg).astype(BF16))
            ops["kdt16"].append((kt * jnp.exp(g_last - gc_r)).astype(BF16))
            if h % 2 == 1:
                yield


def _gdn_stages(ops, u_ref, wq_ref):
    c = GDN_CHUNK
    ii = lax.broadcasted_iota(jnp.int32, (c, c), 0)
    jj = lax.broadcasted_iota(jnp.int32, (c, c), 1)
    eye = jnp.where(ii == jj, 1.0, 0.0).astype(F32)
    n = len(ops["unit"])
    kq = [_dot(ops["lhs_kq"][i], ops["kt16"][i]) for i in range(n)]
    yield
    xs = [kq[i][:c] * ops["neg_d"][i] for i in range(n)]
    ps = [eye + xs[i] for i in range(n)]
    att16 = [(kq[i][c:] * ops["dec"][i]).astype(BF16) for i in range(n)]
    x16 = [x.astype(BF16) for x in xs]
    x16 = [_dot(x, x).astype(BF16) for x in x16]
    yield
    span = 2
    while 2 * span < c:
        both = [_dot(jnp.concatenate([ps[i].astype(BF16), x16[i]], axis=0), x16[i]) for i in range(n)]
        ps = [ps[i] + both[i][:c] for i in range(n)]
        x16 = [both[i][c:].astype(BF16) for i in range(n)]
        span *= 2
        yield
    ps = [ps[i] + _dot(ps[i].astype(BF16), x16[i]) for i in range(n)]
    yield
    uw = [_dot(ps[i].astype(BF16), ops["vbkg"][i]) for i in range(n)]
    yield
    fold = [_dot(jnp.concatenate([ops["kdt16"][i], att16[i]], axis=0), uw[i].astype(BF16)) for i in range(n)]
    for i, (ci, h) in enumerate(ops["unit"]):
        idx = _slot(ci, h)
        d = DN_HEAD_DIM
        u_ref[idx] = fold[i][:, :d]
        q_eff = ops["qg16"][i].astype(F32) - fold[i][c:, d:]
        wq_ref[idx] = jnp.concatenate([fold[i][:c, d:].astype(BF16), q_eff.astype(BF16)], axis=0)
    yield


def _gdn_phase2(chunks, gcol_ref, sz_ref, og_ref, state_ref, u_ref, wq_ref, y_ref):
    c = GDN_CHUNK
    heads = range(DN_HEADS)
    for ci in chunks:
        r0 = ci * c
        st = [state_ref[h] for h in heads]
        st16 = [s.astype(BF16) for s in st]
        r1 = [_dot(wq_ref[_slot(ci, h)], st16[h]) for h in heads]
        yield
        for h in heads:
            lo = h * DN_HEAD_DIM
            g_last = gcol_ref[r0 + c - 1:r0 + c, DN_HEADS + h:DN_HEADS + h + 1]
            folded = u_ref[_slot(ci, h)]
            state_ref[h] = st[h] * jnp.exp(g_last) - r1[h][:c] + folded[:c]
            o = r1[h][c:] + folded[c:]
            on = o * lax.rsqrt(jnp.mean(o * o, axis=-1, keepdims=True) + EPS) * og_ref[...]
            y_ref[r0:r0 + c, lo:lo + DN_HEAD_DIM] = (on * sz_ref[r0:r0 + c, lo:lo + DN_HEAD_DIM]).astype(y_ref.dtype)


def _gdn_kernel(q_ref, k_ref, v_ref, grow_ref, gcol_ref, sz_ref, og_ref, y_ref, state_ref, u_ref, wq_ref):
    @pl.when(pl.program_id(1) == 0)
    def _():
        state_ref[...] = jnp.zeros_like(state_ref)

    n_chunks = TM_GDN // GDN_CHUNK
    sweeps = [range(g0, g0 + CHUNKS_PER_SWEEP) for g0 in range(0, n_chunks, CHUNKS_PER_SWEEP)]
    new_ops = lambda: {name: [] for name in ("unit", "dec", "neg_d", "kt16", "lhs_kq", "vbkg", "qg16", "kdt16")}
    prep = lambda chunks, ops: _gdn_prep(chunks, q_ref, k_ref, v_ref, grow_ref, gcol_ref, ops)
    phase2 = lambda chunks: _gdn_phase2(chunks, gcol_ref, sz_ref, og_ref, state_ref, u_ref, wq_ref, y_ref)
    ops = new_ops()
    for _ in prep(sweeps[0], ops):
        pass
    for i, sweep in enumerate(sweeps):
        next_ops = new_ops()
        side = []
        if i + 1 < len(sweeps):
            side.append(prep(sweeps[i + 1], next_ops))
        if i > 0:
            side.append(phase2(sweeps[i - 1]))
        _interleave(_gdn_stages(ops, u_ref, wq_ref), _round_robin(side))
        ops = next_ops
    for _ in phase2(sweeps[-1]):
        pass


def _gdn(q2, k2, v2, grow3, gcol2, sz2, o_norm_g, seq):
    t = q2.shape[0]
    b = t // seq
    tiles_per_batch = seq // TM_GDN
    tile = lambda width: pl.BlockSpec((TM_GDN, width), lambda bi, si: (bi * tiles_per_batch + si, 0))
    return pl.pallas_call(
        _gdn_kernel,
        grid=(b, tiles_per_batch),
        in_specs=[tile(DN_WIDTH), tile(DN_WIDTH), tile(DN_WIDTH),
                  pl.BlockSpec((None, GATE_ROWS, TM_GDN), lambda bi, si: (bi * tiles_per_batch + si, 0, 0)),
                  tile(LANES), tile(DN_WIDTH),
                  _const_spec((1, DN_HEAD_DIM))],
        out_specs=tile(DN_WIDTH),
        out_shape=jax.ShapeDtypeStruct((t, DN_WIDTH), BF16),
        scratch_shapes=[
            pltpu.VMEM((DN_HEADS, DN_HEAD_DIM, DN_HEAD_DIM), F32),
            pltpu.VMEM((GDN_SLOTS, 2 * GDN_CHUNK, DN_HEAD_DIM), F32),
            pltpu.VMEM((GDN_SLOTS, 2 * GDN_CHUNK, DN_HEAD_DIM), BF16),
        ],
        compiler_params=pltpu.CompilerParams(
            dimension_semantics=("arbitrary", "arbitrary"), vmem_limit_bytes=VMEM_LIMIT),
        name="gdn",
    )(q2, k2, v2, grow3, gcol2, sz2, o_norm_g)


def _mem_kv_kernel(m_ref, g_ref, wk_ref, wv_ref, kt_ref, v_ref):
    n_batches, mem_len, _ = m_ref.shape
    mn = _rms(m_ref[...].reshape(n_batches * mem_len, D_MODEL), g_ref[...]).astype(BF16)
    k = _dot(mn, wk_ref[...])
    v = _dot(mn, wv_ref[...])
    for bi in range(n_batches):
        rows = slice(bi * mem_len, (bi + 1) * mem_len)
        kt_ref[bi] = k[rows, :].T.astype(kt_ref.dtype)
        v_ref[bi] = v[rows, :].astype(v_ref.dtype)


def _mem_kv(mem, g, w_k, w_v):
    b, mem_len, _ = mem.shape
    nb = MEM_BATCHES_PER_STEP
    assert b % nb == 0
    return pl.pallas_call(
        _mem_kv_kernel,
        grid=(b // nb,),
        in_specs=[
            pl.BlockSpec((nb, mem_len, D_MODEL), lambda i: (i, 0, 0)),
            _const_spec((1, D_MODEL)),
            _const_spec((D_MODEL, D_MODEL)),
            _const_spec((D_MODEL, D_MODEL)),
        ],
        out_specs=[
            pl.BlockSpec((nb, D_MODEL, mem_len), lambda i: (i, 0, 0)),
            pl.BlockSpec((nb, mem_len, D_MODEL), lambda i: (i, 0, 0)),
        ],
        out_shape=[
            jax.ShapeDtypeStruct((b, D_MODEL, mem_len), BF16),
            jax.ShapeDtypeStruct((b, mem_len, D_MODEL), BF16),
        ],
        compiler_params=pltpu.CompilerParams(
            dimension_semantics=("arbitrary",), vmem_limit_bytes=VMEM_LIMIT),
        name="mem_kv",
    )(mem, g, w_k, w_v)


def _post_rows(rows, x_ref, ypool_ref, ydn_ref, kt_ref, v_ref, wout_ref, gxa_ref, wq_ref, wo_ref, gmlp_ref,
               wup_ref, wdn_ref, gfin_ref, out_ref):
    y = jnp.concatenate([ypool_ref[rows, :], ydn_ref[rows, :]], axis=-1)
    h1 = x_ref[rows, :] + _dot(y, wout_ref[...])
    yield
    q = _dot(_rms(h1, gxa_ref[...]).astype(BF16), wq_ref[...]).astype(BF16)
    yield
    head_lanes = [slice(hd * XA_HEAD_DIM, (hd + 1) * XA_HEAD_DIM) for hd in range(XA_HEADS)]
    scores = [_dot(q[:, hl], kt_ref[hl, :]) * (XA_HEAD_DIM ** -0.5) for hl in head_lanes]
    yield
    expd = [jnp.exp(s - jnp.max(s, axis=-1, keepdims=True)) for s in scores]
    pv = [_dot(e.astype(BF16), v_ref[:, hl]) for e, hl in zip(expd, head_lanes)]
    yield
    att = jnp.concatenate([(o / jnp.sum(e, axis=-1, keepdims=True)).astype(BF16) for e, o in zip(expd, pv)],
                          axis=-1)
    h2 = h1 + _dot(att, wo_ref[...])
    yield
    hn2 = _rms(h2, gmlp_ref[...]).astype(BF16)
    acc = h2
    for j in range(D_FF // FF_BLOCK):
        a = jnp.maximum(_dot(hn2, wup_ref[:, j * FF_BLOCK:(j + 1) * FF_BLOCK]), 0.0)
        yield
        acc = acc + _dot((a * a).astype(BF16), wdn_ref[j * FF_BLOCK:(j + 1) * FF_BLOCK, :])
        yield
    out_ref[rows, :] = _rms(acc, gfin_ref[...])


def _post_kernel(*refs):
    blocks = [slice(r0, r0 + POST_ROW_BLOCK) for r0 in range(0, TM_POST, POST_ROW_BLOCK)]
    for _ in _round_robin([_post_rows(rows, *refs) for rows in blocks]):
        pass


def _post(x2, y_pool, y_dn, kt3, v3, w_out, g_xa, w_q, w_o, g_mlp, w_up, w_dn, g_fin, seq):
    t = x2.shape[0]
    mem_len = v3.shape[1]
    tiles_per_batch = seq // TM_POST
    tile = pl.BlockSpec((TM_POST, D_MODEL), lambda i: (i, 0))
    half_tile = pl.BlockSpec((TM_POST, D_MODEL // 2), lambda i: (i, 0))
    return pl.pallas_call(
        _post_kernel,
        grid=(t // TM_POST,),
        in_specs=[
            tile, half_tile, half_tile,
            pl.BlockSpec((None, D_MODEL, mem_len), lambda i: (i // tiles_per_batch, 0, 0)),
            pl.BlockSpec((None, mem_len, D_MODEL), lambda i: (i // tiles_per_batch, 0, 0)),
            _const_spec((D_MODEL, D_MODEL)),
            _const_spec((1, D_MODEL)),
            _const_spec((D_MODEL, D_MODEL)),
            _const_spec((D_MODEL, D_MODEL)),
            _const_spec((1, D_MODEL)),
            _const_spec((D_MODEL, D_FF)),
            _const_spec((D_FF, D_MODEL)),
            _const_spec((1, D_MODEL)),
        ],
        out_specs=tile,
        out_shape=jax.ShapeDtypeStruct((t, D_MODEL), F32),
        compiler_params=pltpu.CompilerParams(
            dimension_semantics=("arbitrary",), vmem_limit_bytes=VMEM_LIMIT),
        name="post",
    )(x2, y_pool, y_dn, kt3, v3, w_out, g_xa, w_q, w_o, g_mlp, w_up, w_dn, g_fin)


def kernel(x, mem, norm_mix_g, w_in, w_pool, pool_scale, conv_w, a_log, dt_bias, dn_out_norm_g, w_out,
           norm_xattn_g, mem_norm_g, w_xq, w_xk, w_xv, w_xo, norm_mlp_g, w_up, w_down, final_norm_g):
    b, s, d = x.shape
    assert w_in.shape[0] == 1, "single-layer problem: the final RMSNorm is fused into the layer's post stage"
    row = lambda v: v.reshape(1, -1).astype(F32)

    w_qkv = w_in[0, :, QKV_OFF:QKV_OFF + 3 * DN_WIDTH].astype(BF16)
    w_uz = jnp.concatenate([w_in[0, :, :POOL_WIDTH].astype(BF16), w_in[0, :, Z_OFF:Z_OFF + DN_WIDTH].astype(BF16)],
                           axis=1)
    w_ba = jnp.pad(w_in[0, :, MAIN_COLS:].astype(BF16), ((0, 0), (0, LANES - 2 * DN_HEADS)))
    gate_params = jnp.zeros((GATE_ROWS, LANES), F32)
    gate_params = gate_params.at[DN_HEADS:, 0].set(a_log[0]).at[DN_HEADS:, 1].set(dt_bias[0])

    x2 = x.reshape(b * s, d)
    later_weights = [w_xk[0], w_xv[0], w_out[0], w_xq[0], w_xo[0], w_up[0], w_down[0]]
    q, k, v, grow, gcol, y_pool, sz, later16 = _in_proj(
        x2, row(norm_mix_g[0]), w_qkv, w_ba, w_uz, conv_w[0].astype(F32), gate_params, w_pool[0].astype(BF16),
        row(pool_scale[0]), later_weights, s)
    w_xk16, w_xv16, w_out16, w_xq16, w_xo16, w_up16, w_down16 = later16
    y_dn = _gdn(q, k, v, grow, gcol, sz, row(dn_out_norm_g[0]), s)
    kt3, v3 = _mem_kv(mem, row(mem_norm_g[0]), w_xk16, w_xv16)
    out = _post(x2, y_pool, y_dn, kt3, v3, w_out16, row(norm_xattn_g[0]),
                w_xq16, w_xo16, row(norm_mlp_g[0]), w_up16, w_down16, row(final_norm_g), s)
    return out.reshape(b, s, d)
```

```python
import functools

import jax
import jax.numpy as jnp
from jax import lax
from jax.experimental import pallas as pl
from jax.experimental.pallas import tpu as pltpu

D_MODEL = 1024
POOL_WIDTH = 512
POOL_WINDOWS = (2, 4, 8, 16)
POOL_GROUP_DIM = 128
DN_WIDTH = 512
DN_HEADS = 4
DN_HEAD_DIM = 128
CONV_K = 4
MAIN_COLS = POOL_WIDTH + 4 * DN_WIDTH
QKV_OFF = POOL_WIDTH
Z_OFF = POOL_WIDTH + 3 * DN_WIDTH
XA_HEADS = 4
XA_HEAD_DIM = 256
D_FF = 4096
EPS = 1e-6

LANES = 128
GATE_ROWS = 2 * DN_HEADS
GDN_CHUNK = 128
POOL_HALO = 16
CONV_HALO = 8
TM = 1024
TM_GDN = 2048
TM_POST = 1024
POST_ROW_BLOCK = 512
CHUNKS_PER_SWEEP = 2
GDN_SLOTS = 2 * CHUNKS_PER_SWEEP * DN_HEADS
FF_BLOCK = 1024
MEM_BATCHES_PER_STEP = 4
_ROW_BLOCKS = TM // GDN_CHUNK
PREP_SEGMENTS_AFTER_IN_PROJ_SEGMENT = (0, 0, 1 + _ROW_BLOCKS, _ROW_BLOCKS, _ROW_BLOCKS, len(POOL_WINDOWS))
VMEM_LIMIT = 60 * 1024 * 1024

F32 = jnp.float32
BF16 = jnp.bfloat16


def _dot(a, b):
    return jnp.dot(a, b, preferred_element_type=F32)


def _rms(x, g):
    return x * lax.rsqrt(jnp.mean(x * x, axis=-1, keepdims=True) + EPS) * g


def _silu(x):
    return x * jax.nn.sigmoid(x)


def _const_spec(shape):
    zeros = (0,) * len(shape)
    return pl.BlockSpec(shape, lambda *_: zeros, pipeline_mode=pl.Buffered(1))


def _round_robin(generators):
    live = list(generators)
    while live:
        for g in list(live):
            try:
                yield next(g)
            except StopIteration:
                live.remove(g)


def _interleave(first, second):
    for _ in first:
        next(second, None)
    for _ in second:
        pass


def _pool_mixer(t0, wpool_ref, pscale_ref, y_ref, extp_ref):
    tm = y_ref.shape[0]
    t_glob = lax.broadcasted_iota(jnp.int32, (tm, POOL_GROUP_DIM), 0) + t0
    for gi, win in enumerate(POOL_WINDOWS):
        lo = gi * POOL_GROUP_DIM
        assert win & (win - 1) == 0 and win <= POOL_HALO, "doubling needs power-of-two windows within the halo"
        acc = extp_ref[:, lo:lo + POOL_GROUP_DIM]
        shift = 1
        while shift < win:
            acc = acc + pltpu.roll(acc, shift, axis=0)
            shift *= 2
        acc = acc[POOL_HALO:, :]
        u = extp_ref[POOL_HALO:POOL_HALO + tm, lo:lo + POOL_GROUP_DIM]
        cnt = jnp.minimum(t_glob + 1, win).astype(F32)
        pooled = acc / cnt - u
        mixed = _dot(pooled.astype(BF16), wpool_ref[gi])
        y_ref[:, lo:lo + POOL_GROUP_DIM] = (mixed * pscale_ref[:, lo:lo + POOL_GROUP_DIM]).astype(y_ref.dtype)
        yield
    extp_ref[0:POOL_HALO, :] = extp_ref[tm:tm + POOL_HALO, :]


def _in_proj_matmuls(x_ref, g_ref, wqkv_ref, wba_ref, wuz_ref, hn_ref, qkv_ref, ba_ref, extp_ref, z_ref):
    hn_ref[...] = _rms(x_ref[...], g_ref[...]).astype(BF16)
    ba_ref[...] = _dot(hn_ref[...], wba_ref[...])
    yield
    for part in range(3):
        lanes = slice(part * DN_WIDTH, (part + 1) * DN_WIDTH)
        qkv_ref[CONV_HALO:CONV_HALO + TM, lanes] = _dot(hn_ref[...], wqkv_ref[:, lanes])
        yield
    extp_ref[POOL_HALO:, :] = _dot(hn_ref[...], wuz_ref[:, 0:POOL_WIDTH])
    yield
    z_ref[...] = _dot(hn_ref[...], wuz_ref[:, POOL_WIDTH:POOL_WIDTH + DN_WIDTH])
    yield


def _in_proj_prep(qkv_ref, ba_ref, z_ref, convw_ref, gp_ref, q_ref, k_ref, v_ref, grow_ref, gcol_ref, sz_ref, pool):
    c = GDN_CHUNK

    bat = ba_ref[...].T[0:GATE_ROWS, :]
    gp = gp_ref[...]
    a_log = gp[:, 0:1]
    dt_bias = gp[:, 1:2]
    xa = bat + dt_bias
    softplus = jnp.maximum(xa, 0.0) + jnp.log1p(jnp.exp(-jnp.abs(xa)))
    cs = -jnp.exp(a_log) * softplus
    lane = lax.broadcasted_iota(jnp.int32, (GATE_ROWS, TM), 1) & (c - 1)
    sh = 1
    while sh < c:
        cs = cs + jnp.where(lane >= sh, pltpu.roll(cs, sh, axis=1), 0.0)
        sh *= 2
    row = lax.broadcasted_iota(jnp.int32, (GATE_ROWS, TM), 0)
    stack = jnp.where(row < DN_HEADS, jax.nn.sigmoid(bat), cs)
    grow_ref[...] = stack
    gcol_ref[...] = jnp.concatenate([stack, jnp.zeros((LANES - GATE_ROWS, TM), F32)], axis=0).T
    yield

    for part, dst in enumerate((q_ref, k_ref, v_ref)):
        for r0 in range(0, TM, c):
            for h in range(DN_HEADS):
                lo = h * DN_HEAD_DIM
                lanes = slice(part * DN_WIDTH + lo, part * DN_WIDTH + lo + DN_HEAD_DIM)
                xe = qkv_ref[r0:r0 + CONV_HALO + c, lanes]
                xc = convw_ref[CONV_K - 1:CONV_K, lanes] * xe
                for d in range(1, CONV_K):
                    xc = xc + convw_ref[CONV_K - 1 - d:CONV_K - d, lanes] * pltpu.roll(xe, d, axis=0)
                dst[r0:r0 + c, lo:lo + DN_HEAD_DIM] = xc[CONV_HALO:, :]
            yield
    yield from pool
    for r0 in range(0, TM, c):
        sz_ref[r0:r0 + c, :] = _silu(z_ref[r0:r0 + c, :])
        yield


def _in_proj_kernel(tiles_per_batch, n_cast, *refs):
    x_ref, g_ref, wqkv_ref, wba_ref, wuz_ref, convw_ref, gp_ref, wpool_ref, pscale_ref = refs[:9]
    cast_src = refs[9:9 + n_cast]
    q_ref, k_ref, v_ref, grow_ref, gcol_ref, ypool_ref, sz_ref = refs[9 + n_cast:16 + n_cast]
    cast_dst = refs[16 + n_cast:16 + 2 * n_cast]
    hn_ref, qkv_ref, ba_ref, extp_ref, z_ref = refs[16 + 2 * n_cast:]
    t_in_batch = lax.rem(pl.program_id(0), jnp.int32(tiles_per_batch))

    for src, dst in zip(cast_src, cast_dst):
        dst[...] = src[...].astype(dst.dtype)

    @pl.when(t_in_batch == 0)
    def _():
        qkv_ref[0:CONV_HALO, :] = jnp.zeros((CONV_HALO, 3 * DN_WIDTH), F32)
        extp_ref[0:POOL_HALO, :] = jnp.zeros((POOL_HALO, POOL_WIDTH), F32)

    @pl.when(t_in_batch != 0)
    def _():
        qkv_ref[0:CONV_HALO, :] = qkv_ref[TM:TM + CONV_HALO, :]

    matmuls = _in_proj_matmuls(x_ref, g_ref, wqkv_ref, wba_ref, wuz_ref, hn_ref, qkv_ref, ba_ref, extp_ref, z_ref)
    pool = _pool_mixer(t_in_batch * TM, wpool_ref, pscale_ref, ypool_ref, extp_ref)
    prep = _in_proj_prep(qkv_ref, ba_ref, z_ref, convw_ref, gp_ref, q_ref, k_ref, v_ref, grow_ref, gcol_ref,
                         sz_ref, pool)
    for n_prep_segments in PREP_SEGMENTS_AFTER_IN_PROJ_SEGMENT:
        next(matmuls)
        for _ in range(n_prep_segments):
            next(prep)
    done = object()
    assert next(matmuls, done) is done
    for _ in prep:
        pass


def _in_proj(x2, g, w_qkv, w_ba, w_uz, conv_w, gate_params, w_pool, pool_scale, later_weights, seq):
    t = x2.shape[0]
    n_tiles = t // TM
    tiles_per_gdn_tile = TM_GDN // TM
    tile = lambda width: pl.BlockSpec((TM, width), lambda i: (i, 0))
    wide = jax.ShapeDtypeStruct((t, DN_WIDTH), F32)
    assert all(w.shape[0] % (16 * n_tiles) == 0 for w in later_weights)
    cast_specs = [pl.BlockSpec((w.shape[0] // n_tiles, w.shape[1]), lambda i: (i, 0)) for w in later_weights]
    outs = pl.pallas_call(
        functools.partial(_in_proj_kernel, seq // TM, len(later_weights)),
        grid=(n_tiles,),
        in_specs=[
            tile(D_MODEL),
            _const_spec((1, D_MODEL)),
            _const_spec((D_MODEL, 3 * DN_WIDTH)),
            _const_spec((D_MODEL, LANES)),
            _const_spec((D_MODEL, POOL_WIDTH + DN_WIDTH)),
            _const_spec((CONV_K, 3 * DN_WIDTH)),
            _const_spec((GATE_ROWS, LANES)),
            _const_spec((len(POOL_WINDOWS), POOL_GROUP_DIM, POOL_GROUP_DIM)),
            _const_spec((1, POOL_WIDTH)),
        ] + cast_specs,
        out_specs=[
            tile(DN_WIDTH), tile(DN_WIDTH), tile(DN_WIDTH),
            pl.BlockSpec((None, GATE_ROWS, TM), lambda i: (i // tiles_per_gdn_tile, 0, i % tiles_per_gdn_tile)),
            tile(LANES), tile(POOL_WIDTH), tile(DN_WIDTH),
        ] + cast_specs,
        out_shape=[
            wide, wide, wide,
            jax.ShapeDtypeStruct((t // TM_GDN, GATE_ROWS, TM_GDN), F32),
            jax.ShapeDtypeStruct((t, LANES), F32),
            jax.ShapeDtypeStruct((t, POOL_WIDTH), BF16),
            wide,
        ] + [jax.ShapeDtypeStruct(w.shape, BF16) for w in later_weights],
        scratch_shapes=[
            pltpu.VMEM((TM, D_MODEL), BF16),
            pltpu.VMEM((CONV_HALO + TM, 3 * DN_WIDTH), F32),
            pltpu.VMEM((TM, LANES), F32),
            pltpu.VMEM((POOL_HALO + TM, POOL_WIDTH), F32),
            pltpu.VMEM((TM, DN_WIDTH), F32),
        ],
        compiler_params=pltpu.CompilerParams(
            dimension_semantics=("arbitrary",), vmem_limit_bytes=VMEM_LIMIT),
        name="in_proj",
    )(x2, g, w_qkv, w_ba, w_uz, conv_w, gate_params, w_pool, pool_scale, *later_weights)
    return (*outs[:7], tuple(outs[7:]))


def _slot(chunk, head):
    return (chunk * DN_HEADS + head) % GDN_SLOTS


def _gdn_prep(chunks, q_ref, k_ref, v_ref, grow_ref, gcol_ref, ops):
    c = GDN_CHUNK
    ii = lax.broadcasted_iota(jnp.int32, (c, c), 0)
    jj = lax.broadcasted_iota(jnp.int32, (c, c), 1)
    tril = ii >= jj
    strict = ii > jj

    def col(r0, j):
        return jnp.broadcast_to(gcol_ref[r0:r0 + c, j:j + 1], (c, c))

    for ci in chunks:
        for h in range(DN_HEADS):
            r0 = ci * c
            lo = h * DN_HEAD_DIM
            qh = _silu(q_ref[r0:r0 + c, lo:lo + DN_HEAD_DIM])
            kh = _silu(k_ref[r0:r0 + c, lo:lo + DN_HEAD_DIM])
            vh = _silu(v_ref[r0:r0 + c, lo:lo + DN_HEAD_DIM])
            qn = qh * (lax.rsqrt(jnp.sum(qh * qh, axis=-1, keepdims=True) + EPS) * (DN_HEAD_DIM ** -0.5))
            kn = kh * lax.rsqrt(jnp.sum(kh * kh, axis=-1, keepdims=True) + EPS)
            beta_b = col(r0, h)
            gc_b = col(r0, DN_HEADS + h)
            gc_r = grow_ref[DN_HEADS + h:DN_HEADS + h + 1, r0:r0 + c]
            g_last = gcol_ref[r0 + c - 1:r0 + c, DN_HEADS + h:DN_HEADS + h + 1]
            d = jnp.where(tril, jnp.exp(gc_b - gc_r), 0.0)
            eg = jnp.exp(gc_b)
            kb = kn * beta_b
            kt = kn.T
            ops["unit"].append((ci, h))
            ops["dec"].append(d)
            ops["neg_d"].append(jnp.where(strict, -d, 0.0))
            ops["kt16"].append(kt.astype(BF16))
            ops["lhs_kq"].append(jnp.concatenate([kb, qn], axis=0).astype(BF16))
            ops["vbkg"].append(jnp.concatenate([vh * beta_b, kb * eg], axis=1).astype(BF16))
            ops["qg16"].append((qn * eg).astype(BF16))
            ops["kdt16"].append((kt * jnp.exp(g_last - gc_r)).astype(BF16))
            if h % 2 == 1:
                yield


def _gdn_stages(ops, u_ref, wq_ref, ak_ref):
    c = GDN_CHUNK
    ii = lax.broadcasted_iota(jnp.int32, (c, c), 0)
    jj = lax.broadcasted_iota(jnp.int32, (c, c), 1)
    eye = jnp.where(ii == jj, 1.0, 0.0).astype(F32)
    n = len(ops["unit"])
    kq = [_dot(ops["lhs_kq"][i], ops["kt16"][i]) for i in range(n)]
    yield
    xs = [kq[i][:c] * ops["neg_d"][i] for i in range(n)]
    ps = [eye + xs[i] for i in range(n)]
    att16 = [(kq[i][c:] * ops["dec"][i]).astype(BF16) for i in range(n)]
    x16 = [x.astype(BF16) for x in xs]
    x16 = [_dot(x, x).astype(BF16) for x in x16]
    yield
    span = 2
    while 2 * span < c:
        both = [_dot(jnp.concatenate([ps[i].astype(BF16), x16[i]], axis=0), x16[i]) for i in range(n)]
        ps = [ps[i] + both[i][:c] for i in range(n)]
        x16 = [both[i][c:].astype(BF16) for i in range(n)]
        span *= 2
        yield
    ps = [ps[i] + _dot(ps[i].astype(BF16), x16[i]) for i in range(n)]
    yield
    uw = [_dot(ps[i].astype(BF16), ops["vbkg"][i]) for i in range(n)]
    for i, (ci, h) in enumerate(ops["unit"]):
        idx = _slot(ci, h)
        u_ref[idx] = uw[i][:, :DN_HEAD_DIM]
        wq_ref[idx] = jnp.concatenate([uw[i][:, DN_HEAD_DIM:].astype(BF16), ops["qg16"][i]], axis=0)
        ak_ref[idx] = jnp.concatenate([att16[i], ops["kdt16"][i]], axis=0)
    yield


def _gdn_phase2(chunks, gcol_ref, sz_ref, og_ref, state_ref, u_ref, wq_ref, ak_ref, y_ref):
    c = GDN_CHUNK
    heads = range(DN_HEADS)
    for ci in chunks:
        r0 = ci * c
        st = [state_ref[h] for h in heads]
        st16 = [s.astype(BF16) for s in st]
        r1 = [_dot(wq_ref[_slot(ci, h)], st16[h]) for h in heads]
        yield
        vn16 = [(u_ref[_slot(ci, h)] - r1[h][:c]).astype(BF16) for h in heads]
        r2 = [_dot(ak_ref[_slot(ci, h)], vn16[h]) for h in heads]
        yield
        for h in heads:
            lo = h * DN_HEAD_DIM
            g_last = gcol_ref[r0 + c - 1:r0 + c, DN_HEADS + h:DN_HEADS + h + 1]
            state_ref[h] = st[h] * jnp.exp(g_last) + r2[h][c:]
            o = r1[h][c:] + r2[h][:c]
            on = o * lax.rsqrt(jnp.mean(o * o, axis=-1, keepdims=True) + EPS) * og_ref[...]
            y_ref[r0:r0 + c, lo:lo + DN_HEAD_DIM] = (on * sz_ref[r0:r0 + c, lo:lo + DN_HEAD_DIM]).astype(y_ref.dtype)


def _gdn_kernel(q_ref, k_ref, v_ref, grow_ref, gcol_ref, sz_ref, og_ref, y_ref, state_ref, u_ref, wq_ref, ak_ref):
    @pl.when(pl.program_id(1) == 0)
    def _():
        state_ref[...] = jnp.zeros_like(state_ref)

    n_chunks = TM_GDN // GDN_CHUNK
    sweeps = [range(g0, g0 + CHUNKS_PER_SWEEP) for g0 in range(0, n_chunks, CHUNKS_PER_SWEEP)]
    new_ops = lambda: {name: [] for name in ("unit", "dec", "neg_d", "kt16", "lhs_kq", "vbkg", "qg16", "kdt16")}
    prep = lambda chunks, ops: _gdn_prep(chunks, q_ref, k_ref, v_ref, grow_ref, gcol_ref, ops)
    phase2 = lambda chunks: _gdn_phase2(chunks, gcol_ref, sz_ref, og_ref, state_ref, u_ref, wq_ref, ak_ref, y_ref)
    ops = new_ops()
    for _ in prep(sweeps[0], ops):
        pass
    for i, sweep in enumerate(sweeps):
        next_ops = new_ops()
        side = []
        if i + 1 < len(sweeps):
            side.append(prep(sweeps[i + 1], next_ops))
        if i > 0:
            side.append(phase2(sweeps[i - 1]))
        _interleave(_gdn_stages(ops, u_ref, wq_ref, ak_ref), _round_robin(side))
        ops = next_ops
    for _ in phase2(sweeps[-1]):
        pass


def _gdn(q2, k2, v2, grow3, gcol2, sz2, o_norm_g, seq):
    t = q2.shape[0]
    b = t // seq
    tiles_per_batch = seq // TM_GDN
    tile = lambda width: pl.BlockSpec((TM_GDN, width), lambda bi, si: (bi * tiles_per_batch + si, 0))
    return pl.pallas_call(
        _gdn_kernel,
        grid=(b, tiles_per_batch),
        in_specs=[tile(DN_WIDTH), tile(DN_WIDTH), tile(DN_WIDTH),
                  pl.BlockSpec((None, GATE_ROWS, TM_GDN), lambda bi, si: (bi * tiles_per_batch + si, 0, 0)),
                  tile(LANES), tile(DN_WIDTH),
                  _const_spec((1, DN_HEAD_DIM))],
        out_specs=tile(DN_WIDTH),
        out_shape=jax.ShapeDtypeStruct((t, DN_WIDTH), BF16),
        scratch_shapes=[
            pltpu.VMEM((DN_HEADS, DN_HEAD_DIM, DN_HEAD_DIM), F32),
            pltpu.VMEM((GDN_SLOTS, GDN_CHUNK, DN_HEAD_DIM), F32),
            pltpu.VMEM((GDN_SLOTS, 2 * GDN_CHUNK, DN_HEAD_DIM), BF16),
            pltpu.VMEM((GDN_SLOTS, 2 * GDN_CHUNK, DN_HEAD_DIM), BF16),
        ],
        compiler_params=pltpu.CompilerParams(
            dimension_semantics=("arbitrary", "arbitrary"), vmem_limit_bytes=VMEM_LIMIT),
        name="gdn",
    )(q2, k2, v2, grow3, gcol2, sz2, o_norm_g)


def _mem_kv_kernel(m_ref, g_ref, wk_ref, wv_ref, kt_ref, v_ref):
    n_batches, mem_len, _ = m_ref.shape
    mn = _rms(m_ref[...].reshape(n_batches * mem_len, D_MODEL), g_ref[...]).astype(BF16)
    k = _dot(mn, wk_ref[...])
    v = _dot(mn, wv_ref[...])
    for bi in range(n_batches):
        rows = slice(bi * mem_len, (bi + 1) * mem_len)
        kt_ref[bi] = k[rows, :].T.astype(kt_ref.dtype)
        v_ref[bi] = v[rows, :].astype(v_ref.dtype)


def _mem_kv(mem, g, w_k, w_v):
    b, mem_len, _ = mem.shape
    nb = MEM_BATCHES_PER_STEP
    assert b % nb == 0
    return pl.pallas_call(
        _mem_kv_kernel,
        grid=(b // nb,),
        in_specs=[
            pl.BlockSpec((nb, mem_len, D_MODEL), lambda i: (i, 0, 0)),
            _const_spec((1, D_MODEL)),
            _const_spec((D_MODEL, D_MODEL)),
            _const_spec((D_MODEL, D_MODEL)),
        ],
        out_specs=[
            pl.BlockSpec((nb, D_MODEL, mem_len), lambda i: (i, 0, 0)),
            pl.BlockSpec((nb, mem_len, D_MODEL), lambda i: (i, 0, 0)),
        ],
        out_shape=[
            jax.ShapeDtypeStruct((b, D_MODEL, mem_len), BF16),
            jax.ShapeDtypeStruct((b, mem_len, D_MODEL), BF16),
        ],
        compiler_params=pltpu.CompilerParams(
            dimension_semantics=("arbitrary",), vmem_limit_bytes=VMEM_LIMIT),
        name="mem_kv",
    )(mem, g, w_k, w_v)


def _post_rows(rows, x_ref, ypool_ref, ydn_ref, kt_ref, v_ref, wout_ref, gxa_ref, wq_ref, wo_ref, gmlp_ref,
               wup_ref, wdn_ref, gfin_ref, out_ref):
    y = jnp.concatenate([ypool_ref[rows, :], ydn_ref[rows, :]], axis=-1)
    h1 = x_ref[rows, :] + _dot(y, wout_ref[...])
    yield
    q = _dot(_rms(h1, gxa_ref[...]).astype(BF16), wq_ref[...]).astype(BF16)
    yield
    head_lanes = [slice(hd * XA_HEAD_DIM, (hd + 1) * XA_HEAD_DIM) for hd in range(XA_HEADS)]
    scores = [_dot(q[:, hl], kt_ref[hl, :]) * (XA_HEAD_DIM ** -0.5) for hl in head_lanes]
    yield
    expd = [jnp.exp(s - jnp.max(s, axis=-1, keepdims=True)) for s in scores]
    pv = [_dot(e.astype(BF16), v_ref[:, hl]) for e, hl in zip(expd, head_lanes)]
    yield
    att = jnp.concatenate([(o / jnp.sum(e, axis=-1, keepdims=True)).astype(BF16) for e, o in zip(expd, pv)],
                          axis=-1)
    h2 = h1 + _dot(att, wo_ref[...])
    yield
    hn2 = _rms(h2, gmlp_ref[...]).astype(BF16)
    acc = h2
    for j in range(D_FF // FF_BLOCK):
        a = jnp.maximum(_dot(hn2, wup_ref[:, j * FF_BLOCK:(j + 1) * FF_BLOCK]), 0.0)
        yield
        acc = acc + _dot((a * a).astype(BF16), wdn_ref[j * FF_BLOCK:(j + 1) * FF_BLOCK, :])
        yield
    for r0 in range(0, POST_ROW_BLOCK, GDN_CHUNK):
        out_ref[rows.start + r0:rows.start + r0 + GDN_CHUNK, :] = _rms(acc[r0:r0 + GDN_CHUNK, :], gfin_ref[...])


def _post_kernel(*refs):
    blocks = [slice(r0, r0 + POST_ROW_BLOCK) for r0 in range(0, TM_POST, POST_ROW_BLOCK)]
    for _ in _round_robin([_post_rows(rows, *refs) for rows in blocks]):
        pass


def _post(x2, y_pool, y_dn, kt3, v3, w_out, g_xa, w_q, w_o, g_mlp, w_up, w_dn, g_fin, seq):
    t = x2.shape[0]
    mem_len = v3.shape[1]
    tiles_per_batch = seq // TM_POST
    tile = pl.BlockSpec((TM_POST, D_MODEL), lambda i: (i, 0))
    half_tile = pl.BlockSpec((TM_POST, D_MODEL // 2), lambda i: (i, 0))
    return pl.pallas_call(
        _post_kernel,
        grid=(t // TM_POST,),
        in_specs=[
            tile, half_tile, half_tile,
            pl.BlockSpec((None, D_MODEL, mem_len), lambda i: (i // tiles_per_batch, 0, 0)),
            pl.BlockSpec((None, mem_len, D_MODEL), lambda i: (i // tiles_per_batch, 0, 0)),
            _const_spec((D_MODEL, D_MODEL)),
            _const_spec((1, D_MODEL)),
            _const_spec((D_MODEL, D_MODEL)),
            _const_spec((D_MODEL, D_MODEL)),
            _const_spec((1, D_MODEL)),
            _const_spec((D_MODEL, D_FF)),
            _const_spec((D_FF, D_MODEL)),
            _const_spec((1, D_MODEL)),
        ],
        out_specs=tile,
        out_shape=jax.ShapeDtypeStruct((t, D_MODEL), F32),
        compiler_params=pltpu.CompilerParams(
            dimension_semantics=("arbitrary",), vmem_limit_bytes=VMEM_LIMIT),
        name="post",
    )(x2, y_pool, y_dn, kt3, v3, w_out, g_xa, w_q, w_o, g_mlp, w_up, w_dn, g_fin)


def kernel(x, mem, norm_mix_g, w_in, w_pool, pool_scale, conv_w, a_log, dt_bias, dn_out_norm_g, w_out,
           norm_xattn_g, mem_norm_g, w_xq, w_xk, w_xv, w_xo, norm_mlp_g, w_up, w_down, final_norm_g):
    b, s, d = x.shape
    assert w_in.shape[0] == 1, "single-layer problem: the final RMSNorm is fused into the layer's post stage"
    row = lambda v: v.reshape(1, -1).astype(F32)

    w_qkv = w_in[0, :, QKV_OFF:QKV_OFF + 3 * DN_WIDTH].astype(BF16)
    w_uz = jnp.concatenate([w_in[0, :, :POOL_WIDTH].astype(BF16), w_in[0, :, Z_OFF:Z_OFF + DN_WIDTH].astype(BF16)],
                           axis=1)
    w_ba = jnp.pad(w_in[0, :, MAIN_COLS:].astype(BF16), ((0, 0), (0, LANES - 2 * DN_HEADS)))
    gate_params = jnp.zeros((GATE_ROWS, LANES), F32)
    gate_params = gate_params.at[DN_HEADS:, 0].set(a_log[0]).at[DN_HEADS:, 1].set(dt_bias[0])

    x2 = x.reshape(b * s, d)
    later_weights = [w_xk[0], w_xv[0], w_out[0], w_xq[0], w_xo[0], w_up[0], w_down[0]]
    q, k, v, grow, gcol, y_pool, sz, later16 = _in_proj(
        x2, row(norm_mix_g[0]), w_qkv, w_ba, w_uz, conv_w[0].astype(F32), gate_params, w_pool[0].astype(BF16),
        row(pool_scale[0]), later_weights, s)
    w_xk16, w_xv16, w_out16, w_xq16, w_xo16, w_up16, w_down16 = later16
    y_dn = _gdn(q, k, v, grow, gcol, sz, row(dn_out_norm_g[0]), s)
    kt3, v3 = _mem_kv(mem, row(mem_norm_g[0]), w_xk16, w_xv16)
    out = _post(x2, y_pool, y_dn, kt3, v3, w_out16, row(norm_xattn_g[0]),
                w_xq16, w_xo16, row(norm_mlp_g[0]), w_up16, w_down16, row(final_norm_g), s)
    return out.reshape(b, s, d)
```
